```python
import math
import jax
import jax.numpy as jnp
from jax import lax
import numpy as np

D_MODEL = 1024
BATCH = 16
SEQ = 4096
DEPTH = 4

GRID_W = 64
N_MIXERS = 3
HEAD_DIM = 64
MIX_WIDTH = 768
MIX_HEADS = MIX_WIDTH // HEAD_DIM
MEM_LEN = 256
MEM_HEADS = 4
MEM_WIDTH = MEM_HEADS * HEAD_DIM
CAT_WIDTH = MIX_WIDTH + MEM_WIDTH

NA_KH_MAX = 8
NA_KW = 16

KV_HEADS = 4
KV_WIDTH = KV_HEADS * HEAD_DIM
ROPE_THETA = 10000.0
Q_BLOCK = 128
QK_NORM_EPS = 1e-6

HYENA_ORDER = 2
HYENA_BANDS = 16
HYENA_EMB = 2 * HYENA_BANDS + 1
HYENA_FFN = 64
HYENA_WIDTH = (HYENA_ORDER + 1) * MIX_WIDTH
HYENA_DECAY_TARGET = 1e-2
HYENA_FAST_DECAY = 0.3
HYENA_SLOW_DECAY = 1.5

N_EXPERTS = 32
TOP_K = 4
D_EXPERT = D_MODEL
SWIGLU_LIMIT = 7.0
SWIGLU_ALPHA = 1.702

DEEPNORM_ALPHA = (2 * DEPTH) ** 0.25
DEEPNORM_BETA = (8 * DEPTH) ** -0.25
LN_EPS = 1e-5

N_LAYERS_A = len(range(0, DEPTH, N_MIXERS))
N_LAYERS_B = len(range(1, DEPTH, N_MIXERS))
N_LAYERS_C = len(range(2, DEPTH, N_MIXERS))
IN_WIDTH_A = 3 * MIX_WIDTH + MEM_WIDTH
IN_WIDTH_B = MIX_WIDTH + 2 * KV_WIDTH + MEM_WIDTH
IN_WIDTH_C = HYENA_WIDTH + MEM_WIDTH

kernel_name = 'hybrid_na_gqa_hyena_moe_encoder'

F32 = jnp.float32


def layer_norm(x, g, b):
    xf = x.astype(F32)
    mu = jnp.mean(xf, -1, keepdims=True)
    xc = xf - mu
    var = jnp.mean(xc * xc, -1, keepdims=True)
    return (xc * lax.rsqrt(var + LN_EPS) * g.astype(F32) + b.astype(F32)).astype(x.dtype)


def rms_norm(x, g):
    xf = x.astype(F32)
    inv = lax.rsqrt(jnp.mean(xf * xf, -1, keepdims=True) + QK_NORM_EPS)
    return (xf * inv * g.astype(F32)).astype(x.dtype)


def split_heads(t, n_heads):
    return t.reshape(*t.shape[:-1], n_heads, HEAD_DIM)


def neighbourhood_attention(q, k, v, rpb):
    B, L, H, Dh = q.shape
    rows = L // GRID_W
    kh = min(NA_KH_MAX, rows)
    qg = q.reshape(B, rows, GRID_W, H, Dh)
    kg = k.reshape(B, rows, GRID_W, H, Dh)
    vg = v.reshape(B, rows, GRID_W, H, Dh)
    col = np.arange(GRID_W)
    col_start = np.clip(col - NA_KW // 2, 0, GRID_W - NA_KW)
    col_idx = col_start[:, None] + np.arange(NA_KW)[None, :]
    col_off = col_idx - col[:, None] + NA_KW - 1
    rpb_cols = rpb.astype(F32)[:, :, col_off]
    scale = Dh ** -0.5

    def row_block(r):
        r0 = jnp.clip(r - kh // 2, 0, rows - kh)
        k_band = lax.dynamic_slice_in_dim(kg, r0, kh, axis=1)
        v_band = lax.dynamic_slice_in_dim(vg, r0, kh, axis=1)
        k_win = k_band[:, :, col_idx]
        v_win = v_band[:, :, col_idx]
        q_row = lax.dynamic_index_in_dim(qg, r, axis=1, keepdims=False)
        row_off = r0 + jnp.arange(kh) - r + NA_KH_MAX - 1
        bias = jnp.take(rpb_cols, row_off, axis=1).transpose(0, 2, 1, 3)
        s = jnp.einsum('bqhd,biqjhd->bhqij', q_row, k_win).astype(F32) * scale + bias[None]
        p = jax.nn.softmax(s.reshape(B, H, GRID_W, kh * NA_KW), axis=-1)
        p = p.reshape(B, H, GRID_W, kh, NA_KW).astype(v.dtype)
        return jnp.einsum('bhqij,biqjhd->bqhd', p, v_win)

    out = lax.map(row_block, jnp.arange(rows))
    return out.transpose(1, 0, 2, 3, 4).reshape(B, L, H, Dh)


def axial_rope_tables(L):
    pos = jnp.arange(L)
    row = (pos // GRID_W).astype(F32)
    col = (pos % GRID_W).astype(F32)
    axis_dim = HEAD_DIM // 2
    inv_freq = ROPE_THETA ** (-jnp.arange(0, axis_dim, 2, dtype=F32) / axis_dim)
    ang = jnp.concatenate([row[:, None] * inv_freq, col[:, None] * inv_freq], -1)
    return jnp.cos(ang), jnp.sin(ang)


def apply_rope(x, cos, sin):
    xf = x.astype(F32).reshape(*x.shape[:-1], HEAD_DIM // 2, 2)
    x1, x2 = xf[..., 0], xf[..., 1]
    c = cos[None, :, None, :]
    s = sin[None, :, None, :]
    out = jnp.stack([x1 * c - x2 * s, x1 * s + x2 * c], -1).reshape(x.shape)
    return out.astype(x.dtype)


def gqa_blocked(q, k, v):
    B, L, Hq, Dh = q.shape
    G = Hq // KV_HEADS
    nblk = L // Q_BLOCK
    qb = q.reshape(B, nblk, Q_BLOCK, KV_HEADS, G, Dh).transpose(1, 0, 2, 3, 4, 5)
    scale = Dh ** -0.5

    def block(q_blk):
        s = jnp.einsum('bqkgd,bskd->bkgqs', q_blk, k).astype(F32) * scale
        p = jax.nn.softmax(s, axis=-1).astype(v.dtype)
        return jnp.einsum('bkgqs,bskd->bqkgd', p, v)

    out = lax.map(block, qb)
    return out.transpose(1, 0, 2, 3, 4, 5).reshape(B, L, Hq, Dh)


def short_conv_centred(u, w, b):
    up = jnp.pad(u, ((0, 0), (1, 1), (0, 0)))
    return up[:, :-2] * w[0] + up[:, 1:-1] * w[1] + up[:, 2:] * w[2] + b


def hyena_filters(L, w1, b1, w2, b2, w3, b3, w4, b4, freq):
    pos = jnp.arange(L, dtype=F32)
    t = pos / max(L - 1, 1)
    w = 2.0 * math.pi * pos / L
    f = jnp.linspace(1e-4, HYENA_BANDS - 1, HYENA_BANDS, dtype=F32)
    wf = w[:, None] * f[None, :]
    z = jnp.concatenate([t[:, None], jnp.cos(wf), -jnp.sin(wf)], -1)
    fr = freq.astype(F32)
    h = jnp.sin(fr * (z @ w1.astype(F32) + b1.astype(F32)))
    h = jnp.sin(fr * (h @ w2.astype(F32) + b2.astype(F32)))
    h = jnp.sin(fr * (h @ w3.astype(F32) + b3.astype(F32)))
    h = (h @ w4.astype(F32) + b4.astype(F32)).reshape(L, 2, HYENA_ORDER, MIX_WIDTH)
    max_decay = math.log(HYENA_DECAY_TARGET) / HYENA_FAST_DECAY
    min_decay = math.log(HYENA_DECAY_TARGET) / HYENA_SLOW_DECAY
    deltas = jnp.abs(jnp.linspace(min_decay, max_decay, MIX_WIDTH, dtype=F32))
    h = h * jnp.exp(-t[:, None, None, None] * deltas)
    return h / jnp.sum(jnp.abs(h), axis=(0, 1), keepdims=True)


def bidir_long_conv(u, h_fwd, h_bwd, bias):
    L, C = h_fwd.shape
    g = jnp.concatenate([h_fwd, jnp.zeros((1, C), F32), h_bwd[:0:-1]], 0)
    n = 2 * L
    u_f = jnp.fft.rfft(u.astype(F32), n=n, axis=1)
    g_f = jnp.fft.rfft(g, n=n, axis=0)
    y = jnp.fft.irfft(u_f * g_f[None], n=n, axis=1)[:, :L]
    return (y + u.astype(F32) * bias.astype(F32)).astype(u.dtype)


def hyena_mixer(u, conv_w, conv_b, filt, long_bias):
    uc = short_conv_centred(u, conv_w, conv_b)
    parts = jnp.split(uc, HYENA_ORDER + 1, axis=-1)
    z = parts[0]
    for o in range(HYENA_ORDER):
        z = parts[o + 1] * bidir_long_conv(z, filt[:, 0, o], filt[:, 1, o], long_bias[o])
    return z


def memory_attention(q, k, v):
    s = jnp.einsum('blhd,bmhd->bhlm', q, k).astype(F32) * (HEAD_DIM ** -0.5)
    p = jax.nn.softmax(s, axis=-1).astype(v.dtype)
    return jnp.einsum('bhlm,bmhd->blhd', p, v)


def moe(x, router_w, router_b, w1, b1, w2, b2):
    B, L, D = x.shape
    xt = x.reshape(-1, D)
    logits = (xt @ router_w + router_b).astype(F32)
    top_val, top_idx = lax.top_k(logits, TOP_K)
    gate = jax.nn.softmax(top_val, axis=-1)
    comb = jnp.einsum('nk,nke->ne', gate, jax.nn.one_hot(top_idx, N_EXPERTS, dtype=F32)).astype(x.dtype)
    y = jnp.zeros_like(xt)
    for e in range(N_EXPERTS):
        h = xt @ w1[e] + b1[e]
        g = jnp.minimum(h[:, :D_EXPERT], SWIGLU_LIMIT)
        u = jnp.clip(h[:, D_EXPERT:], -SWIGLU_LIMIT, SWIGLU_LIMIT)
        a = (u + 1.0) * (g * jax.nn.sigmoid(SWIGLU_ALPHA * g))
        y = y + comb[:, e:e + 1] * (a @ w2[e] + b2[e])
    return y.reshape(B, L, D)


def setup_inputs(seed: int = 0) -> dict:
    key = jax.random.key(seed)
    ks = iter(jax.random.split(key, 40))

    def nrm(shape, scale):
        return jax.random.normal(next(ks), shape, F32) * scale

    D = D_MODEL
    return {
        'x': nrm((BATCH, SEQ, D), 1.0),
        'mem': nrm((BATCH, MEM_LEN, D), 1.0),
        'w_in_a': nrm((N_LAYERS_A, D, IN_WIDTH_A), D ** -0.5),
        'rpb_a': nrm((N_LAYERS_A, MIX_HEADS, 2 * NA_KH_MAX - 1, 2 * NA_KW - 1), 0.1),
        'w_in_b': nrm((N_LAYERS_B, D, IN_WIDTH_B), D ** -0.5),
        'q_norm_b': 1.0 + nrm((N_LAYERS_B, HEAD_DIM), 0.02),
        'k_norm_b': 1.0 + nrm((N_LAYERS_B, HEAD_DIM), 0.02),
        'w_in_c': nrm((N_LAYERS_C, D, IN_WIDTH_C), D ** -0.5),
        'conv_w_c': nrm((N_LAYERS_C, 3, HYENA_WIDTH), 3 ** -0.5),
        'conv_b_c': nrm((N_LAYERS_C, HYENA_WIDTH), 0.02),
        'filt_w1': nrm((N_LAYERS_C, HYENA_EMB, HYENA_FFN), HYENA_EMB ** -0.5),
        'filt_b1': nrm((N_LAYERS_C, HYENA_FFN), 0.02),
        'filt_w2': nrm((N_LAYERS_C, HYENA_FFN, HYENA_FFN), HYENA_FFN ** -0.5),
        'filt_b2': nrm((N_LAYERS_C, HYENA_FFN), 0.02),
        'filt_w3': nrm((N_LAYERS_C, HYENA_FFN, HYENA_FFN), HYENA_FFN ** -0.5),
        'filt_b3': nrm((N_LAYERS_C, HYENA_FFN), 0.02),
        'filt_w4': nrm((N_LAYERS_C, HYENA_FFN, 2 * HYENA_ORDER * MIX_WIDTH), HYENA_FFN ** -0.5),
        'filt_b4': nrm((N_LAYERS_C, 2 * HYENA_ORDER * MIX_WIDTH), 0.02),
        'filt_freq': 1.0 + nrm((N_LAYERS_C, HYENA_FFN), 0.02),
        'long_bias_c': nrm((N_LAYERS_C, HYENA_ORDER, MIX_WIDTH), 1.0),
        'w_mem_kv': nrm((DEPTH, D, 2 * MEM_WIDTH), D ** -0.5),
        'w_out': nrm((DEPTH, CAT_WIDTH, D), CAT_WIDTH ** -0.5 * DEEPNORM_BETA),
        'ln1_g': 1.0 + nrm((DEPTH, D), 0.02),
        'ln1_b': nrm((DEPTH, D), 0.02),
        'ln2_g': 1.0 + nrm((DEPTH, D), 0.02),
        'ln2_b': nrm((DEPTH, D), 0.02),
        'router_w': nrm((DEPTH, D, N_EXPERTS), D ** -0.5),
        'router_b': nrm((DEPTH, N_EXPERTS), 0.01),
        'moe_w1': nrm((DEPTH, N_EXPERTS, D, 2 * D_EXPERT), D ** -0.5),
        'moe_b1': nrm((DEPTH, N_EXPERTS, 2 * D_EXPERT), 0.02),
        'moe_w2': nrm((DEPTH, N_EXPERTS, D_EXPERT, D), D_EXPERT ** -0.5 * DEEPNORM_BETA),
        'moe_b2': nrm((DEPTH, N_EXPERTS, D), 0.02),
    }


def reference(x, mem, w_in_a, rpb_a, w_in_b, q_norm_b, k_norm_b, w_in_c, conv_w_c, conv_b_c,
              filt_w1, filt_b1, filt_w2, filt_b2, filt_w3, filt_b3, filt_w4, filt_b4, filt_freq,
              long_bias_c, w_mem_kv, w_out, ln1_g, ln1_b, ln2_g, ln2_b, router_w, router_b,
              moe_w1, moe_b1, moe_w2, moe_b2):
    B, L, _ = x.shape
    for i in range(DEPTH):
        kind = i % N_MIXERS
        j = i // N_MIXERS
        if kind == 0:
            proj = x @ w_in_a[j]
            q, k, v, q_mem = jnp.split(proj, [MIX_WIDTH, 2 * MIX_WIDTH, 3 * MIX_WIDTH], axis=-1)
            mix = neighbourhood_attention(split_heads(q, MIX_HEADS), split_heads(k, MIX_HEADS),
                                          split_heads(v, MIX_HEADS), rpb_a[j])
        elif kind == 1:
            proj = x @ w_in_b[j]
            q, k, v, q_mem = jnp.split(proj, [MIX_WIDTH, MIX_WIDTH + KV_WIDTH, MIX_WIDTH + 2 * KV_WIDTH], axis=-1)
            cos, sin = axial_rope_tables(L)
            q = apply_rope(rms_norm(split_heads(q, MIX_HEADS), q_norm_b[j]), cos, sin)
            k = apply_rope(rms_norm(split_heads(k, KV_HEADS), k_norm_b[j]), cos, sin)
            mix = gqa_blocked(q, k, split_heads(v, KV_HEADS))
        else:
            proj = x @ w_in_c[j]
            u, q_mem = jnp.split(proj, [HYENA_WIDTH], axis=-1)
            filt = hyena_filters(L, filt_w1[j], filt_b1[j], filt_w2[j], filt_b2[j], filt_w3[j],
                                 filt_b3[j], filt_w4[j], filt_b4[j], filt_freq[j])
            mix = hyena_mixer(u, conv_w_c[j], conv_b_c[j], filt, long_bias_c[j])
        k_mem, v_mem = jnp.split(mem @ w_mem_kv[i], 2, axis=-1)
        mem_out = memory_attention(split_heads(q_mem, MEM_HEADS), split_heads(k_mem, MEM_HEADS),
                                   split_heads(v_mem, MEM_HEADS))
        cat = jnp.concatenate([mix.reshape(B, L, MIX_WIDTH), mem_out.reshape(B, L, MEM_WIDTH)], axis=-1)
        x = layer_norm(DEEPNORM_ALPHA * x + cat @ w_out[i], ln1_g[i], ln1_b[i])
        y = moe(x, router_w[i], router_b[i], moe_w1[i], moe_b1[i], moe_w2[i], moe_b2[i])
        x = layer_norm(DEEPNORM_ALPHA * x + y, ln2_g[i], ln2_b[i])
    return x
```

```python
import functools
import math

import numpy as np
import jax
import jax.numpy as jnp
from jax import lax
from jax.experimental import pallas as pl
from jax.experimental.pallas import tpu as pltpu

F32 = jnp.float32
BF16 = jnp.bfloat16

D_MODEL = 1024
DEPTH = 4
GRID_W = 64
N_MIXERS = 3
HEAD_DIM = 64
MIX_WIDTH = 768
MIX_HEADS = MIX_WIDTH // HEAD_DIM
MEM_HEADS = 4
MEM_WIDTH = MEM_HEADS * HEAD_DIM
NA_KH = 8
NA_KW = 16
KV_HEADS = 4
KV_WIDTH = KV_HEADS * HEAD_DIM
GQA_GROUP = MIX_HEADS // KV_HEADS
ROPE_THETA = 10000.0
QK_NORM_EPS = 1e-6
HYENA_ORDER = 2
HYENA_BANDS = 16
HYENA_EMB = 2 * HYENA_BANDS + 1
HYENA_FFN = 64
HYENA_WIDTH = (HYENA_ORDER + 1) * MIX_WIDTH
HYENA_DECAY_TARGET = 1e-2
HYENA_FAST_DECAY = 0.3
HYENA_SLOW_DECAY = 1.5
N_EXPERTS = 32
TOP_K = 4
D_EXPERT = D_MODEL
SWIGLU_LIMIT = 7.0
SWIGLU_ALPHA = 1.702
DEEPNORM_ALPHA = (2 * DEPTH) ** 0.25
LN_EPS = 1e-5

LANES = 128
VMEM_LIMIT = 56 * 1024 * 1024
NEG_BIG = -1e30
ATT_SCALE = HEAD_DIM ** -0.5


def _cparams(sem):
    return pltpu.CompilerParams(dimension_semantics=sem, vmem_limit_bytes=VMEM_LIMIT)


def _layer_norm(x, g, b):
    mu = jnp.mean(x, -1, keepdims=True)
    xc = x - mu
    var = jnp.mean(xc * xc, -1, keepdims=True)
    return xc * lax.rsqrt(var + LN_EPS) * g + b


def _mm_kernel(x_ref, w_ref, o_ref):
    o_ref[...] = jnp.dot(x_ref[...].astype(BF16), w_ref[...],
                         preferred_element_type=F32).astype(o_ref.dtype)


def _matmul(x, w, out_dtype, tm, tn):
    M, K = x.shape
    N = w.shape[1]
    tm = min(tm, M)
    tn = min(tn, N)
    assert M % tm == 0 and N % tn == 0
    return pl.pallas_call(
        _mm_kernel,
        grid=(M // tm, N // tn),
        in_specs=[pl.BlockSpec((tm, K), lambda i, j: (i, 0)),
                  pl.BlockSpec((K, tn), lambda i, j: (0, j))],
        out_specs=pl.BlockSpec((tm, tn), lambda i, j: (i, j)),
        out_shape=jax.ShapeDtypeStruct((M, N), out_dtype),
        compiler_params=_cparams(("parallel", "arbitrary")),
        name="matmul",
    )(x, w)


def _na_bias_table(rpb, rows):
    del rows
    col = np.arange(GRID_W)
    c0 = np.clip(col - NA_KW // 2, 0, GRID_W - NA_KW)
    cc = np.arange(GRID_W)
    valid = (cc[None, :] >= c0[:, None]) & (cc[None, :] < c0[:, None] + NA_KW)
    coff = np.clip(cc[None, :] - col[:, None] + NA_KW - 1, 0, 2 * NA_KW - 2)
    d = np.arange(NA_KH)
    i = np.arange(NA_KH)
    roff = i[None, :] - d[:, None] + NA_KH - 1
    t = rpb.astype(F32)[:, roff]
    t = t[:, :, :, coff]
    t = t.transpose(0, 1, 3, 2, 4)
    t = jnp.where(valid[None, None, :, None, :], t, NEG_BIG)
    return t.reshape(rpb.shape[0], NA_KH, GRID_W, NA_KH * GRID_W)


def _na_kernel(q_ref, k_ref, v_ref, bias_ref, o_ref, *, rows):
    lane = lax.broadcasted_iota(jnp.int32, (GRID_W, LANES), 1)
    first = lane < HEAD_DIM
    band = NA_KH * GRID_W

    def body(r, carry):
        r0 = jnp.clip(r - NA_KH // 2, 0, rows - NA_KH)
        d = r - r0
        qoff = pl.multiple_of(r * GRID_W, GRID_W)
        koff = pl.multiple_of(r0 * GRID_W, GRID_W)
        q = q_ref[pl.ds(qoff, GRID_W), :]
        kb = k_ref[pl.ds(koff, band), :]
        vb = v_ref[pl.ds(koff, band), :]
        zero = jnp.zeros_like(q)
        q2 = jnp.concatenate([jnp.where(first, q, zero), jnp.where(first, zero, q)], axis=0)
        s = lax.dot_general(q2, kb, (((1,), (1,)), ((), ())), preferred_element_type=F32)
        bias = jnp.concatenate([bias_ref[0, d], bias_ref[1, d]], axis=0)
        s = s * ATT_SCALE + bias
        m = jnp.max(s, -1, keepdims=True)
        e = jnp.exp(s - m)
        l = jnp.sum(e, -1, keepdims=True)
        o2 = jnp.dot(e.astype(BF16), vb, preferred_element_type=F32) / l
        o = jnp.where(first, o2[:GRID_W], o2[GRID_W:])
        o_ref[pl.ds(qoff, GRID_W), :] = o.astype(o_ref.dtype)
        return carry

    lax.fori_loop(0, rows, body, 0)


def _na_attention(proj, bias_tab, B, L):
    rows = L // GRID_W
    assert rows >= NA_KH
    npair = MIX_WIDTH // LANES
    return pl.pallas_call(
        functools.partial(_na_kernel, rows=rows),
        grid=(npair, B),
        in_specs=[pl.BlockSpec((None, L, LANES), lambda p, b: (b, 0, p)),
                  pl.BlockSpec((None, L, LANES), lambda p, b: (b, 0, npair + p)),
                  pl.BlockSpec((None, L, LANES), lambda p, b: (b, 0, 2 * npair + p)),
                  pl.BlockSpec((2, NA_KH, GRID_W, NA_KH * GRID_W), lambda p, b: (p, 0, 0, 0))],
        out_specs=pl.BlockSpec((None, L, LANES), lambda p, b: (b, 0, p)),
        out_shape=jax.ShapeDtypeStruct((B, L, MIX_WIDTH), BF16),
        compiler_params=_cparams(("parallel", "parallel")),
        name="na_attention",
    )(proj, proj, proj, bias_tab)


def _pair_lane_dims():
    l = np.arange(LANES)
    return 2 * (l % 32) + (l >= HEAD_DIM).astype(np.int64), ((l % HEAD_DIM) >= 32).astype(np.int64)


def _gqa_column_perms():
    dim, which = _pair_lane_dims()
    q_cols = []
    out_rows = []
    for P in range(KV_HEADS // 2):
        for g in range(GQA_GROUP):
            heads = ((2 * P) * GQA_GROUP + g, (2 * P + 1) * GQA_GROUP + g)
            q_cols.append(np.array([heads[w] * HEAD_DIM + dd for dd, w in zip(dim, which)]))
            out_rows.append(np.concatenate([heads[0] * HEAD_DIM + np.arange(HEAD_DIM),
                                            heads[1] * HEAD_DIM + np.arange(HEAD_DIM)]))
    k_cols = []
    for P in range(KV_HEADS // 2):
        heads = (2 * P, 2 * P + 1)
        k_cols.append(np.array([heads[w] * HEAD_DIM + dd for dd, w in zip(dim, which)]))
    return np.concatenate(q_cols), np.concatenate(k_cols), np.concatenate(out_rows)


def _rope_lane_tables(L):
    pos = jnp.arange(L)
    row = (pos // GRID_W).astype(F32)
    col = (pos % GRID_W).astype(F32)
    axis_dim = HEAD_DIM // 2
    inv_freq = ROPE_THETA ** (-jnp.arange(0, axis_dim, 2, dtype=F32) / axis_dim)
    ang = jnp.concatenate([row[:, None] * inv_freq, col[:, None] * inv_freq], -1)
    c = jnp.cos(ang)
    s = jnp.sin(ang)
    return jnp.concatenate([c, c, c, c], -1), jnp.concatenate([-s, -s, s, s], -1)


def _qk_prep_kernel(q_ref, k_ref, cos_ref, sin_ref, gq_ref, gk_ref, qo_ref, ko_ref):
    tm = q_ref.shape[0]
    lane = lax.broadcasted_iota(jnp.int32, (tm, LANES), 1)
    head_a = (lane % HEAD_DIM) < 32
    c = cos_ref[...]
    s = sin_ref[...]

    def norm_rope(x, g):
        x = x.astype(F32)
        x2 = x * x
        tot = jnp.sum(x2, -1, keepdims=True)
        sa = jnp.sum(jnp.where(head_a, x2, 0.0), -1, keepdims=True)
        inv = jnp.where(head_a, lax.rsqrt(sa * (1.0 / HEAD_DIM) + QK_NORM_EPS),
                        lax.rsqrt((tot - sa) * (1.0 / HEAD_DIM) + QK_NORM_EPS))
        xn = x * inv * g
        return xn * c + pltpu.roll(xn, HEAD_DIM, axis=1) * s

    for j in range(q_ref.shape[1] // LANES):
        sl = slice(j * LANES, (j + 1) * LANES)
        qo_ref[:, sl] = (norm_rope(q_ref[:, sl], gq_ref[...]) * ATT_SCALE).astype(qo_ref.dtype)
    for j in range(k_ref.shape[1] // LANES):
        sl = slice(j * LANES, (j + 1) * LANES)
        ko_ref[:, sl] = norm_rope(k_ref[:, sl], gk_ref[...]).astype(ko_ref.dtype)


def _qk_prep(proj, cos_t, sin_t, gq, gk, L, tm):
    N = proj.shape[0]
    nl = L // tm
    return pl.pallas_call(
        _qk_prep_kernel,
        grid=(N // tm,),
        in_specs=[pl.BlockSpec((tm, MIX_WIDTH), lambda i: (i, 0)),
                  pl.BlockSpec((tm, KV_WIDTH), lambda i: (i, MIX_WIDTH // KV_WIDTH)),
                  pl.BlockSpec((tm, LANES), lambda i: (i % nl, 0)),
                  pl.BlockSpec((tm, LANES), lambda i: (i % nl, 0)),
                  pl.BlockSpec((1, LANES), lambda i: (0, 0)),
                  pl.BlockSpec((1, LANES), lambda i: (0, 0))],
        out_specs=[pl.BlockSpec((tm, MIX_WIDTH), lambda i: (i, 0)),
                   pl.BlockSpec((tm, KV_WIDTH), lambda i: (i, 0))],
        out_shape=[jax.ShapeDtypeStruct((N, MIX_WIDTH), BF16),
                   jax.ShapeDtypeStruct((N, KV_WIDTH), BF16)],
        compiler_params=_cparams(("parallel",)),
        name="qk_prep",
    )(proj, proj, cos_t, sin_t, gq, gk)


def _gqa_kernel(q_ref, k_ref, v_ref, o_ref):
    tq = q_ref.shape[0]
    lane = lax.broadcasted_iota(jnp.int32, (tq, LANES), 1)
    head_a = (lane % HEAD_DIM) < 32
    q = q_ref[...]
    parts = []
    for g in range(GQA_GROUP):
        qs = q[:, g * LANES:(g + 1) * LANES]
        zero = jnp.zeros_like(qs)
        parts += [jnp.where(head_a, qs, zero), jnp.where(head_a, zero, qs)]
    q6 = jnp.concatenate(parts, axis=0)
    s = lax.dot_general(q6, k_ref[...], (((1,), (1,)), ((), ())), preferred_element_type=F32)
    m = jnp.max(s, -1, keepdims=True)
    e = jnp.exp(s - m)
    l = jnp.sum(e, -1, keepdims=True)
    o6 = jnp.dot(e.astype(BF16), v_ref[...], preferred_element_type=F32) / l
    for g in range(GQA_GROUP):
        oa = o6[(2 * g) * tq:(2 * g + 1) * tq]
        ob = o6[(2 * g + 1) * tq:(2 * g + 2) * tq]
        o_ref[:, g * LANES:(g + 1) * LANES] = jnp.where(lane < HEAD_DIM, oa, ob).astype(o_ref.dtype)


def _gqa_attention(q_rot, k_rot, proj, B, L, tq):
    slab_w = GQA_GROUP * LANES
    v_blk0 = (MIX_WIDTH + KV_WIDTH) // LANES
    return pl.pallas_call(
        _gqa_kernel,
        grid=(B, KV_HEADS // 2, L // tq),
        in_specs=[pl.BlockSpec((None, tq, slab_w), lambda b, p, i: (b, i, p)),
                  pl.BlockSpec((None, L, LANES), lambda b, p, i: (b, 0, p)),
                  pl.BlockSpec((None, L, LANES), lambda b, p, i: (b, 0, v_blk0 + p))],
        out_specs=pl.BlockSpec((None, tq, slab_w), lambda b, p, i: (b, i, p)),
        out_shape=jax.ShapeDtypeStruct((B, L, MIX_WIDTH), BF16),
        compiler_params=_cparams(("parallel", "parallel", "arbitrary")),
        name="gqa_attention",
    )(q_rot, k_rot, proj)


def _hyena_features(L):
    pos = jnp.arange(L, dtype=F32)
    t = pos / max(L - 1, 1)
    w = 2.0 * math.pi * pos / L
    f = jnp.linspace(1e-4, HYENA_BANDS - 1, HYENA_BANDS, dtype=F32)
    wf = w[:, None] * f[None, :]
    z = jnp.concatenate([t[:, None], jnp.cos(wf), -jnp.sin(wf)], -1)
    return jnp.pad(z, ((0, 0), (0, LANES - HYENA_EMB))), t[:, None]


def _filter_kernel(z_ref, t_ref, w1_ref, b1_ref, w2_ref, b2_ref, w3_ref, b3_ref, w4f_ref, w4b_ref,
                   b4f_ref, b4b_ref, fr_ref, delta_ref, hsum_ref, hdiff_ref):
    hp = lax.Precision.HIGHEST
    fr = fr_ref[...]
    h = jnp.sin(fr * (jnp.dot(z_ref[...], w1_ref[...], precision=hp, preferred_element_type=F32) + b1_ref[...]))
    h = jnp.sin(fr * (jnp.dot(h, w2_ref[...], precision=hp, preferred_element_type=F32) + b2_ref[...]))
    h = jnp.sin(fr * (jnp.dot(h, w3_ref[...], precision=hp, preferred_element_type=F32) + b3_ref[...]))
    hf = jnp.dot(h, w4f_ref[...], precision=hp, preferred_element_type=F32) + b4f_ref[...]
    hb = jnp.dot(h, w4b_ref[...], precision=hp, preferred_element_type=F32) + b4b_ref[...]
    dec = jnp.exp(-t_ref[...] * delta_ref[...])
    hf = hf * dec
    hb = hb * dec
    norm = jnp.sum(jnp.abs(hf), 0, keepdims=True) + jnp.sum(jnp.abs(hb), 0, keepdims=True)
    inv = 1.0 / norm
    hf = hf * inv
    row = lax.broadcasted_iota(jnp.int32, hb.shape, 0)
    hb = jnp.where(row == 0, 0.0, hb * inv)
    hsum_ref[...] = hf + hb
    hdiff_ref[...] = hf - hb


def _hyena_filters(zf, t, w1, b1, w2, b2, w3, b3, w4, b4, freq, L, tc):
    OC = HYENA_ORDER * MIX_WIDTH
    max_decay = math.log(HYENA_DECAY_TARGET) / HYENA_FAST_DECAY
    min_decay = math.log(HYENA_DECAY_TARGET) / HYENA_SLOW_DECAY
    deltas = jnp.abs(jnp.linspace(min_decay, max_decay, MIX_WIDTH, dtype=F32))
    deltas = jnp.tile(deltas, HYENA_ORDER)[None, :]
    w1p = jnp.pad(w1.astype(F32), ((0, LANES - HYENA_EMB), (0, 0)))
    row = lambda a: a.astype(F32)[None, :]
    full = lambda shape: pl.BlockSpec(shape, lambda j: (0,) * len(shape))
    nb = OC // tc
    return pl.pallas_call(
        _filter_kernel,
        grid=(nb,),
        in_specs=[full((L, LANES)), full((L, 1)),
                  full((LANES, HYENA_FFN)), full((1, HYENA_FFN)),
                  full((HYENA_FFN, HYENA_FFN)), full((1, HYENA_FFN)),
                  full((HYENA_FFN, HYENA_FFN)), full((1, HYENA_FFN)),
                  pl.BlockSpec((HYENA_FFN, tc), lambda j: (0, j)),
                  pl.BlockSpec((HYENA_FFN, tc), lambda j: (0, nb + j)),
                  pl.BlockSpec((1, tc), lambda j: (0, j)),
                  pl.BlockSpec((1, tc), lambda j: (0, nb + j)),
                  full((1, HYENA_FFN)),
                  pl.BlockSpec((1, tc), lambda j: (0, j))],
        out_specs=[pl.BlockSpec((L, tc), lambda j: (0, j)),
                   pl.BlockSpec((L, tc), lambda j: (0, j))],
        out_shape=[jax.ShapeDtypeStruct((L, OC), F32), jax.ShapeDtypeStruct((L, OC), F32)],
        compiler_params=_cparams(("parallel",)),
        name="hyena_filters",
    )(zf, t, w1p, row(b1), w2.astype(F32), row(b2), w3.astype(F32), row(b3), w4.astype(F32), w4.astype(F32),
      row(b4), row(b4), row(freq), deltas)


def _dft_matrices(L):
    n = 2 * L
    k = jnp.arange(L, dtype=jnp.int32)
    ks = (k[:, None] * k[None, :]) % n
    ang = ks.astype(F32) * (2.0 * math.pi / n)
    fc = jnp.cos(ang)
    sn = jnp.sin(ang)
    alt = jnp.where(k % 2 == 0, 1.0, -1.0).astype(F32)
    fs = jnp.where(k[:, None] == 0, alt[None, :], sn)
    fst = jnp.where(k[None, :] == 0, alt[:, None], sn)
    return fc.astype(BF16), fs.astype(BF16), fst.astype(BF16)


def _short_conv_kernel(u_ref, w_ref, b_ref, o_ref):
    L = u_ref.shape[0]
    u = u_ref[...].astype(F32)
    row = lax.broadcasted_iota(jnp.int32, u.shape, 0)
    prev = jnp.where(row == 0, 0.0, pltpu.roll(u, 1, axis=0))
    nxt = jnp.where(row == L - 1, 0.0, pltpu.roll(u, L - 1, axis=0))
    w = w_ref[...]
    o_ref[...] = (prev * w[0:1] + u * w[1:2] + nxt * w[2:3] + b_ref[...]).astype(o_ref.dtype)


def _short_conv(proj, conv_w, conv_b, B, L, tc):
    return pl.pallas_call(
        _short_conv_kernel,
        grid=(B, HYENA_WIDTH // tc),
        in_specs=[pl.BlockSpec((None, L, tc), lambda b, j: (b, 0, j)),
                  pl.BlockSpec((3, tc), lambda b, j: (0, j)),
                  pl.BlockSpec((1, tc), lambda b, j: (0, j))],
        out_specs=pl.BlockSpec((None, L, tc), lambda b, j: (b, 0, j)),
        out_shape=jax.ShapeDtypeStruct((B, L, HYENA_WIDTH), BF16),
        compiler_params=_cparams(("parallel", "parallel")),
        name="short_conv",
    )(proj, conv_w.astype(F32), conv_b.astype(F32)[None, :])


def _hy_fwd_kernel(fc_ref, fs_ref, z_ref, pa_ref, qa_ref, qb_ref, sb_ref, ya_ref, yb_ref):
    z = z_ref[...]
    a = jnp.dot(fc_ref[...], z, preferred_element_type=F32)
    b = jnp.dot(fs_ref[...], z, preferred_element_type=F32)
    ya_ref[...] = (a * pa_ref[...] - b * qa_ref[...]).astype(ya_ref.dtype)
    yb_ref[...] = (a * qb_ref[...] + b * sb_ref[...]).astype(yb_ref.dtype)


def _hy_inv_kernel(fc_ref, fst_ref, ya_ref, yb_ref, z_ref, xg_ref, bias_ref, o_ref):
    y = jnp.dot(fc_ref[...], ya_ref[...], preferred_element_type=F32)
    y = y + jnp.dot(fst_ref[...], yb_ref[...], preferred_element_type=F32)
    z = z_ref[...].astype(F32)
    o_ref[...] = (xg_ref[...].astype(F32) * (y + bias_ref[...] * z)).astype(o_ref.dtype)


def _hyena_long_conv_gate(uc, z_arr, z_blk, xg_blk, fc, fs, fst, coef, bias, B, L, tm_f, tm_i):
    C = MIX_WIDTH
    pa, qa, qb, sb = coef
    ya, yb = pl.pallas_call(
        _hy_fwd_kernel,
        grid=(B, L // tm_f),
        in_specs=[pl.BlockSpec((tm_f, L), lambda b, i: (i, 0)),
                  pl.BlockSpec((tm_f, L), lambda b, i: (i, 0)),
                  pl.BlockSpec((None, L, C), lambda b, i: (b, 0, z_blk)),
                  pl.BlockSpec((tm_f, C), lambda b, i: (i, 0)),
                  pl.BlockSpec((tm_f, C), lambda b, i: (i, 0)),
                  pl.BlockSpec((tm_f, C), lambda b, i: (i, 0)),
                  pl.BlockSpec((tm_f, C), lambda b, i: (i, 0))],
        out_specs=[pl.BlockSpec((None, tm_f, C), lambda b, i: (b, i, 0)),
                   pl.BlockSpec((None, tm_f, C), lambda b, i: (b, i, 0))],
        out_shape=[jax.ShapeDtypeStruct((B, L, C), BF16), jax.ShapeDtypeStruct((B, L, C), BF16)],
        compiler_params=_cparams(("parallel", "arbitrary")),
        name="hyena_dft_fwd",
    )(fc, fs, z_arr, pa, qa, qb, sb)
    return pl.pallas_call(
        _hy_inv_kernel,
        grid=(B, L // tm_i),
        in_specs=[pl.BlockSpec((tm_i, L), lambda b, i: (i, 0)),
                  pl.BlockSpec((tm_i, L), lambda b, i: (i, 0)),
                  pl.BlockSpec((None, L, C), lambda b, i: (b, 0, 0)),
                  pl.BlockSpec((None, L, C), lambda b, i: (b, 0, 0)),
                  pl.BlockSpec((None, tm_i, C), lambda b, i: (b, i, z_blk)),
                  pl.BlockSpec((None, tm_i, C), lambda b, i: (b, i, xg_blk)),
                  pl.BlockSpec((1, C), lambda b, i: (0, 0))],
        out_specs=pl.BlockSpec((None, tm_i, C), lambda b, i: (b, i, 0)),
        out_shape=jax.ShapeDtypeStruct((B, L, C), BF16),
        compiler_params=_cparams(("parallel", "arbitrary")),
        name="hyena_dft_inv",
    )(fc, fst, ya, yb, z_arr, uc, bias)


def _hyena_mixer(proj, j, p, B, L):
    C = MIX_WIDTH
    OC = HYENA_ORDER * C
    n = 2 * L
    zf, t = _hyena_features(L)
    hsum, hdiff = _hyena_filters(zf, t, p['filt_w1'][j], p['filt_b1'][j], p['filt_w2'][j], p['filt_b2'][j],
                                 p['filt_w3'][j], p['filt_b3'][j], p['filt_w4'][j], p['filt_b4'][j],
                                 p['filt_freq'][j], L, 256)
    fc, fs, fst = _dft_matrices(L)
    parts = []
    for h in (hsum, hdiff):
        hi = h.astype(BF16)
        parts += [hi, (h - hi.astype(F32)).astype(BF16)]
    hcat = jnp.concatenate(parts, axis=1)
    spec = _matmul(jnp.concatenate([fc, fs], axis=0), hcat, F32, 512, 512)
    gc = spec[:L, 0:OC] + spec[:L, OC:2 * OC]
    nyq = spec[L:L + 1, 0:OC] + spec[L:L + 1, OC:2 * OC]
    gs = spec[L:, 2 * OC:3 * OC] + spec[L:, 3 * OC:4 * OC]
    k0 = (jnp.arange(L) == 0)[:, None]
    ca = jnp.where(k0, 1.0 / n, 2.0 / n).astype(F32)
    pa = ca * gc
    qa = ca * jnp.where(k0, 0.0, gs)
    qb = qa
    sb = ca * jnp.where(k0, nyq, gc)
    uc = _short_conv(proj, p['conv_w_c'][j], p['conv_b_c'][j], B, L, 256)
    bias = p['long_bias_c'][j].astype(F32)
    tm_f = min(512, L)
    tm_i = min(256, L)
    z = uc
    z_blk = 0
    for o in range(HYENA_ORDER):
        sl = slice(o * C, (o + 1) * C)
        coef = (pa[:, sl], qa[:, sl], qb[:, sl], sb[:, sl])
        z = _hyena_long_conv_gate(uc, z, z_blk, o + 1, fc, fs, fst, coef, bias[o][None, :], B, L, tm_f, tm_i)
        z_blk = 0
    return z


def _post_kernel(x_ref, mix_ref, qm_ref, km_ref, vm_ref, wo_ref, g_ref, b_ref, rw_ref, rb_ref,
                 xo_ref, xb_ref, comb_ref):
    tm = x_ref.shape[0]
    lane = lax.broadcasted_iota(jnp.int32, (tm, LANES), 1)
    first = lane < HEAD_DIM
    outs = []
    for p in range(MEM_WIDTH // LANES):
        sl = slice(p * LANES, (p + 1) * LANES)
        qs = qm_ref[:, sl]
        zero = jnp.zeros_like(qs)
        q2 = jnp.concatenate([jnp.where(first, qs, zero), jnp.where(first, zero, qs)], axis=0)
        s = lax.dot_general(q2, km_ref[:, sl], (((1,), (1,)), ((), ())), preferred_element_type=F32) * ATT_SCALE
        m = jnp.max(s, -1, keepdims=True)
        e = jnp.exp(s - m)
        l = jnp.sum(e, -1, keepdims=True)
        o2 = jnp.dot(e.astype(BF16), vm_ref[:, sl], preferred_element_type=F32) / l
        outs.append(jnp.where(first, o2[:tm], o2[tm:]).astype(BF16))
    mem_out = jnp.concatenate(outs, axis=1)
    acc = jnp.dot(mix_ref[...], wo_ref[:MIX_WIDTH, :], preferred_element_type=F32)
    acc = acc + jnp.dot(mem_out, wo_ref[MIX_WIDTH:, :], preferred_element_type=F32)
    x1 = _layer_norm(DEEPNORM_ALPHA * x_ref[...] + acc, g_ref[...], b_ref[...])
    xo_ref[...] = x1
    xb_ref[...] = x1.astype(BF16)
    logits = jnp.dot(x1, rw_ref[...], precision=lax.Precision.HIGHEST, preferred_element_type=F32) + rb_ref[...]
    lane_f = lane.astype(F32)
    work = logits
    comb = jnp.zeros_like(logits)
    denom = jnp.zeros((tm, 1), F32)
    m0 = None
    for k in range(TOP_K):
        m = jnp.max(work, -1, keepdims=True)
        idx = jnp.min(jnp.where(work == m, lane_f, float(LANES)), -1, keepdims=True)
        sel = lane_f == idx
        if k == 0:
            m0 = m
        ek = jnp.exp(m - m0)
        comb = comb + jnp.where(sel, ek, 0.0)
        denom = denom + ek
        work = jnp.where(sel, -jnp.inf, work)
    comb_ref[...] = comb / denom


def _post_mixer(x, mix, proj, qm_blk, kmem, vmem, wo, g, b, rw, rb, L, tm):
    N, D = x.shape
    nl = L // tm
    CAT = MIX_WIDTH + MEM_WIDTH
    return pl.pallas_call(
        _post_kernel,
        grid=(N // tm,),
        in_specs=[pl.BlockSpec((tm, D), lambda i: (i, 0)),
                  pl.BlockSpec((tm, MIX_WIDTH), lambda i: (i, 0)),
                  pl.BlockSpec((tm, MEM_WIDTH), lambda i: (i, qm_blk)),
                  pl.BlockSpec((None, kmem.shape[1], MEM_WIDTH), lambda i: (i // nl, 0, 0)),
                  pl.BlockSpec((None, vmem.shape[1], MEM_WIDTH), lambda i: (i // nl, 0, 0)),
                  pl.BlockSpec((CAT, D), lambda i: (0, 0)),
                  pl.BlockSpec((1, D), lambda i: (0, 0)),
                  pl.BlockSpec((1, D), lambda i: (0, 0)),
                  pl.BlockSpec((D, LANES), lambda i: (0, 0)),
                  pl.BlockSpec((1, LANES), lambda i: (0, 0))],
        out_specs=[pl.BlockSpec((tm, D), lambda i: (i, 0)),
                   pl.BlockSpec((tm, D), lambda i: (i, 0)),
                   pl.BlockSpec((tm, LANES), lambda i: (i, 0))],
        out_shape=[jax.ShapeDtypeStruct((N, D), F32), jax.ShapeDtypeStruct((N, D), BF16),
                   jax.ShapeDtypeStruct((N, LANES), F32)],
        compiler_params=_cparams(("parallel",)),
        name="post_mixer",
    )(x, mix, proj, kmem, vmem, wo, g, b, rw, rb)


def _moe_kernel(xb_ref, x_ref, comb_ref, w1_ref, b1_ref, w2_ref, b2_ref, g_ref, b_ref, o_ref, acc_ref):
    e = pl.program_id(1)

    @pl.when(e == 0)
    def _():
        acc_ref[...] = jnp.zeros_like(acc_ref)

    h = jnp.dot(xb_ref[...], w1_ref[...], preferred_element_type=F32) + b1_ref[...]
    gl = jnp.minimum(h[:, :D_EXPERT], SWIGLU_LIMIT)
    ul = jnp.clip(h[:, D_EXPERT:], -SWIGLU_LIMIT, SWIGLU_LIMIT)
    a = (ul + 1.0) * (gl * jax.nn.sigmoid(SWIGLU_ALPHA * gl))
    o = jnp.dot(a.astype(BF16), w2_ref[...], preferred_element_type=F32) + b2_ref[...]
    comb = comb_ref[...]
    lane = lax.broadcasted_iota(jnp.int32, comb.shape, 1)
    gate = jnp.sum(jnp.where(lane == e, comb, 0.0), -1, keepdims=True)
    acc_ref[...] += gate * o

    @pl.when(e == pl.num_programs(1) - 1)
    def _():
        o_ref[...] = _layer_norm(DEEPNORM_ALPHA * x_ref[...] + acc_ref[...], g_ref[...], b_ref[...])


def _moe(xb, x, comb, w1, b1, w2, b2, g, b, layer, tm):
    N, D = x.shape
    return pl.pallas_call(
        _moe_kernel,
        grid=(N // tm, N_EXPERTS),
        in_specs=[pl.BlockSpec((tm, D), lambda i, e: (i, 0)),
                  pl.BlockSpec((tm, D), lambda i, e: (i, 0)),
                  pl.BlockSpec((tm, LANES), lambda i, e: (i, 0)),
                  pl.BlockSpec((None, None, D, 2 * D_EXPERT), lambda i, e: (layer, e, 0, 0)),
                  pl.BlockSpec((None, None, 1, 2 * D_EXPERT), lambda i, e: (layer, e, 0, 0)),
                  pl.BlockSpec((None, None, D_EXPERT, D), lambda i, e: (layer, e, 0, 0)),
                  pl.BlockSpec((None, None, 1, D), lambda i, e: (layer, e, 0, 0)),
                  pl.BlockSpec((1, D), lambda i, e: (0, 0)),
                  pl.BlockSpec((1, D), lambda i, e: (0, 0))],
        out_specs=pl.BlockSpec((tm, D), lambda i, e: (i, 0)),
        out_shape=jax.ShapeDtypeStruct((N, D), F32),
        scratch_shapes=[pltpu.VMEM((tm, D), F32)],
        compiler_params=_cparams(("parallel", "arbitrary")),
        name="moe",
    )(xb, x, comb, w1, b1, w2, b2, g, b)


def kernel(x, mem, w_in_a, rpb_a, w_in_b, q_norm_b, k_norm_b, w_in_c, conv_w_c, conv_b_c, filt_w1, filt_b1, filt_w2, filt_b2, filt_w3, filt_b3, filt_w4, filt_b4, filt_freq, long_bias_c, w_mem_kv, w_out, ln1_g, ln1_b, ln2_g, ln2_b, router_w, router_b, moe_w1, moe_b1, moe_w2, moe_b2):
    B, L, D = x.shape
    N = B * L
    M = mem.shape[1]
    tm = min(512, L)
    hy = dict(filt_w1=filt_w1, filt_b1=filt_b1, filt_w2=filt_w2, filt_b2=filt_b2, filt_w3=filt_w3,
              filt_b3=filt_b3, filt_w4=filt_w4, filt_b4=filt_b4, filt_freq=filt_freq,
              conv_w_c=conv_w_c, conv_b_c=conv_b_c, long_bias_c=long_bias_c)

    n_layers = w_mem_kv.shape[0]
    wkv = jnp.transpose(w_mem_kv, (1, 0, 2)).reshape(D, n_layers * 2 * MEM_WIDTH).astype(BF16)
    kv_all = _matmul(mem.reshape(B * M, D), wkv, BF16, 512, 512).reshape(B, M, n_layers, 2, MEM_WIDTH)

    w1_all = moe_w1.astype(BF16)
    w2_all = moe_w2.astype(BF16)
    b1_all = moe_b1.astype(F32)[:, :, None, :]
    b2_all = moe_b2.astype(F32)[:, :, None, :]

    q_cols, k_cols, out_rows_b = _gqa_column_perms()

    xf = x.reshape(N, D).astype(F32)
    for i in range(DEPTH):
        kind = i % N_MIXERS
        j = i // N_MIXERS
        wo = w_out[i]
        if kind == 0:
            proj = _matmul(xf, w_in_a[j].astype(BF16), BF16, tm, 3 * MIX_WIDTH + MEM_WIDTH)
            bias_tab = _na_bias_table(rpb_a[j], L // GRID_W)
            mix = _na_attention(proj.reshape(B, L, -1), bias_tab, B, L).reshape(N, MIX_WIDTH)
            qm_blk = (3 * MIX_WIDTH) // MEM_WIDTH
        elif kind == 1:
            w = w_in_b[j]
            w = jnp.concatenate([w[:, :MIX_WIDTH][:, q_cols],
                                 w[:, MIX_WIDTH:MIX_WIDTH + KV_WIDTH][:, k_cols],
                                 w[:, MIX_WIDTH + KV_WIDTH:]], axis=1)
            proj = _matmul(xf, w.astype(BF16), BF16, tm, w.shape[1])
            cos_t, sin_t = _rope_lane_tables(L)
            dim, _ = _pair_lane_dims()
            gq = q_norm_b[j].astype(F32)[dim][None, :]
            gk = k_norm_b[j].astype(F32)[dim][None, :]
            q_rot, k_rot = _qk_prep(proj, cos_t, sin_t, gq, gk, L, tm)
            mix = _gqa_attention(q_rot.reshape(B, L, -1), k_rot.reshape(B, L, -1), proj.reshape(B, L, -1),
                                 B, L, min(128, L)).reshape(N, MIX_WIDTH)
            wo = jnp.concatenate([wo[:MIX_WIDTH][out_rows_b], wo[MIX_WIDTH:]], axis=0)
            qm_blk = (MIX_WIDTH + 2 * KV_WIDTH) // MEM_WIDTH
        else:
            proj = _matmul(xf, w_in_c[j].astype(BF16), BF16, tm, HYENA_WIDTH + MEM_WIDTH)
            mix = _hyena_mixer(proj.reshape(B, L, -1), j, hy, B, L).reshape(N, MIX_WIDTH)
            qm_blk = HYENA_WIDTH // MEM_WIDTH
        kmem = kv_all[:, :, i, 0, :]
        vmem = kv_all[:, :, i, 1, :]
        rw = jnp.pad(router_w[i].astype(F32), ((0, 0), (0, LANES - N_EXPERTS)))
        rb = jnp.pad(router_b[i].astype(F32), (0, LANES - N_EXPERTS), constant_values=NEG_BIG)[None, :]
        x1, x1b, comb = _post_mixer(xf, mix, proj, qm_blk, kmem, vmem, wo.astype(BF16),
                                    ln1_g[i].astype(F32)[None, :], ln1_b[i].astype(F32)[None, :], rw, rb, L, tm)
        xf = _moe(x1b, x1, comb, w1_all, b1_all, w2_all, b2_all,
                  ln2_g[i].astype(F32)[None, :], ln2_b[i].astype(F32)[None, :], i, tm)
    return xf.reshape(B, L, D).astype(x.dtype)
```

```python
import functools
import math

import numpy as np
import jax
import jax.numpy as jnp
from jax import lax
from jax.experimental import pallas as pl
from jax.experimental.pallas import tpu as pltpu

F32 = jnp.float32
BF16 = jnp.bfloat16

D_MODEL = 1024
DEPTH = 4
GRID_W = 64
N_MIXERS = 3
HEAD_DIM = 64
MIX_WIDTH = 768
MIX_HEADS = MIX_WIDTH // HEAD_DIM
MEM_HEADS = 4
MEM_WIDTH = MEM_HEADS * HEAD_DIM
NA_KH = 8
NA_KW = 16
KV_HEADS = 4
KV_WIDTH = KV_HEADS * HEAD_DIM
GQA_GROUP = MIX_HEADS // KV_HEADS
ROPE_THETA = 10000.0
QK_NORM_EPS = 1e-6
HYENA_ORDER = 2
HYENA_BANDS = 16
HYENA_EMB = 2 * HYENA_BANDS + 1
HYENA_FFN = 64
HYENA_WIDTH = (HYENA_ORDER + 1) * MIX_WIDTH
HYENA_DECAY_TARGET = 1e-2
HYENA_FAST_DECAY = 0.3
HYENA_SLOW_DECAY = 1.5
N_EXPERTS = 32
TOP_K = 4
D_EXPERT = D_MODEL
SWIGLU_LIMIT = 7.0
SWIGLU_ALPHA = 1.702
DEEPNORM_ALPHA = (2 * DEPTH) ** 0.25
LN_EPS = 1e-5

LANES = 128
VMEM_LIMIT = 56 * 1024 * 1024
NEG_BIG = -1e30
ATT_SCALE = HEAD_DIM ** -0.5


def _cparams(sem):
    return pltpu.CompilerParams(dimension_semantics=sem, vmem_limit_bytes=VMEM_LIMIT)


def _layer_norm(x, g, b):
    mu = jnp.mean(x, -1, keepdims=True)
    xc = x - mu
    var = jnp.mean(xc * xc, -1, keepdims=True)
    return xc * lax.rsqrt(var + LN_EPS) * g + b


def _mm_kernel(x_ref, w_ref, o_ref):
    o_ref[...] = jnp.dot(x_ref[...].astype(BF16), w_ref[...],
                         preferred_element_type=F32).astype(o_ref.dtype)


def _matmul(x, w, out_dtype, tm, tn):
    M, K = x.shape
    N = w.shape[1]
    tm = min(tm, M)
    tn = min(tn, N)
    assert M % tm == 0 and N % tn == 0
    return pl.pallas_call(
        _mm_kernel,
        grid=(M // tm, N // tn),
        in_specs=[pl.BlockSpec((tm, K), lambda i, j: (i, 0)),
                  pl.BlockSpec((K, tn), lambda i, j: (0, j))],
        out_specs=pl.BlockSpec((tm, tn), lambda i, j: (i, j)),
        out_shape=jax.ShapeDtypeStruct((M, N), out_dtype),
        compiler_params=_cparams(("parallel", "arbitrary")),
        name="matmul",
    )(x, w)


def _na_bias_table(rpb, rows):
    del rows
    col = np.arange(GRID_W)
    c0 = np.clip(col - NA_KW // 2, 0, GRID_W - NA_KW)
    cc = np.arange(GRID_W)
    valid = (cc[None, :] >= c0[:, None]) & (cc[None, :] < c0[:, None] + NA_KW)
    coff = np.clip(cc[None, :] - col[:, None] + NA_KW - 1, 0, 2 * NA_KW - 2)
    d = np.arange(NA_KH)
    i = np.arange(NA_KH)
    roff = i[None, :] - d[:, None] + NA_KH - 1
    t = rpb.astype(F32)[:, roff]
    t = t[:, :, :, coff]
    t = t.transpose(0, 1, 3, 2, 4)
    t = jnp.where(valid[None, None, :, None, :], t, NEG_BIG)
    return t.reshape(rpb.shape[0], NA_KH, GRID_W, NA_KH * GRID_W)


def _na_kernel(q_ref, k_ref, v_ref, bias_ref, o_ref, *, rows):
    lane = lax.broadcasted_iota(jnp.int32, (GRID_W, LANES), 1)
    first = lane < HEAD_DIM
    band = NA_KH * GRID_W

    def body(r, carry):
        r0 = jnp.clip(r - NA_KH // 2, 0, rows - NA_KH)
        d = r - r0
        qoff = pl.multiple_of(r * GRID_W, GRID_W)
        koff = pl.multiple_of(r0 * GRID_W, GRID_W)
        q = q_ref[pl.ds(qoff, GRID_W), :]
        kb = k_ref[pl.ds(koff, band), :]
        vb = v_ref[pl.ds(koff, band), :]
        zero = jnp.zeros_like(q)
        q2 = jnp.concatenate([jnp.where(first, q, zero), jnp.where(first, zero, q)], axis=0)
        s = lax.dot_general(q2, kb, (((1,), (1,)), ((), ())), preferred_element_type=F32)
        bias = jnp.concatenate([bias_ref[0, d], bias_ref[1, d]], axis=0)
        s = s * ATT_SCALE + bias
        m = jnp.max(s, -1, keepdims=True)
        e = jnp.exp(s - m)
        l = jnp.sum(e, -1, keepdims=True)
        o2 = jnp.dot(e.astype(BF16), vb, preferred_element_type=F32) / l
        o = jnp.where(first, o2[:GRID_W], o2[GRID_W:])
        o_ref[pl.ds(qoff, GRID_W), :] = o.astype(o_ref.dtype)
        return carry

    lax.fori_loop(0, rows, body, 0)


def _na_attention(proj, bias_tab, B, L):
    rows = L // GRID_W
    assert rows >= NA_KH
    npair = MIX_WIDTH // LANES
    return pl.pallas_call(
        functools.partial(_na_kernel, rows=rows),
        grid=(npair, B),
        in_specs=[pl.BlockSpec((None, L, LANES), lambda p, b: (b, 0, p)),
                  pl.BlockSpec((None, L, LANES), lambda p, b: (b, 0, npair + p)),
                  pl.BlockSpec((None, L, LANES), lambda p, b: (b, 0, 2 * npair + p)),
                  pl.BlockSpec((2, NA_KH, GRID_W, NA_KH * GRID_W), lambda p, b: (p, 0, 0, 0))],
        out_specs=pl.BlockSpec((None, L, LANES), lambda p, b: (b, 0, p)),
        out_shape=jax.ShapeDtypeStruct((B, L, MIX_WIDTH), BF16),
        compiler_params=_cparams(("parallel", "parallel")),
        name="na_attention",
    )(proj, proj, proj, bias_tab)


def _pair_lane_dims():
    l = np.arange(LANES)
    return 2 * (l % 32) + (l >= HEAD_DIM).astype(np.int64), ((l % HEAD_DIM) >= 32).astype(np.int64)


def _gqa_column_perms():
    dim, which = _pair_lane_dims()
    q_cols = []
    out_rows = []
    for P in range(KV_HEADS // 2):
        for g in range(GQA_GROUP):
            heads = ((2 * P) * GQA_GROUP + g, (2 * P + 1) * GQA_GROUP + g)
            q_cols.append(np.array([heads[w] * HEAD_DIM + dd for dd, w in zip(dim, which)]))
            out_rows.append(np.concatenate([heads[0] * HEAD_DIM + np.arange(HEAD_DIM),
                                            heads[1] * HEAD_DIM + np.arange(HEAD_DIM)]))
    k_cols = []
    for P in range(KV_HEADS // 2):
        heads = (2 * P, 2 * P + 1)
        k_cols.append(np.array([heads[w] * HEAD_DIM + dd for dd, w in zip(dim, which)]))
    return np.concatenate(q_cols), np.concatenate(k_cols), np.concatenate(out_rows)


def _rope_lane_tables(L):
    pos = jnp.arange(L)
    row = (pos // GRID_W).astype(F32)
    col = (pos % GRID_W).astype(F32)
    axis_dim = HEAD_DIM // 2
    inv_freq = ROPE_THETA ** (-jnp.arange(0, axis_dim, 2, dtype=F32) / axis_dim)
    ang = jnp.concatenate([row[:, None] * inv_freq, col[:, None] * inv_freq], -1)
    c = jnp.cos(ang)
    s = jnp.sin(ang)
    return jnp.concatenate([c, c, c, c], -1), jnp.concatenate([-s, -s, s, s], -1)


def _qk_prep_kernel(q_ref, k_ref, cos_ref, sin_ref, gq_ref, gk_ref, qo_ref, ko_ref):
    tm = q_ref.shape[0]
    lane = lax.broadcasted_iota(jnp.int32, (tm, LANES), 1)
    head_a = (lane % HEAD_DIM) < 32
    c = cos_ref[...]
    s = sin_ref[...]

    def norm_rope(x, g):
        x = x.astype(F32)
        x2 = x * x
        tot = jnp.sum(x2, -1, keepdims=True)
        sa = jnp.sum(jnp.where(head_a, x2, 0.0), -1, keepdims=True)
        inv = jnp.where(head_a, lax.rsqrt(sa * (1.0 / HEAD_DIM) + QK_NORM_EPS),
                        lax.rsqrt((tot - sa) * (1.0 / HEAD_DIM) + QK_NORM_EPS))
        xn = x * inv * g
        return xn * c + pltpu.roll(xn, HEAD_DIM, axis=1) * s

    for j in range(q_ref.shape[1] // LANES):
        sl = slice(j * LANES, (j + 1) * LANES)
        qo_ref[:, sl] = (norm_rope(q_ref[:, sl], gq_ref[...]) * ATT_SCALE).astype(qo_ref.dtype)
    for j in range(k_ref.shape[1] // LANES):
        sl = slice(j * LANES, (j + 1) * LANES)
        ko_ref[:, sl] = norm_rope(k_ref[:, sl], gk_ref[...]).astype(ko_ref.dtype)


def _qk_prep(proj, cos_t, sin_t, gq, gk, L, tm):
    N = proj.shape[0]
    nl = L // tm
    return pl.pallas_call(
        _qk_prep_kernel,
        grid=(N // tm,),
        in_specs=[pl.BlockSpec((tm, MIX_WIDTH), lambda i: (i, 0)),
                  pl.BlockSpec((tm, KV_WIDTH), lambda i: (i, MIX_WIDTH // KV_WIDTH)),
                  pl.BlockSpec((tm, LANES), lambda i: (i % nl, 0)),
                  pl.BlockSpec((tm, LANES), lambda i: (i % nl, 0)),
                  pl.BlockSpec((1, LANES), lambda i: (0, 0)),
                  pl.BlockSpec((1, LANES), lambda i: (0, 0))],
        out_specs=[pl.BlockSpec((tm, MIX_WIDTH), lambda i: (i, 0)),
                   pl.BlockSpec((tm, KV_WIDTH), lambda i: (i, 0))],
        out_shape=[jax.ShapeDtypeStruct((N, MIX_WIDTH), BF16),
                   jax.ShapeDtypeStruct((N, KV_WIDTH), BF16)],
        compiler_params=_cparams(("parallel",)),
        name="qk_prep",
    )(proj, proj, cos_t, sin_t, gq, gk)


def _gqa_kernel(q_ref, k_ref, v_ref, o_ref):
    tq = q_ref.shape[0]
    lane = lax.broadcasted_iota(jnp.int32, (tq, LANES), 1)
    head_a = (lane % HEAD_DIM) < 32
    q = q_ref[...]
    parts = []
    for g in range(GQA_GROUP):
        qs = q[:, g * LANES:(g + 1) * LANES]
        zero = jnp.zeros_like(qs)
        parts += [jnp.where(head_a, qs, zero), jnp.where(head_a, zero, qs)]
    q6 = jnp.concatenate(parts, axis=0)
    s = lax.dot_general(q6, k_ref[...], (((1,), (1,)), ((), ())), preferred_element_type=F32)
    m = jnp.max(s, -1, keepdims=True)
    e = jnp.exp(s - m)
    l = jnp.sum(e, -1, keepdims=True)
    o6 = jnp.dot(e.astype(BF16), v_ref[...], preferred_element_type=F32) / l
    for g in range(GQA_GROUP):
        oa = o6[(2 * g) * tq:(2 * g + 1) * tq]
        ob = o6[(2 * g + 1) * tq:(2 * g + 2) * tq]
        o_ref[:, g * LANES:(g + 1) * LANES] = jnp.where(lane < HEAD_DIM, oa, ob).astype(o_ref.dtype)


def _gqa_attention(q_rot, k_rot, proj, B, L, tq):
    slab_w = GQA_GROUP * LANES
    v_blk0 = (MIX_WIDTH + KV_WIDTH) // LANES
    return pl.pallas_call(
        _gqa_kernel,
        grid=(B, KV_HEADS // 2, L // tq),
        in_specs=[pl.BlockSpec((None, tq, slab_w), lambda b, p, i: (b, i, p)),
                  pl.BlockSpec((None, L, LANES), lambda b, p, i: (b, 0, p)),
                  pl.BlockSpec((None, L, LANES), lambda b, p, i: (b, 0, v_blk0 + p))],
        out_specs=pl.BlockSpec((None, tq, slab_w), lambda b, p, i: (b, i, p)),
        out_shape=jax.ShapeDtypeStruct((B, L, MIX_WIDTH), BF16),
        compiler_params=_cparams(("parallel", "parallel", "arbitrary")),
        name="gqa_attention",
    )(q_rot, k_rot, proj)


def _hyena_features(L):
    pos = jnp.arange(L, dtype=F32)
    t = pos / max(L - 1, 1)
    w = 2.0 * math.pi * pos / L
    f = jnp.linspace(1e-4, HYENA_BANDS - 1, HYENA_BANDS, dtype=F32)
    wf = w[:, None] * f[None, :]
    z = jnp.concatenate([t[:, None], jnp.cos(wf), -jnp.sin(wf)], -1)
    return jnp.pad(z, ((0, 0), (0, LANES - HYENA_EMB))), t[:, None]


def _filter_kernel(z_ref, t_ref, w1_ref, b1_ref, w2_ref, b2_ref, w3_ref, b3_ref, w4f_ref, w4b_ref,
                   b4f_ref, b4b_ref, fr_ref, delta_ref, hsum_ref, hdiff_ref):
    hp = lax.Precision.HIGHEST
    fr = fr_ref[...]
    h = jnp.sin(fr * (jnp.dot(z_ref[...], w1_ref[...], precision=hp, preferred_element_type=F32) + b1_ref[...]))
    h = jnp.sin(fr * (jnp.dot(h, w2_ref[...], precision=hp, preferred_element_type=F32) + b2_ref[...]))
    h = jnp.sin(fr * (jnp.dot(h, w3_ref[...], precision=hp, preferred_element_type=F32) + b3_ref[...]))
    hf = jnp.dot(h, w4f_ref[...], precision=hp, preferred_element_type=F32) + b4f_ref[...]
    hb = jnp.dot(h, w4b_ref[...], precision=hp, preferred_element_type=F32) + b4b_ref[...]
    dec = jnp.exp(-t_ref[...] * delta_ref[...])
    hf = hf * dec
    hb = hb * dec
    norm = jnp.sum(jnp.abs(hf), 0, keepdims=True) + jnp.sum(jnp.abs(hb), 0, keepdims=True)
    inv = 1.0 / norm
    hf = hf * inv
    row = lax.broadcasted_iota(jnp.int32, hb.shape, 0)
    hb = jnp.where(row == 0, 0.0, hb * inv)
    hsum_ref[...] = hf + hb
    hdiff_ref[...] = hf - hb


def _hyena_filters(zf, t, w1, b1, w2, b2, w3, b3, w4, b4, freq, L, tc):
    OC = HYENA_ORDER * MIX_WIDTH
    max_decay = math.log(HYENA_DECAY_TARGET) / HYENA_FAST_DECAY
    min_decay = math.log(HYENA_DECAY_TARGET) / HYENA_SLOW_DECAY
    deltas = jnp.abs(jnp.linspace(min_decay, max_decay, MIX_WIDTH, dtype=F32))
    deltas = jnp.tile(deltas, HYENA_ORDER)[None, :]
    w1p = jnp.pad(w1.astype(F32), ((0, LANES - HYENA_EMB), (0, 0)))
    row = lambda a: a.astype(F32)[None, :]
    full = lambda shape: pl.BlockSpec(shape, lambda j: (0,) * len(shape))
    nb = OC // tc
    return pl.pallas_call(
        _filter_kernel,
        grid=(nb,),
        in_specs=[full((L, LANES)), full((L, 1)),
                  full((LANES, HYENA_FFN)), full((1, HYENA_FFN)),
                  full((HYENA_FFN, HYENA_FFN)), full((1, HYENA_FFN)),
                  full((HYENA_FFN, HYENA_FFN)), full((1, HYENA_FFN)),
                  pl.BlockSpec((HYENA_FFN, tc), lambda j: (0, j)),
                  pl.BlockSpec((HYENA_FFN, tc), lambda j: (0, nb + j)),
                  pl.BlockSpec((1, tc), lambda j: (0, j)),
                  pl.BlockSpec((1, tc), lambda j: (0, nb + j)),
                  full((1, HYENA_FFN)),
                  pl.BlockSpec((1, tc), lambda j: (0, j))],
        out_specs=[pl.BlockSpec((L, tc), lambda j: (0, j)),
                   pl.BlockSpec((L, tc), lambda j: (0, j))],
        out_shape=[jax.ShapeDtypeStruct((L, OC), F32), jax.ShapeDtypeStruct((L, OC), F32)],
        compiler_params=_cparams(("parallel",)),
        name="hyena_filters",
    )(zf, t, w1p, row(b1), w2.astype(F32), row(b2), w3.astype(F32), row(b3), w4.astype(F32), w4.astype(F32),
      row(b4), row(b4), row(freq), deltas)


def _dft_matrices(L):
    n = 2 * L
    k = jnp.arange(L, dtype=jnp.int32)
    ks = (k[:, None] * k[None, :]) % n
    ang = ks.astype(F32) * (2.0 * math.pi / n)
    fc = jnp.cos(ang)
    sn = jnp.sin(ang)
    alt = jnp.where(k % 2 == 0, 1.0, -1.0).astype(F32)
    fs = jnp.where(k[:, None] == 0, alt[None, :], sn)
    fst = jnp.where(k[None, :] == 0, alt[:, None], sn)
    return fc.astype(BF16), fs.astype(BF16), fst.astype(BF16)


def _short_conv_kernel(u_ref, w_ref, b_ref, o_ref):
    L = u_ref.shape[0]
    u = u_ref[...].astype(F32)
    row = lax.broadcasted_iota(jnp.int32, u.shape, 0)
    prev = jnp.where(row == 0, 0.0, pltpu.roll(u, 1, axis=0))
    nxt = jnp.where(row == L - 1, 0.0, pltpu.roll(u, L - 1, axis=0))
    w = w_ref[...]
    o_ref[...] = (prev * w[0:1] + u * w[1:2] + nxt * w[2:3] + b_ref[...]).astype(o_ref.dtype)


def _short_conv(proj, conv_w, conv_b, B, L, tc):
    return pl.pallas_call(
        _short_conv_kernel,
        grid=(B, HYENA_WIDTH // tc),
        in_specs=[pl.BlockSpec((None, L, tc), lambda b, j: (b, 0, j)),
                  pl.BlockSpec((3, tc), lambda b, j: (0, j)),
                  pl.BlockSpec((1, tc), lambda b, j: (0, j))],
        out_specs=pl.BlockSpec((None, L, tc), lambda b, j: (b, 0, j)),
        out_shape=jax.ShapeDtypeStruct((B, L, HYENA_WIDTH), BF16),
        compiler_params=_cparams(("parallel", "parallel")),
        name="short_conv",
    )(proj, conv_w.astype(F32), conv_b.astype(F32)[None, :])


def _hy_fwd_kernel(fc_ref, fs_ref, z_ref, pa_ref, qa_ref, qb_ref, sb_ref, ya_ref, yb_ref):
    z = z_ref[...]
    a = jnp.dot(fc_ref[...], z, preferred_element_type=F32)
    b = jnp.dot(fs_ref[...], z, preferred_element_type=F32)
    ya_ref[...] = (a * pa_ref[...] - b * qa_ref[...]).astype(ya_ref.dtype)
    yb_ref[...] = (a * qb_ref[...] + b * sb_ref[...]).astype(yb_ref.dtype)


def _hy_inv_kernel(fc_ref, fst_ref, ya_ref, yb_ref, z_ref, xg_ref, bias_ref, o_ref):
    y = jnp.dot(fc_ref[...], ya_ref[...], preferred_element_type=F32)
    y = y + jnp.dot(fst_ref[...], yb_ref[...], preferred_element_type=F32)
    z = z_ref[...].astype(F32)
    o_ref[...] = (xg_ref[...].astype(F32) * (y + bias_ref[...] * z)).astype(o_ref.dtype)


def _hyena_long_conv_gate(uc, z_arr, z_blk, xg_blk, fc, fs, fst, coef, bias, B, L, tm_f, tm_i):
    C = MIX_WIDTH
    pa, qa, qb, sb = coef
    ya, yb = pl.pallas_call(
        _hy_fwd_kernel,
        grid=(B, L // tm_f),
        in_specs=[pl.BlockSpec((tm_f, L), lambda b, i: (i, 0)),
                  pl.BlockSpec((tm_f, L), lambda b, i: (i, 0)),
                  pl.BlockSpec((None, L, C), lambda b, i: (b, 0, z_blk)),
                  pl.BlockSpec((tm_f, C), lambda b, i: (i, 0)),
                  pl.BlockSpec((tm_f, C), lambda b, i: (i, 0)),
                  pl.BlockSpec((tm_f, C), lambda b, i: (i, 0)),
                  pl.BlockSpec((tm_f, C), lambda b, i: (i, 0))],
        out_specs=[pl.BlockSpec((None, tm_f, C), lambda b, i: (b, i, 0)),
                   pl.BlockSpec((None, tm_f, C), lambda b, i: (b, i, 0))],
        out_shape=[jax.ShapeDtypeStruct((B, L, C), BF16), jax.ShapeDtypeStruct((B, L, C), BF16)],
        compiler_params=_cparams(("parallel", "arbitrary")),
        name="hyena_dft_fwd",
    )(fc, fs, z_arr, pa, qa, qb, sb)
    return pl.pallas_call(
        _hy_inv_kernel,
        grid=(B, L // tm_i),
        in_specs=[pl.BlockSpec((tm_i, L), lambda b, i: (i, 0)),
                  pl.BlockSpec((tm_i, L), lambda b, i: (i, 0)),
                  pl.BlockSpec((None, L, C), lambda b, i: (b, 0, 0)),
                  pl.BlockSpec((None, L, C), lambda b, i: (b, 0, 0)),
                  pl.BlockSpec((None, tm_i, C), lambda b, i: (b, i, z_blk)),
                  pl.BlockSpec((None, tm_i, C), lambda b, i: (b, i, xg_blk)),
                  pl.BlockSpec((1, C), lambda b, i: (0, 0))],
        out_specs=pl.BlockSpec((None, tm_i, C), lambda b, i: (b, i, 0)),
        out_shape=jax.ShapeDtypeStruct((B, L, C), BF16),
        compiler_params=_cparams(("parallel", "arbitrary")),
        name="hyena_dft_inv",
    )(fc, fst, ya, yb, z_arr, uc, bias)


def _hyena_mixer(proj, j, p, B, L):
    C = MIX_WIDTH
    OC = HYENA_ORDER * C
    n = 2 * L
    zf, t = _hyena_features(L)
    hsum, hdiff = _hyena_filters(zf, t, p['filt_w1'][j], p['filt_b1'][j], p['filt_w2'][j], p['filt_b2'][j],
                                 p['filt_w3'][j], p['filt_b3'][j], p['filt_w4'][j], p['filt_b4'][j],
                                 p['filt_freq'][j], L, 256)
    fc, fs, fst = _dft_matrices(L)
    parts = []
    for h in (hsum, hdiff):
        hi = h.astype(BF16)
        parts += [hi, (h - hi.astype(F32)).astype(BF16)]
    hcat = jnp.concatenate(parts, axis=1)
    spec = _matmul(jnp.concatenate([fc, fs], axis=0), hcat, F32, 512, 512)
    gc = spec[:L, 0:OC] + spec[:L, OC:2 * OC]
    nyq = spec[L:L + 1, 0:OC] + spec[L:L + 1, OC:2 * OC]
    gs = spec[L:, 2 * OC:3 * OC] + spec[L:, 3 * OC:4 * OC]
    k0 = (jnp.arange(L) == 0)[:, None]
    ca = jnp.where(k0, 1.0 / n, 2.0 / n).astype(F32)
    pa = ca * gc
    qa = ca * jnp.where(k0, 0.0, gs)
    qb = qa
    sb = ca * jnp.where(k0, nyq, gc)
    uc = _short_conv(proj, p['conv_w_c'][j], p['conv_b_c'][j], B, L, 256)
    bias = p['long_bias_c'][j].astype(F32)
    tm_f = min(512, L)
    tm_i = min(256, L)
    z = uc
    z_blk = 0
    for o in range(HYENA_ORDER):
        sl = slice(o * C, (o + 1) * C)
        coef = (pa[:, sl], qa[:, sl], qb[:, sl], sb[:, sl])
        z = _hyena_long_conv_gate(uc, z, z_blk, o + 1, fc, fs, fst, coef, bias[o][None, :], B, L, tm_f, tm_i)
        z_blk = 0
    return z


def _pack_halves(x):
    c = x.shape[1] // 2
    bits = lax.bitcast_convert_type(x.astype(BF16).astype(F32), jnp.uint32)
    return (bits[:, :c] & jnp.uint32(0xFFFF0000)) | (bits[:, c:] >> 16)


def _unpack_halves(w):
    hi = lax.bitcast_convert_type(w & jnp.uint32(0xFFFF0000), F32)
    lo = lax.bitcast_convert_type(w << 16, F32)
    return hi, lo


def _post_kernel(x_ref, mix_ref, qm_ref, km_ref, vm_ref, wo_ref, g_ref, b_ref, rw_ref, rb_ref,
                 xo_ref, xp_ref, gate_ref, route_ref, count_ref, base_ref):
    tm = x_ref.shape[0]
    lane = lax.broadcasted_iota(jnp.int32, (tm, LANES), 1)
    first = lane < HEAD_DIM
    outs = []
    for p in range(MEM_WIDTH // LANES):
        sl = slice(p * LANES, (p + 1) * LANES)
        qs = qm_ref[:, sl]
        zero = jnp.zeros_like(qs)
        q2 = jnp.concatenate([jnp.where(first, qs, zero), jnp.where(first, zero, qs)], axis=0)
        s = lax.dot_general(q2, km_ref[:, sl], (((1,), (1,)), ((), ())), preferred_element_type=F32) * ATT_SCALE
        m = jnp.max(s, -1, keepdims=True)
        e = jnp.exp(s - m)
        l = jnp.sum(e, -1, keepdims=True)
        o2 = jnp.dot(e.astype(BF16), vm_ref[:, sl], preferred_element_type=F32) / l
        outs.append(jnp.where(first, o2[:tm], o2[tm:]).astype(BF16))
    mem_out = jnp.concatenate(outs, axis=1)
    acc = jnp.dot(mix_ref[...], wo_ref[:MIX_WIDTH, :], preferred_element_type=F32)
    acc = acc + jnp.dot(mem_out, wo_ref[MIX_WIDTH:, :], preferred_element_type=F32)
    x1 = _layer_norm(DEEPNORM_ALPHA * x_ref[...] + acc, g_ref[...], b_ref[...])
    xo_ref[...] = x1
    xp_ref[...] = _pack_halves(x1)
    logits = jnp.dot(x1, rw_ref[...], precision=lax.Precision.HIGHEST, preferred_element_type=F32) + rb_ref[...]
    lane_f = lane.astype(F32)
    work = logits
    chosen = jnp.zeros_like(logits)
    gates = jnp.zeros_like(logits)
    denom = jnp.zeros((tm, 1), F32)
    sels, idxs = [], []
    m0 = None
    for k in range(TOP_K):
        m = jnp.max(work, -1, keepdims=True)
        idx = jnp.min(jnp.where(work == m, lane_f, float(LANES)), -1, keepdims=True)
        sel = lane_f == idx
        if k == 0:
            m0 = m
        ek = jnp.exp(m - m0)
        gates = jnp.where(lane == k, ek, gates)
        chosen = chosen + jnp.where(sel, 1.0, 0.0)
        denom = denom + ek
        work = jnp.where(sel, -jnp.inf, work)
        sels.append(sel)
        idxs.append(idx)
    gate_ref[...] = gates / denom

    @pl.when(pl.program_id(0) == 0)
    def _():
        base_ref[...] = jnp.zeros_like(base_ref)

    r_i = lax.broadcasted_iota(jnp.int32, (tm, tm), 0)
    c_i = lax.broadcasted_iota(jnp.int32, (tm, tm), 1)
    before = jnp.where(r_i > c_i, 1.0, 0.0).astype(BF16)
    prefix = jnp.dot(before, chosen.astype(BF16), preferred_element_type=F32) + base_ref[...]
    route = jnp.zeros((tm, LANES), jnp.int32)
    for k in range(TOP_K):
        rank = jnp.sum(jnp.where(sels[k], prefix, 0.0), -1, keepdims=True)
        route = jnp.where(lane == k, rank.astype(jnp.int32), route)
        route = jnp.where(lane == TOP_K + k, idxs[k].astype(jnp.int32), route)
    route_ref[...] = route
    base_ref[...] += jnp.sum(chosen, 0, keepdims=True)
    count_ref[...] = base_ref[...]


def _post_mixer(x, mix, proj, qm_blk, kmem, vmem, wo, g, b, rw, rb, L, tm):
    N, D = x.shape
    nl = L // tm
    CAT = MIX_WIDTH + MEM_WIDTH
    return pl.pallas_call(
        _post_kernel,
        grid=(N // tm,),
        in_specs=[pl.BlockSpec((tm, D), lambda i: (i, 0)),
                  pl.BlockSpec((tm, MIX_WIDTH), lambda i: (i, 0)),
                  pl.BlockSpec((tm, MEM_WIDTH), lambda i: (i, qm_blk)),
                  pl.BlockSpec((None, kmem.shape[1], MEM_WIDTH), lambda i: (i // nl, 0, 0)),
                  pl.BlockSpec((None, vmem.shape[1], MEM_WIDTH), lambda i: (i // nl, 0, 0)),
                  pl.BlockSpec((CAT, D), lambda i: (0, 0)),
                  pl.BlockSpec((1, D), lambda i: (0, 0)),
                  pl.BlockSpec((1, D), lambda i: (0, 0)),
                  pl.BlockSpec((D, LANES), lambda i: (0, 0)),
                  pl.BlockSpec((1, LANES), lambda i: (0, 0))],
        out_specs=[pl.BlockSpec((tm, D), lambda i: (i, 0)),
                   pl.BlockSpec((tm, D // 2), lambda i: (i, 0)),
                   pl.BlockSpec((tm, LANES), lambda i: (i, 0)),
                   pl.BlockSpec((tm, LANES), lambda i: (i, 0)),
                   pl.BlockSpec((1, LANES), lambda i: (0, 0))],
        out_shape=[jax.ShapeDtypeStruct((N, D), F32), jax.ShapeDtypeStruct((N, D // 2), jnp.uint32),
                   jax.ShapeDtypeStruct((N, LANES), F32), jax.ShapeDtypeStruct((N, LANES), jnp.int32),
                   jax.ShapeDtypeStruct((1, LANES), F32)],
        scratch_shapes=[pltpu.VMEM((1, LANES), F32)],
        compiler_params=_cparams(("arbitrary",)),
        name="post_mixer",
    )(x, mix, proj, kmem, vmem, wo, g, b, rw, rb)


EXPERT_TILE = 512


def _route_tables(route, counts, n_tok_tiles, max_tiles):
    cnt = counts[0, :N_EXPERTS].astype(jnp.int32)
    padded = ((cnt + EXPERT_TILE - 1) // EXPERT_TILE) * EXPERT_TILE
    ends = jnp.cumsum(padded)
    offs = ends - padded
    rank = route[:, :TOP_K]
    eid = route[:, TOP_K:2 * TOP_K]
    pos = (jnp.take(offs, eid) + rank).reshape(-1)
    tile_ends = ends // EXPERT_TILE
    n_tiles = tile_ends[-1]
    t = jnp.arange(max_tiles, dtype=jnp.int32)
    tile_expert = jnp.searchsorted(tile_ends, jnp.minimum(t, n_tiles - 1), side='right').astype(jnp.int32)
    tile_expert = jnp.minimum(tile_expert, N_EXPERTS - 1)
    zero_start = (ends - EXPERT_TILE).astype(jnp.int32)
    zero_valid = (padded > 0).astype(jnp.int32)
    return pos, tile_expert, n_tiles.reshape(1).astype(jnp.int32), zero_start, zero_valid


def _index_copy(pos_hbm, idx_smem, idx_sem, step, slot, n_idx):
    src = pos_hbm.at[pl.ds(pl.multiple_of(step * n_idx, n_idx), n_idx)]
    dst = idx_smem.at[pl.ds(pl.multiple_of(slot * n_idx, n_idx), n_idx)]
    return pltpu.make_async_copy(src, dst, idx_sem.at[slot])


def _dispatch_kernel(zstart_ref, zvalid_ref, xp_ref, pos_hbm, xs_hbm, idx_smem, zero_ref, idx_sem, row_sem, zero_sem):
    i = pl.program_id(0)
    tm = xp_ref.shape[0]
    n_idx = tm * TOP_K
    slot = i % 2

    @pl.when(i == 0)
    def _():
        _index_copy(pos_hbm, idx_smem, idx_sem, 0, 0, n_idx).start()
        zero_ref[...] = jnp.zeros_like(zero_ref)

        def zero_copy(e):
            start = pl.multiple_of(zstart_ref[e], EXPERT_TILE)
            return pltpu.make_async_copy(zero_ref, xs_hbm.at[pl.ds(start, EXPERT_TILE)], zero_sem)

        for e in range(N_EXPERTS):
            @pl.when(zvalid_ref[e] > 0)
            def _():
                zero_copy(e).start()
        for e in range(N_EXPERTS):
            @pl.when(zvalid_ref[e] > 0)
            def _():
                zero_copy(e).wait()

    @pl.when(i + 1 < pl.num_programs(0))
    def _():
        _index_copy(pos_hbm, idx_smem, idx_sem, i + 1, 1 - slot, n_idx).start()

    _index_copy(pos_hbm, idx_smem, idx_sem, i, slot, n_idx).wait()
    base = slot * n_idx

    def issue(n, carry):
        for k in range(TOP_K):
            p = idx_smem[base + n * TOP_K + k]
            pltpu.make_async_copy(xp_ref.at[pl.ds(n, 1)], xs_hbm.at[pl.ds(p, 1)], row_sem).start()
        return carry

    lax.fori_loop(0, tm, issue, 0, unroll=8)
    for k in range(TOP_K):
        pltpu.make_async_copy(xp_ref, xs_hbm.at[pl.ds(0, tm)], row_sem).wait()


def _dispatch(xp, pos, zero_start, zero_valid, rows, tm):
    N, W = xp.shape
    return pl.pallas_call(
        _dispatch_kernel,
        grid_spec=pltpu.PrefetchScalarGridSpec(
            num_scalar_prefetch=2,
            grid=(N // tm,),
            in_specs=[pl.BlockSpec((tm, W), lambda i, zs, zv: (i, 0)),
                      pl.BlockSpec(memory_space=pl.ANY)],
            out_specs=pl.BlockSpec(memory_space=pl.ANY),
            scratch_shapes=[pltpu.SMEM((2 * tm * TOP_K,), jnp.int32),
                            pltpu.VMEM((EXPERT_TILE, W), jnp.uint32),
                            pltpu.SemaphoreType.DMA((2,)),
                            pltpu.SemaphoreType.DMA,
                            pltpu.SemaphoreType.DMA]),
        out_shape=jax.ShapeDtypeStruct((rows, W), jnp.uint32),
        compiler_params=_cparams(("arbitrary",)),
        name="moe_dispatch",
    )(zero_start, zero_valid, xp, pos)


def _expert_kernel(te_ref, nt_ref, xs_ref, w1_ref, b1_ref, w2_ref, b2_ref, o_ref):
    del te_ref

    @pl.when(pl.program_id(0) < nt_ref[0])
    def _():
        hi, lo = _unpack_halves(xs_ref[...])
        half = hi.shape[1]
        h = jnp.dot(hi.astype(BF16), w1_ref[:half, :], preferred_element_type=F32)
        h = h + jnp.dot(lo.astype(BF16), w1_ref[half:, :], preferred_element_type=F32) + b1_ref[...]
        gl = jnp.minimum(h[:, :D_EXPERT], SWIGLU_LIMIT)
        ul = jnp.clip(h[:, D_EXPERT:], -SWIGLU_LIMIT, SWIGLU_LIMIT)
        a = (ul + 1.0) * (gl * jax.nn.sigmoid(SWIGLU_ALPHA * gl))
        o = jnp.dot(a.astype(BF16), w2_ref[...], preferred_element_type=F32) + b2_ref[...]
        o_ref[...] = _pack_halves(o)


def _experts(xs, tile_expert, n_tiles, w1, b1, w2, b2, layer):
    rows, W = xs.shape
    D = 2 * W
    last = lambda t, te, nt: jnp.minimum(t, nt[0] - 1)
    return pl.pallas_call(
        _expert_kernel,
        grid_spec=pltpu.PrefetchScalarGridSpec(
            num_scalar_prefetch=2,
            grid=(rows // EXPERT_TILE,),
            in_specs=[pl.BlockSpec((EXPERT_TILE, W), lambda t, te, nt: (last(t, te, nt), 0)),
                      pl.BlockSpec((None, None, D, 2 * D_EXPERT), lambda t, te, nt: (layer, te[t], 0, 0)),
                      pl.BlockSpec((None, None, 1, 2 * D_EXPERT), lambda t, te, nt: (layer, te[t], 0, 0)),
                      pl.BlockSpec((None, None, D_EXPERT, D), lambda t, te, nt: (layer, te[t], 0, 0)),
                      pl.BlockSpec((None, None, 1, D), lambda t, te, nt: (layer, te[t], 0, 0))],
            out_specs=pl.BlockSpec((EXPERT_TILE, W), lambda t, te, nt: (last(t, te, nt), 0))),
        out_shape=jax.ShapeDtypeStruct((rows, W), jnp.uint32),
        compiler_params=_cparams(("arbitrary",)),
        name="moe_experts",
    )(tile_expert, n_tiles, xs, w1, b1, w2, b2)


def _combine_kernel(x_ref, gate_ref, g_ref, b_ref, pos_hbm, eo_hbm, o_ref, idx_smem, buf_ref, idx_sem, row_sem):
    i = pl.program_id(0)
    tm = x_ref.shape[0]
    n_idx = tm * TOP_K
    slot = i % 2

    @pl.when(i == 0)
    def _():
        _index_copy(pos_hbm, idx_smem, idx_sem, 0, 0, n_idx).start()

    @pl.when(i + 1 < pl.num_programs(0))
    def _():
        _index_copy(pos_hbm, idx_smem, idx_sem, i + 1, 1 - slot, n_idx).start()

    _index_copy(pos_hbm, idx_smem, idx_sem, i, slot, n_idx).wait()
    base = slot * n_idx

    def issue(n, carry):
        for k in range(TOP_K):
            p = idx_smem[base + n * TOP_K + k]
            pltpu.make_async_copy(eo_hbm.at[pl.ds(p, 1)], buf_ref.at[k, pl.ds(n, 1)], row_sem).start()
        return carry

    lax.fori_loop(0, tm, issue, 0, unroll=8)
    for k in range(TOP_K):
        pltpu.make_async_copy(eo_hbm.at[pl.ds(0, tm)], buf_ref.at[k], row_sem).wait()

    gates = gate_ref[...]
    lane = lax.broadcasted_iota(jnp.int32, gates.shape, 1)
    ya = yb = None
    for k in range(TOP_K):
        gk = jnp.sum(jnp.where(lane == k, gates, 0.0), -1, keepdims=True)
        hi, lo = _unpack_halves(buf_ref[k])
        ya = gk * hi if ya is None else ya + gk * hi
        yb = gk * lo if yb is None else yb + gk * lo
    y = jnp.concatenate([ya, yb], axis=1)
    o_ref[...] = _layer_norm(DEEPNORM_ALPHA * x_ref[...] + y, g_ref[...], b_ref[...])


def _combine(x, gates, g, b, pos, eo, tm):
    N, D = x.shape
    W = eo.shape[1]
    return pl.pallas_call(
        _combine_kernel,
        grid=(N // tm,),
        in_specs=[pl.BlockSpec((tm, D), lambda i: (i, 0)),
                  pl.BlockSpec((tm, LANES), lambda i: (i, 0)),
                  pl.BlockSpec((1, D), lambda i: (0, 0)),
                  pl.BlockSpec((1, D), lambda i: (0, 0)),
                  pl.BlockSpec(memory_space=pl.ANY),
                  pl.BlockSpec(memory_space=pl.ANY)],
        out_specs=pl.BlockSpec((tm, D), lambda i: (i, 0)),
        out_shape=jax.ShapeDtypeStruct((N, D), F32),
        scratch_shapes=[pltpu.SMEM((2 * tm * TOP_K,), jnp.int32),
                        pltpu.VMEM((TOP_K, tm, W), jnp.uint32),
                        pltpu.SemaphoreType.DMA((2,)),
                        pltpu.SemaphoreType.DMA],
        compiler_params=_cparams(("arbitrary",)),
        name="moe_combine",
    )(x, gates, g, b, pos, eo)


def _moe(x1, x1p, gates, route, counts, w1, b1, w2, b2, g, b, layer, tm):
    N = x1.shape[0]
    max_tiles = (N * TOP_K) // EXPERT_TILE + N_EXPERTS
    pos, tile_expert, n_tiles, zero_start, zero_valid = _route_tables(route, counts, N // tm, max_tiles)
    xs = _dispatch(x1p, pos, zero_start, zero_valid, max_tiles * EXPERT_TILE, tm)
    eo = _experts(xs, tile_expert, n_tiles, w1, b1, w2, b2, layer)
    return _combine(x1, gates, g, b, pos, eo, tm)


def kernel(x, mem, w_in_a, rpb_a, w_in_b, q_norm_b, k_norm_b, w_in_c, conv_w_c, conv_b_c, filt_w1, filt_b1, filt_w2, filt_b2, filt_w3, filt_b3, filt_w4, filt_b4, filt_freq, long_bias_c, w_mem_kv, w_out, ln1_g, ln1_b, ln2_g, ln2_b, router_w, router_b, moe_w1, moe_b1, moe_w2, moe_b2):
    B, L, D = x.shape
    N = B * L
    M = mem.shape[1]
    tm = min(512, L)
    hy = dict(filt_w1=filt_w1, filt_b1=filt_b1, filt_w2=filt_w2, filt_b2=filt_b2, filt_w3=filt_w3,
              filt_b3=filt_b3, filt_w4=filt_w4, filt_b4=filt_b4, filt_freq=filt_freq,
              conv_w_c=conv_w_c, conv_b_c=conv_b_c, long_bias_c=long_bias_c)

    n_layers = w_mem_kv.shape[0]
    wkv = jnp.transpose(w_mem_kv, (1, 0, 2)).reshape(D, n_layers * 2 * MEM_WIDTH).astype(BF16)
    kv_all = _matmul(mem.reshape(B * M, D), wkv, BF16, 512, 512).reshape(B, M, n_layers, 2, MEM_WIDTH)

    w1_all = moe_w1.astype(BF16)
    w2_all = moe_w2.astype(BF16)
    b1_all = moe_b1.astype(F32)[:, :, None, :]
    b2_all = moe_b2.astype(F32)[:, :, None, :]

    q_cols, k_cols, out_rows_b = _gqa_column_perms()

    xf = x.reshape(N, D).astype(F32)
    for i in range(DEPTH):
        kind = i % N_MIXERS
        j = i // N_MIXERS
        wo = w_out[i]
        if kind == 0:
            proj = _matmul(xf, w_in_a[j].astype(BF16), BF16, tm, 3 * MIX_WIDTH + MEM_WIDTH)
            bias_tab = _na_bias_table(rpb_a[j], L // GRID_W)
            mix = _na_attention(proj.reshape(B, L, -1), bias_tab, B, L).reshape(N, MIX_WIDTH)
            qm_blk = (3 * MIX_WIDTH) // MEM_WIDTH
        elif kind == 1:
            w = w_in_b[j]
            w = jnp.concatenate([w[:, :MIX_WIDTH][:, q_cols],
                                 w[:, MIX_WIDTH:MIX_WIDTH + KV_WIDTH][:, k_cols],
                                 w[:, MIX_WIDTH + KV_WIDTH:]], axis=1)
            proj = _matmul(xf, w.astype(BF16), BF16, tm, w.shape[1])
            cos_t, sin_t = _rope_lane_tables(L)
            dim, _ = _pair_lane_dims()
            gq = q_norm_b[j].astype(F32)[dim][None, :]
            gk = k_norm_b[j].astype(F32)[dim][None, :]
            q_rot, k_rot = _qk_prep(proj, cos_t, sin_t, gq, gk, L, tm)
            mix = _gqa_attention(q_rot.reshape(B, L, -1), k_rot.reshape(B, L, -1), proj.reshape(B, L, -1),
                                 B, L, min(128, L)).reshape(N, MIX_WIDTH)
            wo = jnp.concatenate([wo[:MIX_WIDTH][out_rows_b], wo[MIX_WIDTH:]], axis=0)
            qm_blk = (MIX_WIDTH + 2 * KV_WIDTH) // MEM_WIDTH
        else:
            proj = _matmul(xf, w_in_c[j].astype(BF16), BF16, tm, HYENA_WIDTH + MEM_WIDTH)
            mix = _hyena_mixer(proj.reshape(B, L, -1), j, hy, B, L).reshape(N, MIX_WIDTH)
            qm_blk = HYENA_WIDTH // MEM_WIDTH
        kmem = kv_all[:, :, i, 0, :]
        vmem = kv_all[:, :, i, 1, :]
        rw = jnp.pad(router_w[i].astype(F32), ((0, 0), (0, LANES - N_EXPERTS)))
        rb = jnp.pad(router_b[i].astype(F32), (0, LANES - N_EXPERTS), constant_values=NEG_BIG)[None, :]
        x1, x1p, gates, route, counts = _post_mixer(
            xf, mix, proj, qm_blk, kmem, vmem, wo.astype(BF16),
            ln1_g[i].astype(F32)[None, :], ln1_b[i].astype(F32)[None, :], rw, rb, L, tm)
        xf = _moe(x1, x1p, gates, route, counts, w1_all, b1_all, w2_all, b2_all,
                  ln2_g[i].astype(F32)[None, :], ln2_b[i].astype(F32)[None, :], i, tm)
    return xf.reshape(B, L, D).astype(x.dtype)
```

```python
import functools
import math

import numpy as np
import jax
import jax.numpy as jnp
from jax import lax
from jax.experimental import pallas as pl
from jax.experimental.pallas import tpu as pltpu

F32 = jnp.float32
BF16 = jnp.bfloat16

D_MODEL = 1024
DEPTH = 4
GRID_W = 64
N_MIXERS = 3
HEAD_DIM = 64
MIX_WIDTH = 768
MIX_HEADS = MIX_WIDTH // HEAD_DIM
MEM_HEADS = 4
MEM_WIDTH = MEM_HEADS * HEAD_DIM
NA_KH = 8
NA_KW = 16
NA_ROW_UNROLL = 8
KV_HEADS = 4
KV_WIDTH = KV_HEADS * HEAD_DIM
GQA_GROUP = MIX_HEADS // KV_HEADS
ROPE_THETA = 10000.0
QK_NORM_EPS = 1e-6
HYENA_ORDER = 2
HYENA_BANDS = 16
HYENA_EMB = 2 * HYENA_BANDS + 1
HYENA_FFN = 64
HYENA_WIDTH = (HYENA_ORDER + 1) * MIX_WIDTH
HYENA_DECAY_TARGET = 1e-2
HYENA_FAST_DECAY = 0.3
HYENA_SLOW_DECAY = 1.5
N_EXPERTS = 32
TOP_K = 4
D_EXPERT = D_MODEL
SWIGLU_LIMIT = 7.0
SWIGLU_ALPHA = 1.702
DEEPNORM_ALPHA = (2 * DEPTH) ** 0.25
LN_EPS = 1e-5

LANES = 128
VMEM_LIMIT = 56 * 1024 * 1024
NEG_BIG = -1e30
ATT_SCALE = HEAD_DIM ** -0.5


def _cparams(sem):
    return pltpu.CompilerParams(dimension_semantics=sem, vmem_limit_bytes=VMEM_LIMIT)


def _layer_norm(x, g, b):
    mu = jnp.mean(x, -1, keepdims=True)
    xc = x - mu
    var = jnp.mean(xc * xc, -1, keepdims=True)
    return xc * lax.rsqrt(var + LN_EPS) * g + b


def _mm_kernel(x_ref, w_ref, o_ref):
    o_ref[...] = jnp.dot(x_ref[...].astype(BF16), w_ref[...],
                         preferred_element_type=F32).astype(o_ref.dtype)


def _matmul(x, w, out_dtype, tm, tn):
    M, K = x.shape
    N = w.shape[1]
    tm = min(tm, M)
    tn = min(tn, N)
    assert M % tm == 0 and N % tn == 0
    return pl.pallas_call(
        _mm_kernel,
        grid=(M // tm, N // tn),
        in_specs=[pl.BlockSpec((tm, K), lambda i, j: (i, 0)),
                  pl.BlockSpec((K, tn), lambda i, j: (0, j))],
        out_specs=pl.BlockSpec((tm, tn), lambda i, j: (i, j)),
        out_shape=jax.ShapeDtypeStruct((M, N), out_dtype),
        compiler_params=_cparams(("parallel", "arbitrary")),
        name="matmul",
    )(x, w)


def _na_bias_table(rpb, rows):
    del rows
    col = np.arange(GRID_W)
    c0 = np.clip(col - NA_KW // 2, 0, GRID_W - NA_KW)
    cc = np.arange(GRID_W)
    valid = (cc[None, :] >= c0[:, None]) & (cc[None, :] < c0[:, None] + NA_KW)
    coff = np.clip(cc[None, :] - col[:, None] + NA_KW - 1, 0, 2 * NA_KW - 2)
    d = np.arange(NA_KH)
    i = np.arange(NA_KH)
    roff = i[None, :] - d[:, None] + NA_KH - 1
    t = rpb.astype(F32)[:, roff]
    t = t[:, :, :, coff]
    t = t.transpose(0, 1, 3, 2, 4)
    t = jnp.where(valid[None, None, :, None, :], t, NEG_BIG)
    return t.reshape(rpb.shape[0], NA_KH, GRID_W, NA_KH * GRID_W)


def _na_kernel(q_ref, k_ref, v_ref, bias_ref, o_ref, *, rows):
    lane = lax.broadcasted_iota(jnp.int32, (GRID_W, LANES), 1)
    first = lane < HEAD_DIM
    band = NA_KH * GRID_W

    def body(r, carry):
        r0 = jnp.clip(r - NA_KH // 2, 0, rows - NA_KH)
        d = r - r0
        qoff = pl.multiple_of(r * GRID_W, GRID_W)
        koff = pl.multiple_of(r0 * GRID_W, GRID_W)
        q = q_ref[pl.ds(qoff, GRID_W), :]
        kb = k_ref[pl.ds(koff, band), :]
        vb = v_ref[pl.ds(koff, band), :]
        zero = jnp.zeros_like(q)
        q2 = jnp.concatenate([jnp.where(first, q, zero), jnp.where(first, zero, q)], axis=0)
        s = lax.dot_general(q2, kb, (((1,), (1,)), ((), ())), preferred_element_type=F32)
        bias = jnp.concatenate([bias_ref[0, d], bias_ref[1, d]], axis=0)
        s = s * ATT_SCALE + bias
        m = jnp.max(s, -1, keepdims=True)
        e = jnp.exp(s - m)
        l = jnp.sum(e, -1, keepdims=True)
        o2 = jnp.dot(e.astype(BF16), vb, preferred_element_type=F32) / l
        o = jnp.where(first, o2[:GRID_W], o2[GRID_W:])
        o_ref[pl.ds(qoff, GRID_W), :] = o.astype(o_ref.dtype)
        return carry

    lax.fori_loop(0, rows, body, 0, unroll=NA_ROW_UNROLL)


def _na_attention(proj, bias_tab, B, L):
    rows = L // GRID_W
    assert rows >= NA_KH
    npair = MIX_WIDTH // LANES
    return pl.pallas_call(
        functools.partial(_na_kernel, rows=rows),
        grid=(npair, B),
        in_specs=[pl.BlockSpec((None, L, LANES), lambda p, b: (b, 0, p)),
                  pl.BlockSpec((None, L, LANES), lambda p, b: (b, 0, npair + p)),
                  pl.BlockSpec((None, L, LANES), lambda p, b: (b, 0, 2 * npair + p)),
                  pl.BlockSpec((2, NA_KH, GRID_W, NA_KH * GRID_W), lambda p, b: (p, 0, 0, 0))],
        out_specs=pl.BlockSpec((None, L, LANES), lambda p, b: (b, 0, p)),
        out_shape=jax.ShapeDtypeStruct((B, L, MIX_WIDTH), BF16),
        compiler_params=_cparams(("parallel", "parallel")),
        name="na_attention",
    )(proj, proj, proj, bias_tab)


def _pair_lane_dims():
    l = np.arange(LANES)
    return 2 * (l % 32) + (l >= HEAD_DIM).astype(np.int64), ((l % HEAD_DIM) >= 32).astype(np.int64)


def _gqa_column_perms():
    dim, which = _pair_lane_dims()
    q_cols = []
    out_rows = []
    for P in range(KV_HEADS // 2):
        for g in range(GQA_GROUP):
            heads = ((2 * P) * GQA_GROUP + g, (2 * P + 1) * GQA_GROUP + g)
            q_cols.append(np.array([heads[w] * HEAD_DIM + dd for dd, w in zip(dim, which)]))
            out_rows.append(np.concatenate([heads[0] * HEAD_DIM + np.arange(HEAD_DIM),
                                            heads[1] * HEAD_DIM + np.arange(HEAD_DIM)]))
    k_cols = []
    for P in range(KV_HEADS // 2):
        heads = (2 * P, 2 * P + 1)
        k_cols.append(np.array([heads[w] * HEAD_DIM + dd for dd, w in zip(dim, which)]))
    return np.concatenate(q_cols), np.concatenate(k_cols), np.concatenate(out_rows)


def _rope_lane_tables(L):
    pos = jnp.arange(L)
    row = (pos // GRID_W).astype(F32)
    col = (pos % GRID_W).astype(F32)
    axis_dim = HEAD_DIM // 2
    inv_freq = ROPE_THETA ** (-jnp.arange(0, axis_dim, 2, dtype=F32) / axis_dim)
    ang = jnp.concatenate([row[:, None] * inv_freq, col[:, None] * inv_freq], -1)
    c = jnp.cos(ang)
    s = jnp.sin(ang)
    return jnp.concatenate([c, c, c, c], -1), jnp.concatenate([-s, -s, s, s], -1)


def _qk_prep_kernel(q_ref, k_ref, cos_ref, sin_ref, gq_ref, gk_ref, qo_ref, ko_ref):
    tm = q_ref.shape[0]
    lane = lax.broadcasted_iota(jnp.int32, (tm, LANES), 1)
    head_a = (lane % HEAD_DIM) < 32
    c = cos_ref[...]
    s = sin_ref[...]

    def norm_rope(x, g):
        x = x.astype(F32)
        x2 = x * x
        tot = jnp.sum(x2, -1, keepdims=True)
        sa = jnp.sum(jnp.where(head_a, x2, 0.0), -1, keepdims=True)
        inv = jnp.where(head_a, lax.rsqrt(sa * (1.0 / HEAD_DIM) + QK_NORM_EPS),
                        lax.rsqrt((tot - sa) * (1.0 / HEAD_DIM) + QK_NORM_EPS))
        xn = x * inv * g
        return xn * c + pltpu.roll(xn, HEAD_DIM, axis=1) * s

    for j in range(q_ref.shape[1] // LANES):
        sl = slice(j * LANES, (j + 1) * LANES)
        qo_ref[:, sl] = (norm_rope(q_ref[:, sl], gq_ref[...]) * ATT_SCALE).astype(qo_ref.dtype)
    for j in range(k_ref.shape[1] // LANES):
        sl = slice(j * LANES, (j + 1) * LANES)
        ko_ref[:, sl] = norm_rope(k_ref[:, sl], gk_ref[...]).astype(ko_ref.dtype)


def _qk_prep(proj, cos_t, sin_t, gq, gk, L, tm):
    N = proj.shape[0]
    nl = L // tm
    return pl.pallas_call(
        _qk_prep_kernel,
        grid=(N // tm,),
        in_specs=[pl.BlockSpec((tm, MIX_WIDTH), lambda i: (i, 0)),
                  pl.BlockSpec((tm, KV_WIDTH), lambda i: (i, MIX_WIDTH // KV_WIDTH)),
                  pl.BlockSpec((tm, LANES), lambda i: (i % nl, 0)),
                  pl.BlockSpec((tm, LANES), lambda i: (i % nl, 0)),
                  pl.BlockSpec((1, LANES), lambda i: (0, 0)),
                  pl.BlockSpec((1, LANES), lambda i: (0, 0))],
        out_specs=[pl.BlockSpec((tm, MIX_WIDTH), lambda i: (i, 0)),
                   pl.BlockSpec((tm, KV_WIDTH), lambda i: (i, 0))],
        out_shape=[jax.ShapeDtypeStruct((N, MIX_WIDTH), BF16),
                   jax.ShapeDtypeStruct((N, KV_WIDTH), BF16)],
        compiler_params=_cparams(("parallel",)),
        name="qk_prep",
    )(proj, proj, cos_t, sin_t, gq, gk)


GQA_KV_CHUNK = 1024


def _gqa_kernel(q_ref, k_ref, v_ref, o_ref, q6_ref, s_ref, m_ref, l_ref, acc_ref):
    tq = q_ref.shape[0]
    tk = s_ref.shape[2]
    nk = k_ref.shape[0] // tk
    lane = lax.broadcasted_iota(jnp.int32, (tq, LANES), 1)
    head_a = (lane % HEAD_DIM) < 32
    q = q_ref[...]
    for g in range(GQA_GROUP):
        qs = q[:, g * LANES:(g + 1) * LANES]
        zero = jnp.zeros_like(qs)
        q6_ref[(2 * g) * tq:(2 * g + 1) * tq, :] = jnp.where(head_a, qs, zero)
        q6_ref[(2 * g + 1) * tq:(2 * g + 2) * tq, :] = jnp.where(head_a, zero, qs)
    m_ref[...] = jnp.full(m_ref.shape, -jnp.inf, F32)
    l_ref[...] = jnp.zeros(l_ref.shape, F32)
    acc_ref[...] = jnp.zeros(acc_ref.shape, F32)

    def scores(j):
        return lax.dot_general(q6_ref[...], k_ref[j * tk:(j + 1) * tk, :], (((1,), (1,)), ((), ())),
                               preferred_element_type=F32)

    s_ref[0] = scores(0)
    for j in range(nk):
        if j + 1 < nk:
            s_ref[(j + 1) % 2] = scores(j + 1)
        s = s_ref[j % 2]
        m_old = m_ref[...]
        m_new = jnp.maximum(m_old, jnp.max(s, -1, keepdims=True))
        alpha = jnp.exp(m_old - m_new)
        p = jnp.exp(s - jnp.concatenate([m_new] * (tk // LANES), axis=1))
        l_ref[...] = alpha * l_ref[...] + jnp.sum(p, -1, keepdims=True)
        acc_ref[...] = alpha * acc_ref[...] + jnp.dot(p.astype(BF16), v_ref[j * tk:(j + 1) * tk, :],
                                                      preferred_element_type=F32)
        m_ref[...] = m_new
    o6 = acc_ref[...] / l_ref[...]
    for g in range(GQA_GROUP):
        oa = o6[(2 * g) * tq:(2 * g + 1) * tq]
        ob = o6[(2 * g + 1) * tq:(2 * g + 2) * tq]
        o_ref[:, g * LANES:(g + 1) * LANES] = jnp.where(lane < HEAD_DIM, oa, ob).astype(o_ref.dtype)


def _gqa_attention(q_rot, k_rot, proj, B, L, tq):
    slab_w = GQA_GROUP * LANES
    v_blk0 = (MIX_WIDTH + KV_WIDTH) // LANES
    rows6 = 2 * GQA_GROUP * tq
    tk = min(GQA_KV_CHUNK, L)
    return pl.pallas_call(
        _gqa_kernel,
        grid=(B, KV_HEADS // 2, L // tq),
        in_specs=[pl.BlockSpec((None, tq, slab_w), lambda b, p, i: (b, i, p)),
                  pl.BlockSpec((None, L, LANES), lambda b, p, i: (b, 0, p)),
                  pl.BlockSpec((None, L, LANES), lambda b, p, i: (b, 0, v_blk0 + p))],
        out_specs=pl.BlockSpec((None, tq, slab_w), lambda b, p, i: (b, i, p)),
        out_shape=jax.ShapeDtypeStruct((B, L, MIX_WIDTH), BF16),
        scratch_shapes=[pltpu.VMEM((rows6, LANES), BF16),
                        pltpu.VMEM((2, rows6, tk), F32),
                        pltpu.VMEM((rows6, LANES), F32),
                        pltpu.VMEM((rows6, LANES), F32),
                        pltpu.VMEM((rows6, LANES), F32)],
        compiler_params=_cparams(("parallel", "parallel", "arbitrary")),
        name="gqa_attention",
    )(q_rot, k_rot, proj)


def _hyena_features(L):
    pos = jnp.arange(L, dtype=F32)
    t = pos / max(L - 1, 1)
    w = 2.0 * math.pi * pos / L
    f = jnp.linspace(1e-4, HYENA_BANDS - 1, HYENA_BANDS, dtype=F32)
    wf = w[:, None] * f[None, :]
    z = jnp.concatenate([t[:, None], jnp.cos(wf), -jnp.sin(wf)], -1)
    return jnp.pad(z, ((0, 0), (0, LANES - HYENA_EMB))), t[:, None]


def _filter_kernel(z_ref, t_ref, w1_ref, b1_ref, w2_ref, b2_ref, w3_ref, b3_ref, w4f_ref, w4b_ref,
                   b4f_ref, b4b_ref, fr_ref, delta_ref, hsum_ref, hdiff_ref):
    hp = lax.Precision.HIGHEST
    fr = fr_ref[...]
    h = jnp.sin(fr * (jnp.dot(z_ref[...], w1_ref[...], precision=hp, preferred_element_type=F32) + b1_ref[...]))
    h = jnp.sin(fr * (jnp.dot(h, w2_ref[...], precision=hp, preferred_element_type=F32) + b2_ref[...]))
    h = jnp.sin(fr * (jnp.dot(h, w3_ref[...], precision=hp, preferred_element_type=F32) + b3_ref[...]))
    hf = jnp.dot(h, w4f_ref[...], precision=hp, preferred_element_type=F32) + b4f_ref[...]
    hb = jnp.dot(h, w4b_ref[...], precision=hp, preferred_element_type=F32) + b4b_ref[...]
    dec = jnp.exp(-t_ref[...] * delta_ref[...])
    hf = hf * dec
    hb = hb * dec
    norm = jnp.sum(jnp.abs(hf), 0, keepdims=True) + jnp.sum(jnp.abs(hb), 0, keepdims=True)
    inv = 1.0 / norm
    hf = hf * inv
    row = lax.broadcasted_iota(jnp.int32, hb.shape, 0)
    hb = jnp.where(row == 0, 0.0, hb * inv)
    hsum_ref[...] = hf + hb
    hdiff_ref[...] = hf - hb


def _hyena_filters(zf, t, w1, b1, w2, b2, w3, b3, w4, b4, freq, L, tc):
    OC = HYENA_ORDER * MIX_WIDTH
    max_decay = math.log(HYENA_DECAY_TARGET) / HYENA_FAST_DECAY
    min_decay = math.log(HYENA_DECAY_TARGET) / HYENA_SLOW_DECAY
    deltas = jnp.abs(jnp.linspace(min_decay, max_decay, MIX_WIDTH, dtype=F32))
    deltas = jnp.tile(deltas, HYENA_ORDER)[None, :]
    w1p = jnp.pad(w1.astype(F32), ((0, LANES - HYENA_EMB), (0, 0)))
    row = lambda a: a.astype(F32)[None, :]
    full = lambda shape: pl.BlockSpec(shape, lambda j: (0,) * len(shape))
    nb = OC // tc
    return pl.pallas_call(
        _filter_kernel,
        grid=(nb,),
        in_specs=[full((L, LANES)), full((L, 1)),
                  full((LANES, HYENA_FFN)), full((1, HYENA_FFN)),
                  full((HYENA_FFN, HYENA_FFN)), full((1, HYENA_FFN)),
                  full((HYENA_FFN, HYENA_FFN)), full((1, HYENA_FFN)),
                  pl.BlockSpec((HYENA_FFN, tc), lambda j: (0, j)),
                  pl.BlockSpec((HYENA_FFN, tc), lambda j: (0, nb + j)),
                  pl.BlockSpec((1, tc), lambda j: (0, j)),
                  pl.BlockSpec((1, tc), lambda j: (0, nb + j)),
                  full((1, HYENA_FFN)),
                  pl.BlockSpec((1, tc), lambda j: (0, j))],
        out_specs=[pl.BlockSpec((L, tc), lambda j: (0, j)),
                   pl.BlockSpec((L, tc), lambda j: (0, j))],
        out_shape=[jax.ShapeDtypeStruct((L, OC), F32), jax.ShapeDtypeStruct((L, OC), F32)],
        compiler_params=_cparams(("parallel",)),
        name="hyena_filters",
    )(zf, t, w1p, row(b1), w2.astype(F32), row(b2), w3.astype(F32), row(b3), w4.astype(F32), w4.astype(F32),
      row(b4), row(b4), row(freq), deltas)


def _dft_matrices(L):
    n = 2 * L
    k = jnp.arange(L, dtype=jnp.int32)
    ks = (k[:, None] * k[None, :]) % n
    ang = ks.astype(F32) * (2.0 * math.pi / n)
    fc = jnp.cos(ang)
    sn = jnp.sin(ang)
    alt = jnp.where(k % 2 == 0, 1.0, -1.0).astype(F32)
    fs = jnp.where(k[:, None] == 0, alt[None, :], sn)
    fst = jnp.where(k[None, :] == 0, alt[:, None], sn)
    return fc.astype(BF16), fs.astype(BF16), fst.astype(BF16)


def _short_conv_kernel(u_ref, w_ref, b_ref, o_ref):
    L = u_ref.shape[0]
    u = u_ref[...].astype(F32)
    row = lax.broadcasted_iota(jnp.int32, u.shape, 0)
    prev = jnp.where(row == 0, 0.0, pltpu.roll(u, 1, axis=0))
    nxt = jnp.where(row == L - 1, 0.0, pltpu.roll(u, L - 1, axis=0))
    w = w_ref[...]
    o_ref[...] = (prev * w[0:1] + u * w[1:2] + nxt * w[2:3] + b_ref[...]).astype(o_ref.dtype)


def _short_conv(proj, conv_w, conv_b, B, L, tc):
    return pl.pallas_call(
        _short_conv_kernel,
        grid=(B, HYENA_WIDTH // tc),
        in_specs=[pl.BlockSpec((None, L, tc), lambda b, j: (b, 0, j)),
                  pl.BlockSpec((3, tc), lambda b, j: (0, j)),
                  pl.BlockSpec((1, tc), lambda b, j: (0, j))],
        out_specs=pl.BlockSpec((None, L, tc), lambda b, j: (b, 0, j)),
        out_shape=jax.ShapeDtypeStruct((B, L, HYENA_WIDTH), BF16),
        compiler_params=_cparams(("parallel", "parallel")),
        name="short_conv",
    )(proj, conv_w.astype(F32), conv_b.astype(F32)[None, :])


def _hy_fwd_kernel(fc_ref, fs_ref, z_ref, pa_ref, qa_ref, qb_ref, sb_ref, ya_ref, yb_ref):
    z = z_ref[...]
    a = jnp.dot(fc_ref[...], z, preferred_element_type=F32)
    b = jnp.dot(fs_ref[...], z, preferred_element_type=F32)
    ya_ref[...] = (a * pa_ref[...] - b * qa_ref[...]).astype(ya_ref.dtype)
    yb_ref[...] = (a * qb_ref[...] + b * sb_ref[...]).astype(yb_ref.dtype)


def _hy_inv_kernel(fc_ref, fst_ref, ya_ref, yb_ref, z_ref, xg_ref, bias_ref, o_ref):
    y = jnp.dot(fc_ref[...], ya_ref[...], preferred_element_type=F32)
    y = y + jnp.dot(fst_ref[...], yb_ref[...], preferred_element_type=F32)
    z = z_ref[...].astype(F32)
    o_ref[...] = (xg_ref[...].astype(F32) * (y + bias_ref[...] * z)).astype(o_ref.dtype)


def _hyena_long_conv_gate(uc, z_arr, z_blk, xg_blk, fc, fs, fst, coef, bias, B, L, tm_f, tm_i):
    C = MIX_WIDTH
    pa, qa, qb, sb = coef
    ya, yb = pl.pallas_call(
        _hy_fwd_kernel,
        grid=(B, L // tm_f),
        in_specs=[pl.BlockSpec((tm_f, L), lambda b, i: (i, 0)),
                  pl.BlockSpec((tm_f, L), lambda b, i: (i, 0)),
                  pl.BlockSpec((None, L, C), lambda b, i: (b, 0, z_blk)),
                  pl.BlockSpec((tm_f, C), lambda b, i: (i, 0)),
                  pl.BlockSpec((tm_f, C), lambda b, i: (i, 0)),
                  pl.BlockSpec((tm_f, C), lambda b, i: (i, 0)),
                  pl.BlockSpec((tm_f, C), lambda b, i: (i, 0))],
        out_specs=[pl.BlockSpec((None, tm_f, C), lambda b, i: (b, i, 0)),
                   pl.BlockSpec((None, tm_f, C), lambda b, i: (b, i, 0))],
        out_shape=[jax.ShapeDtypeStruct((B, L, C), BF16), jax.ShapeDtypeStruct((B, L, C), BF16)],
        compiler_params=_cparams(("parallel", "arbitrary")),
        name="hyena_dft_fwd",
    )(fc, fs, z_arr, pa, qa, qb, sb)
    return pl.pallas_call(
        _hy_inv_kernel,
        grid=(B, L // tm_i),
        in_specs=[pl.BlockSpec((tm_i, L), lambda b, i: (i, 0)),
                  pl.BlockSpec((tm_i, L), lambda b, i: (i, 0)),
                  pl.BlockSpec((None, L, C), lambda b, i: (b, 0, 0)),
                  pl.BlockSpec((None, L, C), lambda b, i: (b, 0, 0)),
                  pl.BlockSpec((None, tm_i, C), lambda b, i: (b, i, z_blk)),
                  pl.BlockSpec((None, tm_i, C), lambda b, i: (b, i, xg_blk)),
                  pl.BlockSpec((1, C), lambda b, i: (0, 0))],
        out_specs=pl.BlockSpec((None, tm_i, C), lambda b, i: (b, i, 0)),
        out_shape=jax.ShapeDtypeStruct((B, L, C), BF16),
        compiler_params=_cparams(("parallel", "arbitrary")),
        name="hyena_dft_inv",
    )(fc, fst, ya, yb, z_arr, uc, bias)


def _hyena_mixer(proj, j, p, B, L):
    C = MIX_WIDTH
    OC = HYENA_ORDER * C
    n = 2 * L
    zf, t = _hyena_features(L)
    hsum, hdiff = _hyena_filters(zf, t, p['filt_w1'][j], p['filt_b1'][j], p['filt_w2'][j], p['filt_b2'][j],
                                 p['filt_w3'][j], p['filt_b3'][j], p['filt_w4'][j], p['filt_b4'][j],
                                 p['filt_freq'][j], L, 256)
    fc, fs, fst = _dft_matrices(L)
    parts = []
    for h in (hsum, hdiff):
        hi = h.astype(BF16)
        parts += [hi, (h - hi.astype(F32)).astype(BF16)]
    hcat = jnp.concatenate(parts, axis=1)
    spec = _matmul(jnp.concatenate([fc, fs], axis=0), hcat, F32, 512, 512)
    gc = spec[:L, 0:OC] + spec[:L, OC:2 * OC]
    nyq = spec[L:L + 1, 0:OC] + spec[L:L + 1, OC:2 * OC]
    gs = spec[L:, 2 * OC:3 * OC] + spec[L:, 3 * OC:4 * OC]
    k0 = (jnp.arange(L) == 0)[:, None]
    ca = jnp.where(k0, 1.0 / n, 2.0 / n).astype(F32)
    pa = ca * gc
    qa = ca * jnp.where(k0, 0.0, gs)
    qb = qa
    sb = ca * jnp.where(k0, nyq, gc)
    uc = _short_conv(proj, p['conv_w_c'][j], p['conv_b_c'][j], B, L, 256)
    bias = p['long_bias_c'][j].astype(F32)
    tm_f = min(512, L)
    tm_i = min(256, L)
    z = uc
    z_blk = 0
    for o in range(HYENA_ORDER):
        sl = slice(o * C, (o + 1) * C)
        coef = (pa[:, sl], qa[:, sl], qb[:, sl], sb[:, sl])
        z = _hyena_long_conv_gate(uc, z, z_blk, o + 1, fc, fs, fst, coef, bias[o][None, :], B, L, tm_f, tm_i)
        z_blk = 0
    return z


def _pack_halves(x):
    c = x.shape[1] // 2
    bits = lax.bitcast_convert_type(x.astype(BF16).astype(F32), jnp.uint32)
    return (bits[:, :c] & jnp.uint32(0xFFFF0000)) | (bits[:, c:] >> 16)


def _unpack_halves(w):
    hi = lax.bitcast_convert_type(w & jnp.uint32(0xFFFF0000), F32)
    lo = lax.bitcast_convert_type(w << 16, F32)
    return hi, lo


def _post_kernel(x_ref, mix_ref, qm_ref, km_ref, vm_ref, wo_ref, g_ref, b_ref, rw_ref, rb_ref,
                 xo_ref, xp_ref, gate_ref, route_ref, count_ref, base_ref):
    tm = x_ref.shape[0]
    lane = lax.broadcasted_iota(jnp.int32, (tm, LANES), 1)
    first = lane < HEAD_DIM
    outs = []
    for p in range(MEM_WIDTH // LANES):
        sl = slice(p * LANES, (p + 1) * LANES)
        qs = qm_ref[:, sl]
        zero = jnp.zeros_like(qs)
        q2 = jnp.concatenate([jnp.where(first, qs, zero), jnp.where(first, zero, qs)], axis=0)
        s = lax.dot_general(q2, km_ref[:, sl], (((1,), (1,)), ((), ())), preferred_element_type=F32) * ATT_SCALE
        m = jnp.max(s, -1, keepdims=True)
        e = jnp.exp(s - m)
        l = jnp.sum(e, -1, keepdims=True)
        o2 = jnp.dot(e.astype(BF16), vm_ref[:, sl], preferred_element_type=F32) / l
        outs.append(jnp.where(first, o2[:tm], o2[tm:]).astype(BF16))
    mem_out = jnp.concatenate(outs, axis=1)
    acc = jnp.dot(mix_ref[...], wo_ref[:MIX_WIDTH, :], preferred_element_type=F32)
    acc = acc + jnp.dot(mem_out, wo_ref[MIX_WIDTH:, :], preferred_element_type=F32)
    x1 = _layer_norm(DEEPNORM_ALPHA * x_ref[...] + acc, g_ref[...], b_ref[...])
    xo_ref[...] = x1
    xp_ref[...] = _pack_halves(x1)
    x_hi = x1.astype(BF16)
    x_lo = (x1 - x_hi.astype(F32)).astype(BF16)
    prod = jnp.dot(x_hi, rw_ref[...], preferred_element_type=F32)
    logits = (prod[:, :LANES] + prod[:, LANES:]
              + jnp.dot(x_lo, rw_ref[:, :LANES], preferred_element_type=F32) + rb_ref[...])
    lane_f = lane.astype(F32)
    work = logits
    chosen = jnp.zeros_like(logits)
    gates = jnp.zeros_like(logits)
    denom = jnp.zeros((tm, 1), F32)
    sels, idxs = [], []
    m0 = None
    for k in range(TOP_K):
        m = jnp.max(work, -1, keepdims=True)
        idx = jnp.min(jnp.where(work == m, lane_f, float(LANES)), -1, keepdims=True)
        sel = lane_f == idx
        if k == 0:
            m0 = m
        ek = jnp.exp(m - m0)
        gates = jnp.where(lane == k, ek, gates)
        chosen = chosen + jnp.where(sel, 1.0, 0.0)
        denom = denom + ek
        work = jnp.where(sel, -jnp.inf, work)
        sels.append(sel)
        idxs.append(idx)
    gate_ref[...] = gates / denom

    @pl.when(pl.program_id(0) == 0)
    def _():
        base_ref[...] = jnp.zeros_like(base_ref)

    r_i = lax.broadcasted_iota(jnp.int32, (tm, tm), 0)
    c_i = lax.broadcasted_iota(jnp.int32, (tm, tm), 1)
    before = jnp.where(r_i > c_i, 1.0, 0.0).astype(BF16)
    prefix = jnp.dot(before, chosen.astype(BF16), preferred_element_type=F32) + base_ref[...]
    route = jnp.zeros((tm, LANES), jnp.int32)
    for k in range(TOP_K):
        rank = jnp.sum(jnp.where(sels[k], prefix, 0.0), -1, keepdims=True)
        route = jnp.where(lane == k, rank.astype(jnp.int32), route)
        route = jnp.where(lane == TOP_K + k, idxs[k].astype(jnp.int32), route)
    route_ref[...] = route
    base_ref[...] += jnp.sum(chosen, 0, keepdims=True)
    count_ref[...] = base_ref[...]


def _post_mixer(x, mix, proj, qm_blk, kmem, vmem, wo, g, b, rw, rb, L, tm):
    N, D = x.shape
    nl = L // tm
    CAT = MIX_WIDTH + MEM_WIDTH
    return pl.pallas_call(
        _post_kernel,
        grid=(N // tm,),
        in_specs=[pl.BlockSpec((tm, D), lambda i: (i, 0)),
                  pl.BlockSpec((tm, MIX_WIDTH), lambda i: (i, 0)),
                  pl.BlockSpec((tm, MEM_WIDTH), lambda i: (i, qm_blk)),
                  pl.BlockSpec((None, kmem.shape[1], MEM_WIDTH), lambda i: (i // nl, 0, 0)),
                  pl.BlockSpec((None, vmem.shape[1], MEM_WIDTH), lambda i: (i // nl, 0, 0)),
                  pl.BlockSpec((CAT, D), lambda i: (0, 0)),
                  pl.BlockSpec((1, D), lambda i: (0, 0)),
                  pl.BlockSpec((1, D), lambda i: (0, 0)),
                  pl.BlockSpec((D, 2 * LANES), lambda i: (0, 0)),
                  pl.BlockSpec((1, LANES), lambda i: (0, 0))],
        out_specs=[pl.BlockSpec((tm, D), lambda i: (i, 0)),
                   pl.BlockSpec((tm, D // 2), lambda i: (i, 0)),
                   pl.BlockSpec((tm, LANES), lambda i: (i, 0)),
                   pl.BlockSpec((tm, LANES), lambda i: (i, 0)),
                   pl.BlockSpec((1, LANES), lambda i: (0, 0))],
        out_shape=[jax.ShapeDtypeStruct((N, D), F32), jax.ShapeDtypeStruct((N, D // 2), jnp.uint32),
                   jax.ShapeDtypeStruct((N, LANES), F32), jax.ShapeDtypeStruct((N, LANES), jnp.int32),
                   jax.ShapeDtypeStruct((1, LANES), F32)],
        scratch_shapes=[pltpu.VMEM((1, LANES), F32)],
        compiler_params=_cparams(("arbitrary",)),
        name="post_mixer",
    )(x, mix, proj, kmem, vmem, wo, g, b, rw, rb)


EXPERT_TILE = 512


def _route_tables(route, counts, n_tok_tiles, max_tiles):
    cnt = counts[0, :N_EXPERTS].astype(jnp.int32)
    padded = ((cnt + EXPERT_TILE - 1) // EXPERT_TILE) * EXPERT_TILE
    ends = jnp.cumsum(padded)
    offs = ends - padded
    rank = route[:, :TOP_K]
    eid = route[:, TOP_K:2 * TOP_K]
    pos = (jnp.take(offs, eid) + rank).reshape(-1)
    tile_ends = ends // EXPERT_TILE
    n_tiles = tile_ends[-1]
    t = jnp.arange(max_tiles, dtype=jnp.int32)
    t_live = jnp.minimum(t, n_tiles - 1)
    tile_expert = jnp.sum((t_live[:, None] >= tile_ends[None, :]).astype(jnp.int32), axis=1)
    tile_expert = jnp.minimum(tile_expert, N_EXPERTS - 1)
    zero_start = (ends - EXPERT_TILE).astype(jnp.int32)
    zero_valid = (padded > 0).astype(jnp.int32)
    return pos, tile_expert, n_tiles.reshape(1).astype(jnp.int32), zero_start, zero_valid


def _index_copy(pos_hbm, idx_smem, idx_sem, step, slot, n_idx):
    src = pos_hbm.at[pl.ds(pl.multiple_of(step * n_idx, n_idx), n_idx)]
    dst = idx_smem.at[pl.ds(pl.multiple_of(slot * n_idx, n_idx), n_idx)]
    return pltpu.make_async_copy(src, dst, idx_sem.at[slot])


def _dispatch_kernel(zstart_ref, zvalid_ref, xp_ref, pos_hbm, xs_hbm, idx_smem, zero_ref, idx_sem, row_sem, zero_sem):
    i = pl.program_id(0)
    tm = xp_ref.shape[0]
    n_idx = tm * TOP_K
    slot = i % 2

    @pl.when(i == 0)
    def _():
        _index_copy(pos_hbm, idx_smem, idx_sem, 0, 0, n_idx).start()
        zero_ref[...] = jnp.zeros_like(zero_ref)

        def zero_copy(e):
            start = pl.multiple_of(zstart_ref[e], EXPERT_TILE)
            return pltpu.make_async_copy(zero_ref, xs_hbm.at[pl.ds(start, EXPERT_TILE)], zero_sem)

        for e in range(N_EXPERTS):
            @pl.when(zvalid_ref[e] > 0)
            def _():
                zero_copy(e).start()
        for e in range(N_EXPERTS):
            @pl.when(zvalid_ref[e] > 0)
            def _():
                zero_copy(e).wait()

    @pl.when(i + 1 < pl.num_programs(0))
    def _():
        _index_copy(pos_hbm, idx_smem, idx_sem, i + 1, 1 - slot, n_idx).start()

    _index_copy(pos_hbm, idx_smem, idx_sem, i, slot, n_idx).wait()
    base = slot * n_idx

    def issue(n, carry):
        for k in range(TOP_K):
            p = idx_smem[base + n * TOP_K + k]
            pltpu.make_async_copy(xp_ref.at[pl.ds(n, 1)], xs_hbm.at[pl.ds(p, 1)], row_sem).start()
        return carry

    lax.fori_loop(0, tm, issue, 0, unroll=8)
    for k in range(TOP_K):
        pltpu.make_async_copy(xp_ref, xs_hbm.at[pl.ds(0, tm)], row_sem).wait()


def _dispatch(xp, pos, zero_start, zero_valid, rows, tm):
    N, W = xp.shape
    return pl.pallas_call(
        _dispatch_kernel,
        grid_spec=pltpu.PrefetchScalarGridSpec(
            num_scalar_prefetch=2,
            grid=(N // tm,),
            in_specs=[pl.BlockSpec((tm, W), lambda i, zs, zv: (i, 0)),
                      pl.BlockSpec(memory_space=pl.ANY)],
            out_specs=pl.BlockSpec(memory_space=pl.ANY),
            scratch_shapes=[pltpu.SMEM((2 * tm * TOP_K,), jnp.int32),
                            pltpu.VMEM((EXPERT_TILE, W), jnp.uint32),
                            pltpu.SemaphoreType.DMA((2,)),
                            pltpu.SemaphoreType.DMA,
                            pltpu.SemaphoreType.DMA]),
        out_shape=jax.ShapeDtypeStruct((rows, W), jnp.uint32),
        compiler_params=_cparams(("arbitrary",)),
        name="moe_dispatch",
    )(zero_start, zero_valid, xp, pos)


def _expert_kernel(te_ref, nt_ref, xs_ref, w1_ref, b1_ref, w2_ref, b2_ref, o_ref, w1b_ref, w2b_ref):
    t = pl.program_id(0)

    @pl.when((t == 0) | (te_ref[t] != te_ref[jnp.maximum(t - 1, 0)]))
    def _():
        w1b_ref[...] = w1_ref[...].astype(BF16)
        w2b_ref[...] = w2_ref[...].astype(BF16)

    @pl.when(t < nt_ref[0])
    def _():
        hi, lo = _unpack_halves(xs_ref[...])
        half = hi.shape[1]
        h = jnp.dot(hi.astype(BF16), w1b_ref[:half, :], preferred_element_type=F32)
        h = h + jnp.dot(lo.astype(BF16), w1b_ref[half:, :], preferred_element_type=F32) + b1_ref[...]
        gl = jnp.minimum(h[:, :D_EXPERT], SWIGLU_LIMIT)
        ul = jnp.clip(h[:, D_EXPERT:], -SWIGLU_LIMIT, SWIGLU_LIMIT)
        a = (ul + 1.0) * (gl * jax.nn.sigmoid(SWIGLU_ALPHA * gl))
        o = jnp.dot(a.astype(BF16), w2b_ref[...], preferred_element_type=F32) + b2_ref[...]
        o_ref[...] = _pack_halves(o)


def _experts(xs, tile_expert, n_tiles, w1, b1, w2, b2, layer):
    rows, W = xs.shape
    D = 2 * W
    last = lambda t, te, nt: jnp.minimum(t, nt[0] - 1)
    return pl.pallas_call(
        _expert_kernel,
        grid_spec=pltpu.PrefetchScalarGridSpec(
            num_scalar_prefetch=2,
            grid=(rows // EXPERT_TILE,),
            in_specs=[pl.BlockSpec((EXPERT_TILE, W), lambda t, te, nt: (last(t, te, nt), 0)),
                      pl.BlockSpec((None, None, D, 2 * D_EXPERT), lambda t, te, nt: (layer, te[t], 0, 0)),
                      pl.BlockSpec((None, None, 1, 2 * D_EXPERT), lambda t, te, nt: (layer, te[t], 0, 0)),
                      pl.BlockSpec((None, None, D_EXPERT, D), lambda t, te, nt: (layer, te[t], 0, 0)),
                      pl.BlockSpec((None, None, 1, D), lambda t, te, nt: (layer, te[t], 0, 0))],
            out_specs=pl.BlockSpec((EXPERT_TILE, W), lambda t, te, nt: (last(t, te, nt), 0)),
            scratch_shapes=[pltpu.VMEM((D, 2 * D_EXPERT), BF16), pltpu.VMEM((D_EXPERT, D), BF16)]),
        out_shape=jax.ShapeDtypeStruct((rows, W), jnp.uint32),
        compiler_params=_cparams(("arbitrary",)),
        name="moe_experts",
    )(tile_expert, n_tiles, xs, w1, b1, w2, b2)


def _combine_kernel(x_ref, gate_ref, g_ref, b_ref, pos_hbm, eo_hbm, o_ref, idx_smem, buf_ref, idx_sem, row_sem):
    i = pl.program_id(0)
    tm = x_ref.shape[0]
    n_idx = tm * TOP_K
    slot = i % 2

    @pl.when(i == 0)
    def _():
        _index_copy(pos_hbm, idx_smem, idx_sem, 0, 0, n_idx).start()

    @pl.when(i + 1 < pl.num_programs(0))
    def _():
        _index_copy(pos_hbm, idx_smem, idx_sem, i + 1, 1 - slot, n_idx).start()

    _index_copy(pos_hbm, idx_smem, idx_sem, i, slot, n_idx).wait()
    base = slot * n_idx

    def issue(n, carry):
        for k in range(TOP_K):
            p = idx_smem[base + n * TOP_K + k]
            pltpu.make_async_copy(eo_hbm.at[pl.ds(p, 1)], buf_ref.at[k, pl.ds(n, 1)], row_sem).start()
        return carry

    lax.fori_loop(0, tm, issue, 0, unroll=8)
    for k in range(TOP_K):
        pltpu.make_async_copy(eo_hbm.at[pl.ds(0, tm)], buf_ref.at[k], row_sem).wait()

    gates = gate_ref[...]
    lane = lax.broadcasted_iota(jnp.int32, gates.shape, 1)
    ya = yb = None
    for k in range(TOP_K):
        gk = jnp.sum(jnp.where(lane == k, gates, 0.0), -1, keepdims=True)
        hi, lo = _unpack_halves(buf_ref[k])
        ya = gk * hi if ya is None else ya + gk * hi
        yb = gk * lo if yb is None else yb + gk * lo
    y = jnp.concatenate([ya, yb], axis=1)
    o_ref[...] = _layer_norm(DEEPNORM_ALPHA * x_ref[...] + y, g_ref[...], b_ref[...])


def _combine(x, gates, g, b, pos, eo, tm):
    N, D = x.shape
    W = eo.shape[1]
    return pl.pallas_call(
        _combine_kernel,
        grid=(N // tm,),
        in_specs=[pl.BlockSpec((tm, D), lambda i: (i, 0)),
                  pl.BlockSpec((tm, LANES), lambda i: (i, 0)),
                  pl.BlockSpec((1, D), lambda i: (0, 0)),
                  pl.BlockSpec((1, D), lambda i: (0, 0)),
                  pl.BlockSpec(memory_space=pl.ANY),
                  pl.BlockSpec(memory_space=pl.ANY)],
        out_specs=pl.BlockSpec((tm, D), lambda i: (i, 0)),
        out_shape=jax.ShapeDtypeStruct((N, D), F32),
        scratch_shapes=[pltpu.SMEM((2 * tm * TOP_K,), jnp.int32),
                        pltpu.VMEM((TOP_K, tm, W), jnp.uint32),
                        pltpu.SemaphoreType.DMA((2,)),
                        pltpu.SemaphoreType.DMA],
        compiler_params=_cparams(("arbitrary",)),
        name="moe_combine",
    )(x, gates, g, b, pos, eo)


def _moe(x1, x1p, gates, route, counts, w1, b1, w2, b2, g, b, layer, tm):
    N = x1.shape[0]
    max_tiles = (N * TOP_K) // EXPERT_TILE + N_EXPERTS
    pos, tile_expert, n_tiles, zero_start, zero_valid = _route_tables(route, counts, N // tm, max_tiles)
    xs = _dispatch(x1p, pos, zero_start, zero_valid, max_tiles * EXPERT_TILE, tm)
    eo = _experts(xs, tile_expert, n_tiles, w1, b1, w2, b2, layer)
    return _combine(x1, gates, g, b, pos, eo, tm)


def kernel(x, mem, w_in_a, rpb_a, w_in_b, q_norm_b, k_norm_b, w_in_c, conv_w_c, conv_b_c, filt_w1, filt_b1, filt_w2, filt_b2, filt_w3, filt_b3, filt_w4, filt_b4, filt_freq, long_bias_c, w_mem_kv, w_out, ln1_g, ln1_b, ln2_g, ln2_b, router_w, router_b, moe_w1, moe_b1, moe_w2, moe_b2):
    B, L, D = x.shape
    N = B * L
    M = mem.shape[1]
    tm = min(512, L)
    hy = dict(filt_w1=filt_w1, filt_b1=filt_b1, filt_w2=filt_w2, filt_b2=filt_b2, filt_w3=filt_w3,
              filt_b3=filt_b3, filt_w4=filt_w4, filt_b4=filt_b4, filt_freq=filt_freq,
              conv_w_c=conv_w_c, conv_b_c=conv_b_c, long_bias_c=long_bias_c)

    n_layers = w_mem_kv.shape[0]
    wkv = jnp.transpose(w_mem_kv, (1, 0, 2)).reshape(D, n_layers * 2 * MEM_WIDTH).astype(BF16)
    kv_all = _matmul(mem.reshape(B * M, D), wkv, BF16, 512, 512).reshape(B, M, n_layers, 2, MEM_WIDTH)

    w1_all = moe_w1.astype(F32)
    w2_all = moe_w2.astype(F32)
    b1_all = moe_b1.astype(F32)[:, :, None, :]
    b2_all = moe_b2.astype(F32)[:, :, None, :]

    q_cols, k_cols, out_rows_b = _gqa_column_perms()

    xf = x.reshape(N, D).astype(F32)
    for i in range(DEPTH):
        kind = i % N_MIXERS
        j = i // N_MIXERS
        wo = w_out[i]
        if kind == 0:
            proj = _matmul(xf, w_in_a[j].astype(BF16), BF16, tm, 3 * MIX_WIDTH + MEM_WIDTH)
            bias_tab = _na_bias_table(rpb_a[j], L // GRID_W)
            mix = _na_attention(proj.reshape(B, L, -1), bias_tab, B, L).reshape(N, MIX_WIDTH)
            qm_blk = (3 * MIX_WIDTH) // MEM_WIDTH
        elif kind == 1:
            w = w_in_b[j]
            w = jnp.concatenate([w[:, :MIX_WIDTH][:, q_cols],
                                 w[:, MIX_WIDTH:MIX_WIDTH + KV_WIDTH][:, k_cols],
                                 w[:, MIX_WIDTH + KV_WIDTH:]], axis=1)
            proj = _matmul(xf, w.astype(BF16), BF16, tm, w.shape[1])
            cos_t, sin_t = _rope_lane_tables(L)
            dim, _ = _pair_lane_dims()
            gq = q_norm_b[j].astype(F32)[dim][None, :]
            gk = k_norm_b[j].astype(F32)[dim][None, :]
            q_rot, k_rot = _qk_prep(proj, cos_t, sin_t, gq, gk, L, tm)
            mix = _gqa_attention(q_rot.reshape(B, L, -1), k_rot.reshape(B, L, -1), proj.reshape(B, L, -1),
                                 B, L, min(128, L)).reshape(N, MIX_WIDTH)
            wo = jnp.concatenate([wo[:MIX_WIDTH][out_rows_b], wo[MIX_WIDTH:]], axis=0)
            qm_blk = (MIX_WIDTH + 2 * KV_WIDTH) // MEM_WIDTH
        else:
            proj = _matmul(xf, w_in_c[j].astype(BF16), BF16, tm, HYENA_WIDTH + MEM_WIDTH)
            mix = _hyena_mixer(proj.reshape(B, L, -1), j, hy, B, L).reshape(N, MIX_WIDTH)
            qm_blk = HYENA_WIDTH // MEM_WIDTH
        kmem = kv_all[:, :, i, 0, :]
        vmem = kv_all[:, :, i, 1, :]
        rw = jnp.pad(router_w[i].astype(F32), ((0, 0), (0, LANES - N_EXPERTS)))
        rw_hi = rw.astype(BF16)
        rw = jnp.concatenate([rw_hi, (rw - rw_hi.astype(F32)).astype(BF16)], axis=1)
        rb = jnp.pad(router_b[i].astype(F32), (0, LANES - N_EXPERTS), constant_values=NEG_BIG)[None, :]
        x1, x1p, gates, route, counts = _post_mixer(
            xf, mix, proj, qm_blk, kmem, vmem, wo.astype(BF16),
            ln1_g[i].astype(F32)[None, :], ln1_b[i].astype(F32)[None, :], rw, rb, L, tm)
        xf = _moe(x1, x1p, gates, route, counts, w1_all, b1_all, w2_all, b2_all,
                  ln2_g[i].astype(F32)[None, :], ln2_b[i].astype(F32)[None, :], i, tm)
    return xf.reshape(B, L, D).astype(x.dtype)
```

```python
import functools
import math

import numpy as np
import jax
import jax.numpy as jnp
from jax import lax
from jax.experimental import pallas as pl
from jax.experimental.pallas import tpu as pltpu

F32 = jnp.float32
BF16 = jnp.bfloat16

D_MODEL = 1024
DEPTH = 4
GRID_W = 64
N_MIXERS = 3
HEAD_DIM = 64
MIX_WIDTH = 768
MIX_HEADS = MIX_WIDTH // HEAD_DIM
MEM_HEADS = 4
MEM_WIDTH = MEM_HEADS * HEAD_DIM
NA_KH = 8
NA_KW = 16
NA_ROW_UNROLL = 8
KV_HEADS = 4
KV_WIDTH = KV_HEADS * HEAD_DIM
GQA_GROUP = MIX_HEADS // KV_HEADS
ROPE_THETA = 10000.0
QK_NORM_EPS = 1e-6
HYENA_ORDER = 2
HYENA_BANDS = 16
HYENA_EMB = 2 * HYENA_BANDS + 1
HYENA_FFN = 64
HYENA_WIDTH = (HYENA_ORDER + 1) * MIX_WIDTH
HYENA_DECAY_TARGET = 1e-2
HYENA_FAST_DECAY = 0.3
HYENA_SLOW_DECAY = 1.5
N_EXPERTS = 32
TOP_K = 4
D_EXPERT = D_MODEL
SWIGLU_LIMIT = 7.0
SWIGLU_ALPHA = 1.702
DEEPNORM_ALPHA = (2 * DEPTH) ** 0.25
LN_EPS = 1e-5

LANES = 128
SUBLANES = 8
VMEM_LIMIT = 56 * 1024 * 1024
NEG_BIG = -1e30
ATT_SCALE = HEAD_DIM ** -0.5


def _cparams(sem):
    return pltpu.CompilerParams(dimension_semantics=sem, vmem_limit_bytes=VMEM_LIMIT)


def _layer_norm(x, g, b):
    mu = jnp.mean(x, -1, keepdims=True)
    xc = x - mu
    var = jnp.mean(xc * xc, -1, keepdims=True)
    return xc * lax.rsqrt(var + LN_EPS) * g + b


def _mm_kernel(x_ref, w_ref, o_ref):
    o_ref[...] = jnp.dot(x_ref[...].astype(BF16), w_ref[...],
                         preferred_element_type=F32).astype(o_ref.dtype)


def _matmul(x, w, out_dtype, tm, tn):
    M, K = x.shape
    N = w.shape[1]
    tm = min(tm, M)
    tn = min(tn, N)
    assert M % tm == 0 and N % tn == 0
    return pl.pallas_call(
        _mm_kernel,
        grid=(M // tm, N // tn),
        in_specs=[pl.BlockSpec((tm, K), lambda i, j: (i, 0)),
                  pl.BlockSpec((K, tn), lambda i, j: (0, j))],
        out_specs=pl.BlockSpec((tm, tn), lambda i, j: (i, j)),
        out_shape=jax.ShapeDtypeStruct((M, N), out_dtype),
        compiler_params=_cparams(("parallel", "arbitrary")),
        name="matmul",
    )(x, w)


def _na_bias_table(rpb, rows):
    del rows
    col = np.arange(GRID_W)
    c0 = np.clip(col - NA_KW // 2, 0, GRID_W - NA_KW)
    cc = np.arange(GRID_W)
    valid = (cc[None, :] >= c0[:, None]) & (cc[None, :] < c0[:, None] + NA_KW)
    coff = np.clip(cc[None, :] - col[:, None] + NA_KW - 1, 0, 2 * NA_KW - 2)
    d = np.arange(NA_KH)
    i = np.arange(NA_KH)
    roff = i[None, :] - d[:, None] + NA_KH - 1
    t = rpb.astype(F32)[:, roff]
    t = t[:, :, :, coff]
    t = t.transpose(0, 1, 3, 2, 4)
    t = jnp.where(valid[None, None, :, None, :], t, NEG_BIG)
    return t.reshape(rpb.shape[0], NA_KH, GRID_W, NA_KH * GRID_W)


def _na_kernel(q_ref, k_ref, v_ref, bias_ref, o_ref, *, rows):
    lane = lax.broadcasted_iota(jnp.int32, (GRID_W, LANES), 1)
    first = lane < HEAD_DIM
    band = NA_KH * GRID_W

    def body(r, carry):
        r0 = jnp.clip(r - NA_KH // 2, 0, rows - NA_KH)
        d = r - r0
        qoff = pl.multiple_of(r * GRID_W, GRID_W)
        koff = pl.multiple_of(r0 * GRID_W, GRID_W)
        q = q_ref[pl.ds(qoff, GRID_W), :]
        kb = k_ref[pl.ds(koff, band), :]
        vb = v_ref[pl.ds(koff, band), :]
        zero = jnp.zeros_like(q)
        q2 = jnp.concatenate([jnp.where(first, q, zero), jnp.where(first, zero, q)], axis=0)
        s = lax.dot_general(q2, kb, (((1,), (1,)), ((), ())), preferred_element_type=F32)
        bias = jnp.concatenate([bias_ref[0, d], bias_ref[1, d]], axis=0)
        s = s * ATT_SCALE + bias
        m = jnp.max(s, -1, keepdims=True)
        e = jnp.exp(s - m)
        l = jnp.sum(e, -1, keepdims=True)
        o2 = jnp.dot(e.astype(BF16), vb, preferred_element_type=F32) / l
        o = jnp.where(first, o2[:GRID_W], o2[GRID_W:])
        o_ref[pl.ds(qoff, GRID_W), :] = o.astype(o_ref.dtype)
        return carry

    lax.fori_loop(0, rows, body, 0, unroll=NA_ROW_UNROLL)


def _na_attention(proj, bias_tab, B, L):
    rows = L // GRID_W
    assert rows >= NA_KH
    npair = MIX_WIDTH // LANES
    return pl.pallas_call(
        functools.partial(_na_kernel, rows=rows),
        grid=(npair, B),
        in_specs=[pl.BlockSpec((None, L, LANES), lambda p, b: (b, 0, p)),
                  pl.BlockSpec((None, L, LANES), lambda p, b: (b, 0, npair + p)),
                  pl.BlockSpec((None, L, LANES), lambda p, b: (b, 0, 2 * npair + p)),
                  pl.BlockSpec((2, NA_KH, GRID_W, NA_KH * GRID_W), lambda p, b: (p, 0, 0, 0))],
        out_specs=pl.BlockSpec((None, L, LANES), lambda p, b: (b, 0, p)),
        out_shape=jax.ShapeDtypeStruct((B, L, MIX_WIDTH), BF16),
        compiler_params=_cparams(("parallel", "parallel")),
        name="na_attention",
    )(proj, proj, proj, bias_tab)


def _pair_lane_dims():
    l = np.arange(LANES)
    return 2 * (l % 32) + (l >= HEAD_DIM).astype(np.int64), ((l % HEAD_DIM) >= 32).astype(np.int64)


def _gqa_column_perms():
    dim, which = _pair_lane_dims()
    q_cols = []
    out_rows = []
    for P in range(KV_HEADS // 2):
        for g in range(GQA_GROUP):
            heads = ((2 * P) * GQA_GROUP + g, (2 * P + 1) * GQA_GROUP + g)
            q_cols.append(np.array([heads[w] * HEAD_DIM + dd for dd, w in zip(dim, which)]))
            out_rows.append(np.concatenate([heads[0] * HEAD_DIM + np.arange(HEAD_DIM),
                                            heads[1] * HEAD_DIM + np.arange(HEAD_DIM)]))
    k_cols = []
    for P in range(KV_HEADS // 2):
        heads = (2 * P, 2 * P + 1)
        k_cols.append(np.array([heads[w] * HEAD_DIM + dd for dd, w in zip(dim, which)]))
    return np.concatenate(q_cols), np.concatenate(k_cols), np.concatenate(out_rows)


def _rope_lane_tables(L):
    pos = jnp.arange(L)
    row = (pos // GRID_W).astype(F32)
    col = (pos % GRID_W).astype(F32)
    axis_dim = HEAD_DIM // 2
    inv_freq = ROPE_THETA ** (-jnp.arange(0, axis_dim, 2, dtype=F32) / axis_dim)
    ang = jnp.concatenate([row[:, None] * inv_freq, col[:, None] * inv_freq], -1)
    c = jnp.cos(ang)
    s = jnp.sin(ang)
    return jnp.concatenate([c, c, c, c], -1), jnp.concatenate([-s, -s, s, s], -1)


def _qk_prep_kernel(q_ref, k_ref, cos_ref, sin_ref, gq_ref, gk_ref, qo_ref, ko_ref):
    tm = q_ref.shape[0]
    lane = lax.broadcasted_iota(jnp.int32, (tm, LANES), 1)
    head_a = (lane % HEAD_DIM) < 32
    c = cos_ref[...]
    s = sin_ref[...]

    def norm_rope(x, g):
        x = x.astype(F32)
        x2 = x * x
        tot = jnp.sum(x2, -1, keepdims=True)
        sa = jnp.sum(jnp.where(head_a, x2, 0.0), -1, keepdims=True)
        inv = jnp.where(head_a, lax.rsqrt(sa * (1.0 / HEAD_DIM) + QK_NORM_EPS),
                        lax.rsqrt((tot - sa) * (1.0 / HEAD_DIM) + QK_NORM_EPS))
        xn = x * inv * g
        return xn * c + pltpu.roll(xn, HEAD_DIM, axis=1) * s

    for j in range(q_ref.shape[1] // LANES):
        sl = slice(j * LANES, (j + 1) * LANES)
        qo_ref[:, sl] = (norm_rope(q_ref[:, sl], gq_ref[...]) * ATT_SCALE).astype(qo_ref.dtype)
    for j in range(k_ref.shape[1] // LANES):
        sl = slice(j * LANES, (j + 1) * LANES)
        ko_ref[:, sl] = norm_rope(k_ref[:, sl], gk_ref[...]).astype(ko_ref.dtype)


def _qk_prep(proj, cos_t, sin_t, gq, gk, L, tm):
    N = proj.shape[0]
    nl = L // tm
    return pl.pallas_call(
        _qk_prep_kernel,
        grid=(N // tm,),
        in_specs=[pl.BlockSpec((tm, MIX_WIDTH), lambda i: (i, 0)),
                  pl.BlockSpec((tm, KV_WIDTH), lambda i: (i, MIX_WIDTH // KV_WIDTH)),
                  pl.BlockSpec((tm, LANES), lambda i: (i % nl, 0)),
                  pl.BlockSpec((tm, LANES), lambda i: (i % nl, 0)),
                  pl.BlockSpec((1, LANES), lambda i: (0, 0)),
                  pl.BlockSpec((1, LANES), lambda i: (0, 0))],
        out_specs=[pl.BlockSpec((tm, MIX_WIDTH), lambda i: (i, 0)),
                   pl.BlockSpec((tm, KV_WIDTH), lambda i: (i, 0))],
        out_shape=[jax.ShapeDtypeStruct((N, MIX_WIDTH), BF16),
                   jax.ShapeDtypeStruct((N, KV_WIDTH), BF16)],
        compiler_params=_cparams(("parallel",)),
        name="qk_prep",
    )(proj, proj, cos_t, sin_t, gq, gk)


GQA_KV_CHUNK = 1024


def _gqa_kernel(q_ref, k_ref, v_ref, o_ref, q6_ref, s_ref, m_ref, l_ref, acc_ref):
    tq = q_ref.shape[0]
    tk = s_ref.shape[2]
    nk = k_ref.shape[0] // tk
    lane = lax.broadcasted_iota(jnp.int32, (tq, LANES), 1)
    head_a = (lane % HEAD_DIM) < 32
    q = q_ref[...]
    for g in range(GQA_GROUP):
        qs = q[:, g * LANES:(g + 1) * LANES]
        zero = jnp.zeros_like(qs)
        q6_ref[(2 * g) * tq:(2 * g + 1) * tq, :] = jnp.where(head_a, qs, zero)
        q6_ref[(2 * g + 1) * tq:(2 * g + 2) * tq, :] = jnp.where(head_a, zero, qs)
    m_ref[...] = jnp.full(m_ref.shape, -jnp.inf, F32)
    l_ref[...] = jnp.zeros(l_ref.shape, F32)
    acc_ref[...] = jnp.zeros(acc_ref.shape, F32)

    def scores(j):
        return lax.dot_general(q6_ref[...], k_ref[j * tk:(j + 1) * tk, :], (((1,), (1,)), ((), ())),
                               preferred_element_type=F32)

    s_ref[0] = scores(0)
    for j in range(nk):
        if j + 1 < nk:
            s_ref[(j + 1) % 2] = scores(j + 1)
        s = s_ref[j % 2]
        m_old = m_ref[...]
        m_new = jnp.maximum(m_old, jnp.max(s, -1, keepdims=True))
        alpha = jnp.exp(m_old - m_new)
        p = jnp.exp(s - jnp.concatenate([m_new] * (tk // LANES), axis=1))
        l_ref[...] = alpha * l_ref[...] + jnp.sum(p, -1, keepdims=True)
        acc_ref[...] = alpha * acc_ref[...] + jnp.dot(p.astype(BF16), v_ref[j * tk:(j + 1) * tk, :],
                                                      preferred_element_type=F32)
        m_ref[...] = m_new
    o6 = acc_ref[...] / l_ref[...]
    for g in range(GQA_GROUP):
        oa = o6[(2 * g) * tq:(2 * g + 1) * tq]
        ob = o6[(2 * g + 1) * tq:(2 * g + 2) * tq]
        o_ref[:, g * LANES:(g + 1) * LANES] = jnp.where(lane < HEAD_DIM, oa, ob).astype(o_ref.dtype)


def _gqa_attention(q_rot, k_rot, proj, B, L, tq):
    slab_w = GQA_GROUP * LANES
    v_blk0 = (MIX_WIDTH + KV_WIDTH) // LANES
    rows6 = 2 * GQA_GROUP * tq
    tk = min(GQA_KV_CHUNK, L)
    return pl.pallas_call(
        _gqa_kernel,
        grid=(B, KV_HEADS // 2, L // tq),
        in_specs=[pl.BlockSpec((None, tq, slab_w), lambda b, p, i: (b, i, p)),
                  pl.BlockSpec((None, L, LANES), lambda b, p, i: (b, 0, p)),
                  pl.BlockSpec((None, L, LANES), lambda b, p, i: (b, 0, v_blk0 + p))],
        out_specs=pl.BlockSpec((None, tq, slab_w), lambda b, p, i: (b, i, p)),
        out_shape=jax.ShapeDtypeStruct((B, L, MIX_WIDTH), BF16),
        scratch_shapes=[pltpu.VMEM((rows6, LANES), BF16),
                        pltpu.VMEM((2, rows6, tk), F32),
                        pltpu.VMEM((rows6, LANES), F32),
                        pltpu.VMEM((rows6, LANES), F32),
                        pltpu.VMEM((rows6, LANES), F32)],
        compiler_params=_cparams(("parallel", "parallel", "arbitrary")),
        name="gqa_attention",
    )(q_rot, k_rot, proj)


def _hyena_features(L):
    pos = jnp.arange(L, dtype=F32)
    t = pos / max(L - 1, 1)
    w = 2.0 * math.pi * pos / L
    f = jnp.linspace(1e-4, HYENA_BANDS - 1, HYENA_BANDS, dtype=F32)
    wf = w[:, None] * f[None, :]
    z = jnp.concatenate([t[:, None], jnp.cos(wf), -jnp.sin(wf)], -1)
    return jnp.pad(z, ((0, 0), (0, LANES - HYENA_EMB))), t[:, None]


def _filter_kernel(z_ref, t_ref, w1_ref, b1_ref, w2_ref, b2_ref, w3_ref, b3_ref, w4f_ref, w4b_ref,
                   b4f_ref, b4b_ref, fr_ref, delta_ref, hsum_ref, hdiff_ref):
    hp = lax.Precision.HIGHEST
    fr = fr_ref[...]
    h = jnp.sin(fr * (jnp.dot(z_ref[...], w1_ref[...], precision=hp, preferred_element_type=F32) + b1_ref[...]))
    h = jnp.sin(fr * (jnp.dot(h, w2_ref[...], precision=hp, preferred_element_type=F32) + b2_ref[...]))
    h = jnp.sin(fr * (jnp.dot(h, w3_ref[...], precision=hp, preferred_element_type=F32) + b3_ref[...]))
    hf = jnp.dot(h, w4f_ref[...], precision=hp, preferred_element_type=F32) + b4f_ref[...]
    hb = jnp.dot(h, w4b_ref[...], precision=hp, preferred_element_type=F32) + b4b_ref[...]
    dec = jnp.exp(-t_ref[...] * delta_ref[...])
    hf = hf * dec
    hb = hb * dec
    norm = jnp.sum(jnp.abs(hf), 0, keepdims=True) + jnp.sum(jnp.abs(hb), 0, keepdims=True)
    inv = 1.0 / norm
    hf = hf * inv
    row = lax.broadcasted_iota(jnp.int32, hb.shape, 0)
    hb = jnp.where(row == 0, 0.0, hb * inv)
    hsum_ref[...] = hf + hb
    hdiff_ref[...] = hf - hb


def _hyena_filters(zf, t, w1, b1, w2, b2, w3, b3, w4, b4, freq, L, tc):
    OC = HYENA_ORDER * MIX_WIDTH
    max_decay = math.log(HYENA_DECAY_TARGET) / HYENA_FAST_DECAY
    min_decay = math.log(HYENA_DECAY_TARGET) / HYENA_SLOW_DECAY
    deltas = jnp.abs(jnp.linspace(min_decay, max_decay, MIX_WIDTH, dtype=F32))
    deltas = jnp.tile(deltas, HYENA_ORDER)[None, :]
    w1p = jnp.pad(w1.astype(F32), ((0, LANES - HYENA_EMB), (0, 0)))
    row = lambda a: a.astype(F32)[None, :]
    full = lambda shape: pl.BlockSpec(shape, lambda j: (0,) * len(shape))
    nb = OC // tc
    return pl.pallas_call(
        _filter_kernel,
        grid=(nb,),
        in_specs=[full((L, LANES)), full((L, 1)),
                  full((LANES, HYENA_FFN)), full((1, HYENA_FFN)),
                  full((HYENA_FFN, HYENA_FFN)), full((1, HYENA_FFN)),
                  full((HYENA_FFN, HYENA_FFN)), full((1, HYENA_FFN)),
                  pl.BlockSpec((HYENA_FFN, tc), lambda j: (0, j)),
                  pl.BlockSpec((HYENA_FFN, tc), lambda j: (0, nb + j)),
                  pl.BlockSpec((1, tc), lambda j: (0, j)),
                  pl.BlockSpec((1, tc), lambda j: (0, nb + j)),
                  full((1, HYENA_FFN)),
                  pl.BlockSpec((1, tc), lambda j: (0, j))],
        out_specs=[pl.BlockSpec((L, tc), lambda j: (0, j)),
                   pl.BlockSpec((L, tc), lambda j: (0, j))],
        out_shape=[jax.ShapeDtypeStruct((L, OC), F32), jax.ShapeDtypeStruct((L, OC), F32)],
        compiler_params=_cparams(("parallel",)),
        name="hyena_filters",
    )(zf, t, w1p, row(b1), w2.astype(F32), row(b2), w3.astype(F32), row(b3), w4.astype(F32), w4.astype(F32),
      row(b4), row(b4), row(freq), deltas)


def _dft_matrices(L):
    n = 2 * L
    k = jnp.arange(L, dtype=jnp.int32)
    ks = (k[:, None] * k[None, :]) % n
    ang = ks.astype(F32) * (2.0 * math.pi / n)
    fc = jnp.cos(ang)
    sn = jnp.sin(ang)
    alt = jnp.where(k % 2 == 0, 1.0, -1.0).astype(F32)
    fs = jnp.where(k[:, None] == 0, alt[None, :], sn)
    fst = jnp.where(k[None, :] == 0, alt[:, None], sn)
    return fc.astype(BF16), fs.astype(BF16), fst.astype(BF16)


def _short_conv_kernel(u_ref, w_ref, b_ref, o_ref):
    L = u_ref.shape[0]
    u = u_ref[...].astype(F32)
    row = lax.broadcasted_iota(jnp.int32, u.shape, 0)
    prev = jnp.where(row == 0, 0.0, pltpu.roll(u, 1, axis=0))
    nxt = jnp.where(row == L - 1, 0.0, pltpu.roll(u, L - 1, axis=0))
    w = w_ref[...]
    o_ref[...] = (prev * w[0:1] + u * w[1:2] + nxt * w[2:3] + b_ref[...]).astype(o_ref.dtype)


def _short_conv(proj, conv_w, conv_b, B, L, tc):
    return pl.pallas_call(
        _short_conv_kernel,
        grid=(B, HYENA_WIDTH // tc),
        in_specs=[pl.BlockSpec((None, L, tc), lambda b, j: (b, 0, j)),
                  pl.BlockSpec((3, tc), lambda b, j: (0, j)),
                  pl.BlockSpec((1, tc), lambda b, j: (0, j))],
        out_specs=pl.BlockSpec((None, L, tc), lambda b, j: (b, 0, j)),
        out_shape=jax.ShapeDtypeStruct((B, L, HYENA_WIDTH), BF16),
        compiler_params=_cparams(("parallel", "parallel")),
        name="short_conv",
    )(proj, conv_w.astype(F32), conv_b.astype(F32)[None, :])


def _hy_fwd_kernel(fc_ref, fs_ref, z_ref, pa_ref, qa_ref, qb_ref, sb_ref, ya_ref, yb_ref):
    z = z_ref[...]
    a = jnp.dot(fc_ref[...], z, preferred_element_type=F32)
    b = jnp.dot(fs_ref[...], z, preferred_element_type=F32)
    ya_ref[...] = (a * pa_ref[...] - b * qa_ref[...]).astype(ya_ref.dtype)
    yb_ref[...] = (a * qb_ref[...] + b * sb_ref[...]).astype(yb_ref.dtype)


def _hy_inv_kernel(fc_ref, fst_ref, ya_ref, yb_ref, z_ref, xg_ref, bias_ref, o_ref):
    y = jnp.dot(fc_ref[...], ya_ref[...], preferred_element_type=F32)
    y = y + jnp.dot(fst_ref[...], yb_ref[...], preferred_element_type=F32)
    z = z_ref[...].astype(F32)
    o_ref[...] = (xg_ref[...].astype(F32) * (y + bias_ref[...] * z)).astype(o_ref.dtype)


def _hyena_long_conv_gate(uc, z_arr, z_blk, xg_blk, fc, fs, fst, coef, bias, B, L, tm_f, tm_i):
    C = MIX_WIDTH
    pa, qa, qb, sb = coef
    ya, yb = pl.pallas_call(
        _hy_fwd_kernel,
        grid=(B, L // tm_f),
        in_specs=[pl.BlockSpec((tm_f, L), lambda b, i: (i, 0)),
                  pl.BlockSpec((tm_f, L), lambda b, i: (i, 0)),
                  pl.BlockSpec((None, L, C), lambda b, i: (b, 0, z_blk)),
                  pl.BlockSpec((tm_f, C), lambda b, i: (i, 0)),
                  pl.BlockSpec((tm_f, C), lambda b, i: (i, 0)),
                  pl.BlockSpec((tm_f, C), lambda b, i: (i, 0)),
                  pl.BlockSpec((tm_f, C), lambda b, i: (i, 0))],
        out_specs=[pl.BlockSpec((None, tm_f, C), lambda b, i: (b, i, 0)),
                   pl.BlockSpec((None, tm_f, C), lambda b, i: (b, i, 0))],
        out_shape=[jax.ShapeDtypeStruct((B, L, C), BF16), jax.ShapeDtypeStruct((B, L, C), BF16)],
        compiler_params=_cparams(("parallel", "arbitrary")),
        name="hyena_dft_fwd",
    )(fc, fs, z_arr, pa, qa, qb, sb)
    return pl.pallas_call(
        _hy_inv_kernel,
        grid=(B, L // tm_i),
        in_specs=[pl.BlockSpec((tm_i, L), lambda b, i: (i, 0)),
                  pl.BlockSpec((tm_i, L), lambda b, i: (i, 0)),
                  pl.BlockSpec((None, L, C), lambda b, i: (b, 0, 0)),
                  pl.BlockSpec((None, L, C), lambda b, i: (b, 0, 0)),
                  pl.BlockSpec((None, tm_i, C), lambda b, i: (b, i, z_blk)),
                  pl.BlockSpec((None, tm_i, C), lambda b, i: (b, i, xg_blk)),
                  pl.BlockSpec((1, C), lambda b, i: (0, 0))],
        out_specs=pl.BlockSpec((None, tm_i, C), lambda b, i: (b, i, 0)),
        out_shape=jax.ShapeDtypeStruct((B, L, C), BF16),
        compiler_params=_cparams(("parallel", "arbitrary")),
        name="hyena_dft_inv",
    )(fc, fst, ya, yb, z_arr, uc, bias)


def _hyena_mixer(proj, j, p, B, L):
    C = MIX_WIDTH
    OC = HYENA_ORDER * C
    n = 2 * L
    zf, t = _hyena_features(L)
    hsum, hdiff = _hyena_filters(zf, t, p['filt_w1'][j], p['filt_b1'][j], p['filt_w2'][j], p['filt_b2'][j],
                                 p['filt_w3'][j], p['filt_b3'][j], p['filt_w4'][j], p['filt_b4'][j],
                                 p['filt_freq'][j], L, 256)
    fc, fs, fst = _dft_matrices(L)
    parts = []
    for h in (hsum, hdiff):
        hi = h.astype(BF16)
        parts += [hi, (h - hi.astype(F32)).astype(BF16)]
    hcat = jnp.concatenate(parts, axis=1)
    spec = _matmul(jnp.concatenate([fc, fs], axis=0), hcat, F32, 512, 512)
    gc = spec[:L, 0:OC] + spec[:L, OC:2 * OC]
    nyq = spec[L:L + 1, 0:OC] + spec[L:L + 1, OC:2 * OC]
    gs = spec[L:, 2 * OC:3 * OC] + spec[L:, 3 * OC:4 * OC]
    k0 = (jnp.arange(L) == 0)[:, None]
    ca = jnp.where(k0, 1.0 / n, 2.0 / n).astype(F32)
    pa = ca * gc
    qa = ca * jnp.where(k0, 0.0, gs)
    qb = qa
    sb = ca * jnp.where(k0, nyq, gc)
    uc = _short_conv(proj, p['conv_w_c'][j], p['conv_b_c'][j], B, L, 256)
    bias = p['long_bias_c'][j].astype(F32)
    tm_f = min(512, L)
    tm_i = min(256, L)
    z = uc
    z_blk = 0
    for o in range(HYENA_ORDER):
        sl = slice(o * C, (o + 1) * C)
        coef = (pa[:, sl], qa[:, sl], qb[:, sl], sb[:, sl])
        z = _hyena_long_conv_gate(uc, z, z_blk, o + 1, fc, fs, fst, coef, bias[o][None, :], B, L, tm_f, tm_i)
        z_blk = 0
    return z


def _pack_halves(x):
    c = x.shape[1] // 2
    bits = lax.bitcast_convert_type(x.astype(BF16).astype(F32), jnp.uint32)
    return (bits[:, :c] & jnp.uint32(0xFFFF0000)) | (bits[:, c:] >> 16)


def _unpack_halves(w):
    hi = lax.bitcast_convert_type(w & jnp.uint32(0xFFFF0000), F32)
    lo = lax.bitcast_convert_type(w << 16, F32)
    return hi, lo


def _post_kernel(x_ref, mix_ref, qm_ref, km_ref, vm_ref, wo_ref, g_ref, b_ref, rw_ref, rb_ref,
                 xo_ref, xb_ref, gate_ref, route_ref, count_ref):
    tm = x_ref.shape[0]
    lane = lax.broadcasted_iota(jnp.int32, (tm, LANES), 1)
    first = lane < HEAD_DIM
    outs = []
    for p in range(MEM_WIDTH // LANES):
        sl = slice(p * LANES, (p + 1) * LANES)
        qs = qm_ref[:, sl]
        zero = jnp.zeros_like(qs)
        q2 = jnp.concatenate([jnp.where(first, qs, zero), jnp.where(first, zero, qs)], axis=0)
        s = lax.dot_general(q2, km_ref[:, sl], (((1,), (1,)), ((), ())), preferred_element_type=F32) * ATT_SCALE
        m = jnp.max(s, -1, keepdims=True)
        e = jnp.exp(s - m)
        l = jnp.sum(e, -1, keepdims=True)
        o2 = jnp.dot(e.astype(BF16), vm_ref[:, sl], preferred_element_type=F32) / l
        outs.append(jnp.where(first, o2[:tm], o2[tm:]).astype(BF16))
    mem_out = jnp.concatenate(outs, axis=1)
    acc = jnp.dot(mix_ref[...], wo_ref[:MIX_WIDTH, :], preferred_element_type=F32)
    acc = acc + jnp.dot(mem_out, wo_ref[MIX_WIDTH:, :], preferred_element_type=F32)
    x1 = _layer_norm(DEEPNORM_ALPHA * x_ref[...] + acc, g_ref[...], b_ref[...])
    xo_ref[...] = x1
    x_hi = x1.astype(BF16)
    xb_ref[...] = x_hi
    x_lo = (x1 - x_hi.astype(F32)).astype(BF16)
    prod = jnp.dot(x_hi, rw_ref[...], preferred_element_type=F32)
    logits = (prod[:, :LANES] + prod[:, LANES:]
              + jnp.dot(x_lo, rw_ref[:, :LANES], preferred_element_type=F32) + rb_ref[...])
    lane_f = lane.astype(F32)
    work = logits
    chosen = jnp.zeros_like(logits)
    gates = jnp.zeros_like(logits)
    denom = jnp.zeros((tm, 1), F32)
    sels, idxs = [], []
    m0 = None
    for k in range(TOP_K):
        m = jnp.max(work, -1, keepdims=True)
        idx = jnp.min(jnp.where(work == m, lane_f, float(LANES)), -1, keepdims=True)
        sel = lane_f == idx
        if k == 0:
            m0 = m
        ek = jnp.exp(m - m0)
        gates = jnp.where(lane == k, ek, gates)
        chosen = chosen + jnp.where(sel, 1.0, 0.0)
        denom = denom + ek
        work = jnp.where(sel, -jnp.inf, work)
        sels.append(sel)
        idxs.append(idx)
    gate_ref[...] = gates / denom

    r_i = lax.broadcasted_iota(jnp.int32, (tm, tm), 0)
    c_i = lax.broadcasted_iota(jnp.int32, (tm, tm), 1)
    before = jnp.where(r_i > c_i, 1.0, 0.0).astype(BF16)
    prefix = jnp.dot(before, chosen.astype(BF16), preferred_element_type=F32)
    route = jnp.zeros((tm, LANES), jnp.int32)
    for k in range(TOP_K):
        rank = jnp.sum(jnp.where(sels[k], prefix, 0.0), -1, keepdims=True)
        route = jnp.where(lane == k, rank.astype(jnp.int32), route)
        route = jnp.where(lane == TOP_K + k, idxs[k].astype(jnp.int32), route)
    route_ref[...] = route
    count_ref[...] = jnp.broadcast_to(jnp.sum(chosen, 0, keepdims=True), count_ref.shape)


def _post_mixer(x, mix, proj, qm_blk, kmem, vmem, wo, g, b, rw, rb, L, tm):
    N, D = x.shape
    nl = L // tm
    CAT = MIX_WIDTH + MEM_WIDTH
    return pl.pallas_call(
        _post_kernel,
        grid=(N // tm,),
        in_specs=[pl.BlockSpec((tm, D), lambda i: (i, 0)),
                  pl.BlockSpec((tm, MIX_WIDTH), lambda i: (i, 0)),
                  pl.BlockSpec((tm, MEM_WIDTH), lambda i: (i, qm_blk)),
                  pl.BlockSpec((None, kmem.shape[1], MEM_WIDTH), lambda i: (i // nl, 0, 0)),
                  pl.BlockSpec((None, vmem.shape[1], MEM_WIDTH), lambda i: (i // nl, 0, 0)),
                  pl.BlockSpec((CAT, D), lambda i: (0, 0)),
                  pl.BlockSpec((1, D), lambda i: (0, 0)),
                  pl.BlockSpec((1, D), lambda i: (0, 0)),
                  pl.BlockSpec((D, 2 * LANES), lambda i: (0, 0)),
                  pl.BlockSpec((1, LANES), lambda i: (0, 0))],
        out_specs=[pl.BlockSpec((tm, D), lambda i: (i, 0)),
                   pl.BlockSpec((tm, D), lambda i: (i, 0)),
                   pl.BlockSpec((tm, LANES), lambda i: (i, 0)),
                   pl.BlockSpec((tm, LANES), lambda i: (i, 0)),
                   pl.BlockSpec((SUBLANES, LANES), lambda i: (i, 0))],
        out_shape=[jax.ShapeDtypeStruct((N, D), F32), jax.ShapeDtypeStruct((N, D), BF16),
                   jax.ShapeDtypeStruct((N, LANES), F32), jax.ShapeDtypeStruct((N, LANES), jnp.int32),
                   jax.ShapeDtypeStruct((N // tm * SUBLANES, LANES), F32)],
        compiler_params=_cparams(("parallel",)),
        name="post_mixer",
    )(x, mix, proj, kmem, vmem, wo, g, b, rw, rb)


EXPERT_TILE = 512
CHUNK = SUBLANES


def _tile_buffer_rows(tm):
    return tm * TOP_K + CHUNK * N_EXPERTS


def _route_tables(route, tile_counts, tm, max_tiles):
    N = route.shape[0]
    nt = N // tm
    cnt = tile_counts.reshape(nt, SUBLANES, LANES)[:, 0, :N_EXPERTS].astype(jnp.int32)
    rc = ((cnt + CHUNK - 1) // CHUNK) * CHUNK
    base = jnp.cumsum(rc, axis=0) - rc
    padded = ((jnp.sum(rc, axis=0) + EXPERT_TILE - 1) // EXPERT_TILE) * EXPERT_TILE
    ends = jnp.cumsum(padded)
    offs = ends - padded
    local = jnp.cumsum(rc, axis=1) - rc
    hbm_start = (offs[None, :] + base).reshape(-1).astype(jnp.int32)
    n_chunks = (rc // CHUNK).reshape(-1).astype(jnp.int32)
    total_chunks = jnp.sum(rc // CHUNK, axis=1).astype(jnp.int32)
    eid = route[:, TOP_K:2 * TOP_K]
    tile_of = (jnp.arange(N, dtype=jnp.int32) // tm)[:, None]
    col = jnp.take(local.reshape(-1), tile_of * N_EXPERTS + eid) + route[:, :TOP_K]
    col_lanes = jnp.pad(col.astype(F32), ((0, 0), (0, LANES - TOP_K)), constant_values=-1.0)
    col_rows = jnp.pad(col.reshape(nt, tm, TOP_K).transpose(0, 2, 1), ((0, 0), (0, SUBLANES - TOP_K), (0, 0)),
                       constant_values=-1).reshape(nt * SUBLANES, tm).astype(jnp.int32)
    tile_ends = ends // EXPERT_TILE
    n_tiles = tile_ends[-1]
    t = jnp.arange(max_tiles, dtype=jnp.int32)
    t_live = jnp.minimum(t, n_tiles - 1)
    tile_expert = jnp.sum((t_live[:, None] >= tile_ends[None, :]).astype(jnp.int32), axis=1)
    tile_expert = jnp.minimum(tile_expert, N_EXPERTS - 1)
    zero_start = (ends - EXPERT_TILE).astype(jnp.int32)
    zero_valid = (padded > 0).astype(jnp.int32)
    chunks = (hbm_start, n_chunks, local.reshape(-1).astype(jnp.int32), total_chunks)
    return chunks, col_lanes, col_rows, tile_expert, n_tiles.reshape(1).astype(jnp.int32), zero_start, zero_valid


def _chunk_copies(i, hs_ref, nch_ref, lo_ref, hbm_ref, buf_ref, sem, to_hbm):
    for e in range(N_EXPERTS):
        hs = hs_ref[i * N_EXPERTS + e]
        lo = lo_ref[i * N_EXPERTS + e]

        def start(c, carry, hs=hs, lo=lo):
            h = hbm_ref.at[pl.ds(pl.multiple_of(hs + c * CHUNK, CHUNK), CHUNK)]
            b = buf_ref.at[pl.ds(pl.multiple_of(lo + c * CHUNK, CHUNK), CHUNK)]
            if to_hbm:
                pltpu.make_async_copy(b, h, sem).start()
            else:
                pltpu.make_async_copy(h, b, sem).start()
            return carry

        lax.fori_loop(0, nch_ref[i * N_EXPERTS + e], start, 0)


def _chunk_waits(n, hbm_ref, buf_ref, sem, to_hbm):
    def wait(c, carry):
        h = hbm_ref.at[pl.ds(0, CHUNK)]
        b = buf_ref.at[pl.ds(0, CHUNK)]
        (pltpu.make_async_copy(b, h, sem) if to_hbm else pltpu.make_async_copy(h, b, sem)).wait()
        return carry

    lax.fori_loop(0, n, wait, 0)


def _dispatch_kernel(hs_ref, nch_ref, lo_ref, tot_ref, zstart_ref, zvalid_ref, xb_ref, colr_ref, xs_hbm,
                     perm_ref, zero_ref, sem, zero_sem):
    i = pl.program_id(0)
    tm = xb_ref.shape[0]
    rows = perm_ref.shape[0]

    @pl.when(i == 0)
    def _():
        zero_ref[...] = jnp.zeros_like(zero_ref)

        def zero_copy(e):
            start = pl.multiple_of(zstart_ref[e], EXPERT_TILE)
            return pltpu.make_async_copy(zero_ref, xs_hbm.at[pl.ds(start, EXPERT_TILE)], zero_sem)

        for e in range(N_EXPERTS):
            @pl.when(zvalid_ref[e] > 0)
            def _():
                zero_copy(e).start()
        for e in range(N_EXPERTS):
            @pl.when(zvalid_ref[e] > 0)
            def _():
                zero_copy(e).wait()

    r = lax.broadcasted_iota(jnp.int32, (rows, tm), 0)
    onehot = jnp.zeros((rows, tm), F32)
    for k in range(TOP_K):
        onehot = jnp.where(colr_ref[k:k + 1, :] == r, 1.0, onehot)
    perm = jnp.dot(onehot.astype(BF16), xb_ref[...], preferred_element_type=F32)
    perm_ref[...] = _pack_halves(perm)
    _chunk_copies(i, hs_ref, nch_ref, lo_ref, xs_hbm, perm_ref, sem, True)
    _chunk_waits(tot_ref[i], xs_hbm, perm_ref, sem, True)


def _dispatch(xb, col_rows, chunks, zero_start, zero_valid, rows, tm):
    N, D = xb.shape
    pb = _tile_buffer_rows(tm)
    return pl.pallas_call(
        _dispatch_kernel,
        grid_spec=pltpu.PrefetchScalarGridSpec(
            num_scalar_prefetch=6,
            grid=(N // tm,),
            in_specs=[pl.BlockSpec((tm, D), lambda i, *_: (i, 0)),
                      pl.BlockSpec((SUBLANES, tm), lambda i, *_: (i, 0))],
            out_specs=pl.BlockSpec(memory_space=pl.ANY),
            scratch_shapes=[pltpu.VMEM((pb, D // 2), jnp.uint32),
                            pltpu.VMEM((EXPERT_TILE, D // 2), jnp.uint32),
                            pltpu.SemaphoreType.DMA,
                            pltpu.SemaphoreType.DMA]),
        out_shape=jax.ShapeDtypeStruct((rows, D // 2), jnp.uint32),
        compiler_params=_cparams(("arbitrary",)),
        name="moe_dispatch",
    )(*chunks, zero_start, zero_valid, xb, col_rows)


def _expert_kernel(te_ref, nt_ref, xs_ref, w1_ref, b1_ref, w2_ref, b2_ref, o_ref, w1b_ref, w2b_ref):
    t = pl.program_id(0)

    @pl.when((t == 0) | (te_ref[t] != te_ref[jnp.maximum(t - 1, 0)]))
    def _():
        w1b_ref[...] = w1_ref[...].astype(BF16)
        w2b_ref[...] = w2_ref[...].astype(BF16)

    @pl.when(t < nt_ref[0])
    def _():
        hi, lo = _unpack_halves(xs_ref[...])
        half = hi.shape[1]
        h = jnp.dot(hi.astype(BF16), w1b_ref[:half, :], preferred_element_type=F32)
        h = h + jnp.dot(lo.astype(BF16), w1b_ref[half:, :], preferred_element_type=F32) + b1_ref[...]
        gl = jnp.minimum(h[:, :D_EXPERT], SWIGLU_LIMIT)
        ul = jnp.clip(h[:, D_EXPERT:], -SWIGLU_LIMIT, SWIGLU_LIMIT)
        a = (ul + 1.0) * (gl * jax.nn.sigmoid(SWIGLU_ALPHA * gl))
        o = jnp.dot(a.astype(BF16), w2b_ref[...], preferred_element_type=F32) + b2_ref[...]
        o_ref[...] = _pack_halves(o)


def _experts(xs, tile_expert, n_tiles, w1, b1, w2, b2, layer):
    rows, W = xs.shape
    D = 2 * W
    last = lambda t, te, nt: jnp.minimum(t, nt[0] - 1)
    return pl.pallas_call(
        _expert_kernel,
        grid_spec=pltpu.PrefetchScalarGridSpec(
            num_scalar_prefetch=2,
            grid=(rows // EXPERT_TILE,),
            in_specs=[pl.BlockSpec((EXPERT_TILE, W), lambda t, te, nt: (last(t, te, nt), 0)),
                      pl.BlockSpec((None, None, D, 2 * D_EXPERT), lambda t, te, nt: (layer, te[t], 0, 0)),
                      pl.BlockSpec((None, None, 1, 2 * D_EXPERT), lambda t, te, nt: (layer, te[t], 0, 0)),
                      pl.BlockSpec((None, None, D_EXPERT, D), lambda t, te, nt: (layer, te[t], 0, 0)),
                      pl.BlockSpec((None, None, 1, D), lambda t, te, nt: (layer, te[t], 0, 0))],
            out_specs=pl.BlockSpec((EXPERT_TILE, W), lambda t, te, nt: (last(t, te, nt), 0)),
            scratch_shapes=[pltpu.VMEM((D, 2 * D_EXPERT), BF16), pltpu.VMEM((D_EXPERT, D), BF16)]),
        out_shape=jax.ShapeDtypeStruct((rows, W), jnp.uint32),
        compiler_params=_cparams(("arbitrary",)),
        name="moe_experts",
    )(tile_expert, n_tiles, xs, w1, b1, w2, b2)


def _combine_kernel(hs_ref, nch_ref, lo_ref, tot_ref, x_ref, gate_ref, col_ref, g_ref, b_ref, eo_hbm, o_ref,
                    buf_ref, sem):
    i = pl.program_id(0)
    tm = x_ref.shape[0]
    rows = buf_ref.shape[0]

    @pl.when(i == 0)
    def _():
        buf_ref[...] = jnp.zeros_like(buf_ref)

    _chunk_copies(i, hs_ref, nch_ref, lo_ref, eo_hbm, buf_ref, sem, False)
    _chunk_waits(tot_ref[i], eo_hbm, buf_ref, sem, False)

    gates = gate_ref[...]
    cols = col_ref[...]
    lane = lax.broadcasted_iota(jnp.int32, gates.shape, 1)
    r = lax.broadcasted_iota(jnp.int32, (tm, rows), 1).astype(F32)
    w = jnp.zeros((tm, rows), F32)
    for k in range(TOP_K):
        gk = jnp.sum(jnp.where(lane == k, gates, 0.0), -1, keepdims=True)
        ck = jnp.sum(jnp.where(lane == k, cols, 0.0), -1, keepdims=True)
        w = jnp.where(ck == r, gk, w)
    w_hi = w.astype(BF16)
    w_lo = (w - w_hi.astype(F32)).astype(BF16)
    hi, lo = _unpack_halves(buf_ref[...])
    vals = jnp.concatenate([hi.astype(BF16), lo.astype(BF16)], axis=1)
    both = jnp.dot(jnp.concatenate([w_hi, w_lo], axis=0), vals, preferred_element_type=F32)
    y = both[:tm] + both[tm:]
    o_ref[...] = _layer_norm(DEEPNORM_ALPHA * x_ref[...] + y, g_ref[...], b_ref[...])


def _combine(x, gates, col_lanes, g, b, chunks, eo, tm):
    N, D = x.shape
    pb = _tile_buffer_rows(tm)
    return pl.pallas_call(
        _combine_kernel,
        grid_spec=pltpu.PrefetchScalarGridSpec(
            num_scalar_prefetch=4,
            grid=(N // tm,),
            in_specs=[pl.BlockSpec((tm, D), lambda i, *_: (i, 0)),
                      pl.BlockSpec((tm, LANES), lambda i, *_: (i, 0)),
                      pl.BlockSpec((tm, LANES), lambda i, *_: (i, 0)),
                      pl.BlockSpec((1, D), lambda i, *_: (0, 0)),
                      pl.BlockSpec((1, D), lambda i, *_: (0, 0)),
                      pl.BlockSpec(memory_space=pl.ANY)],
            out_specs=pl.BlockSpec((tm, D), lambda i, *_: (i, 0)),
            scratch_shapes=[pltpu.VMEM((pb, D // 2), jnp.uint32),
                            pltpu.SemaphoreType.DMA]),
        out_shape=jax.ShapeDtypeStruct((N, D), F32),
        compiler_params=_cparams(("arbitrary",)),
        name="moe_combine",
    )(*chunks, x, gates, col_lanes, g, b, eo)


def _moe(x1, x1b, gates, route, tile_counts, w1, b1, w2, b2, g, b, layer, tm):
    N = x1.shape[0]
    max_tiles = (N * TOP_K + CHUNK * N_EXPERTS * (N // tm)) // EXPERT_TILE + N_EXPERTS
    chunks, col_lanes, col_rows, tile_expert, n_tiles, zero_start, zero_valid = _route_tables(
        route, tile_counts, tm, max_tiles)
    xs = _dispatch(x1b, col_rows, chunks, zero_start, zero_valid, max_tiles * EXPERT_TILE, tm)
    eo = _experts(xs, tile_expert, n_tiles, w1, b1, w2, b2, layer)
    return _combine(x1, gates, col_lanes, g, b, chunks, eo, tm)


def kernel(x, mem, w_in_a, rpb_a, w_in_b, q_norm_b, k_norm_b, w_in_c, conv_w_c, conv_b_c, filt_w1, filt_b1, filt_w2, filt_b2, filt_w3, filt_b3, filt_w4, filt_b4, filt_freq, long_bias_c, w_mem_kv, w_out, ln1_g, ln1_b, ln2_g, ln2_b, router_w, router_b, moe_w1, moe_b1, moe_w2, moe_b2):
    B, L, D = x.shape
    N = B * L
    M = mem.shape[1]
    tm = min(512, L)
    hy = dict(filt_w1=filt_w1, filt_b1=filt_b1, filt_w2=filt_w2, filt_b2=filt_b2, filt_w3=filt_w3,
              filt_b3=filt_b3, filt_w4=filt_w4, filt_b4=filt_b4, filt_freq=filt_freq,
              conv_w_c=conv_w_c, conv_b_c=conv_b_c, long_bias_c=long_bias_c)

    n_layers = w_mem_kv.shape[0]
    wkv = jnp.transpose(w_mem_kv, (1, 0, 2)).reshape(D, n_layers * 2 * MEM_WIDTH).astype(BF16)
    kv_all = _matmul(mem.reshape(B * M, D), wkv, BF16, 512, 512).reshape(B, M, n_layers, 2, MEM_WIDTH)

    w1_all = moe_w1.astype(F32)
    w2_all = moe_w2.astype(F32)
    b1_all = moe_b1.astype(F32)[:, :, None, :]
    b2_all = moe_b2.astype(F32)[:, :, None, :]

    q_cols, k_cols, out_rows_b = _gqa_column_perms()

    xf = x.reshape(N, D).astype(F32)
    for i in range(DEPTH):
        kind = i % N_MIXERS
        j = i // N_MIXERS
        wo = w_out[i]
        if kind == 0:
            proj = _matmul(xf, w_in_a[j].astype(BF16), BF16, tm, 3 * MIX_WIDTH + MEM_WIDTH)
            bias_tab = _na_bias_table(rpb_a[j], L // GRID_W)
            mix = _na_attention(proj.reshape(B, L, -1), bias_tab, B, L).reshape(N, MIX_WIDTH)
            qm_blk = (3 * MIX_WIDTH) // MEM_WIDTH
        elif kind == 1:
            w = w_in_b[j]
            w = jnp.concatenate([w[:, :MIX_WIDTH][:, q_cols],
                                 w[:, MIX_WIDTH:MIX_WIDTH + KV_WIDTH][:, k_cols],
                                 w[:, MIX_WIDTH + KV_WIDTH:]], axis=1)
            proj = _matmul(xf, w.astype(BF16), BF16, tm, w.shape[1])
            cos_t, sin_t = _rope_lane_tables(L)
            dim, _ = _pair_lane_dims()
            gq = q_norm_b[j].astype(F32)[dim][None, :]
            gk = k_norm_b[j].astype(F32)[dim][None, :]
            q_rot, k_rot = _qk_prep(proj, cos_t, sin_t, gq, gk, L, tm)
            mix = _gqa_attention(q_rot.reshape(B, L, -1), k_rot.reshape(B, L, -1), proj.reshape(B, L, -1),
                                 B, L, min(128, L)).reshape(N, MIX_WIDTH)
            wo = jnp.concatenate([wo[:MIX_WIDTH][out_rows_b], wo[MIX_WIDTH:]], axis=0)
            qm_blk = (MIX_WIDTH + 2 * KV_WIDTH) // MEM_WIDTH
        else:
            proj = _matmul(xf, w_in_c[j].astype(BF16), BF16, tm, HYENA_WIDTH + MEM_WIDTH)
            mix = _hyena_mixer(proj.reshape(B, L, -1), j, hy, B, L).reshape(N, MIX_WIDTH)
            qm_blk = HYENA_WIDTH // MEM_WIDTH
        kmem = kv_all[:, :, i, 0, :]
        vmem = kv_all[:, :, i, 1, :]
        rw = jnp.pad(router_w[i].astype(F32), ((0, 0), (0, LANES - N_EXPERTS)))
        rw_hi = rw.astype(BF16)
        rw = jnp.concatenate([rw_hi, (rw - rw_hi.astype(F32)).astype(BF16)], axis=1)
        rb = jnp.pad(router_b[i].astype(F32), (0, LANES - N_EXPERTS), constant_values=NEG_BIG)[None, :]
        x1, x1b, gates, route, counts = _post_mixer(
            xf, mix, proj, qm_blk, kmem, vmem, wo.astype(BF16),
            ln1_g[i].astype(F32)[None, :], ln1_b[i].astype(F32)[None, :], rw, rb, L, tm)
        xf = _moe(x1, x1b, gates, route, counts, w1_all, b1_all, w2_all, b2_all,
                  ln2_g[i].astype(F32)[None, :], ln2_b[i].astype(F32)[None, :], i, tm)
    return xf.reshape(B, L, D).astype(x.dtype)
```

```python
import functools
import math

import numpy as np
import jax
import jax.numpy as jnp
from jax import lax
from jax.experimental import pallas as pl
from jax.experimental.pallas import tpu as pltpu

F32 = jnp.float32
BF16 = jnp.bfloat16

D_MODEL = 1024
DEPTH = 4
GRID_W = 64
N_MIXERS = 3
HEAD_DIM = 64
MIX_WIDTH = 768
MIX_HEADS = MIX_WIDTH // HEAD_DIM
MEM_HEADS = 4
MEM_WIDTH = MEM_HEADS * HEAD_DIM
NA_KH = 8
NA_KW = 16
NA_ROW_UNROLL = 8
KV_HEADS = 4
KV_WIDTH = KV_HEADS * HEAD_DIM
GQA_GROUP = MIX_HEADS // KV_HEADS
ROPE_THETA = 10000.0
QK_NORM_EPS = 1e-6
HYENA_ORDER = 2
HYENA_BANDS = 16
HYENA_EMB = 2 * HYENA_BANDS + 1
HYENA_FFN = 64
HYENA_WIDTH = (HYENA_ORDER + 1) * MIX_WIDTH
HYENA_DECAY_TARGET = 1e-2
HYENA_FAST_DECAY = 0.3
HYENA_SLOW_DECAY = 1.5
N_EXPERTS = 32
TOP_K = 4
D_EXPERT = D_MODEL
SWIGLU_LIMIT = 7.0
SWIGLU_ALPHA = 1.702
DEEPNORM_ALPHA = (2 * DEPTH) ** 0.25
LN_EPS = 1e-5

LANES = 128
SUBLANES = 8
VMEM_LIMIT = 56 * 1024 * 1024
NEG_BIG = -1e30
ATT_SCALE = HEAD_DIM ** -0.5


def _cparams(sem):
    return pltpu.CompilerParams(dimension_semantics=sem, vmem_limit_bytes=VMEM_LIMIT)


def _layer_norm(x, g, b):
    mu = jnp.mean(x, -1, keepdims=True)
    xc = x - mu
    var = jnp.mean(xc * xc, -1, keepdims=True)
    return xc * lax.rsqrt(var + LN_EPS) * g + b


def _mm_kernel(x_ref, w_ref, o_ref):
    o_ref[...] = jnp.dot(x_ref[...].astype(BF16), w_ref[...],
                         preferred_element_type=F32).astype(o_ref.dtype)


def _matmul(x, w, out_dtype, tm, tn):
    M, K = x.shape
    N = w.shape[1]
    tm = min(tm, M)
    tn = min(tn, N)
    assert M % tm == 0 and N % tn == 0
    return pl.pallas_call(
        _mm_kernel,
        grid=(M // tm, N // tn),
        in_specs=[pl.BlockSpec((tm, K), lambda i, j: (i, 0)),
                  pl.BlockSpec((K, tn), lambda i, j: (0, j))],
        out_specs=pl.BlockSpec((tm, tn), lambda i, j: (i, j)),
        out_shape=jax.ShapeDtypeStruct((M, N), out_dtype),
        compiler_params=_cparams(("parallel", "arbitrary")),
        name="matmul",
    )(x, w)


def _na_bias_table(rpb, rows):
    del rows
    col = np.arange(GRID_W)
    c0 = np.clip(col - NA_KW // 2, 0, GRID_W - NA_KW)
    cc = np.arange(GRID_W)
    valid = (cc[None, :] >= c0[:, None]) & (cc[None, :] < c0[:, None] + NA_KW)
    coff = np.clip(cc[None, :] - col[:, None] + NA_KW - 1, 0, 2 * NA_KW - 2)
    d = np.arange(NA_KH)
    i = np.arange(NA_KH)
    roff = i[None, :] - d[:, None] + NA_KH - 1
    t = rpb.astype(F32)[:, roff]
    t = t[:, :, :, coff]
    t = t.transpose(0, 1, 3, 2, 4)
    t = jnp.where(valid[None, None, :, None, :], t, NEG_BIG)
    return t.reshape(rpb.shape[0], NA_KH, GRID_W, NA_KH * GRID_W)


def _na_kernel(q_ref, k_ref, v_ref, bias_ref, o_ref, *, rows):
    lane = lax.broadcasted_iota(jnp.int32, (GRID_W, LANES), 1)
    first = lane < HEAD_DIM
    band = NA_KH * GRID_W

    def body(r, carry):
        r0 = jnp.clip(r - NA_KH // 2, 0, rows - NA_KH)
        d = r - r0
        qoff = pl.multiple_of(r * GRID_W, GRID_W)
        koff = pl.multiple_of(r0 * GRID_W, GRID_W)
        q = q_ref[pl.ds(qoff, GRID_W), :]
        kb = k_ref[pl.ds(koff, band), :]
        vb = v_ref[pl.ds(koff, band), :]
        zero = jnp.zeros_like(q)
        q2 = jnp.concatenate([jnp.where(first, q, zero), jnp.where(first, zero, q)], axis=0)
        s = lax.dot_general(q2, kb, (((1,), (1,)), ((), ())), preferred_element_type=F32)
        bias = jnp.concatenate([bias_ref[0, d], bias_ref[1, d]], axis=0)
        s = s * ATT_SCALE + bias
        m = jnp.max(s, -1, keepdims=True)
        e = jnp.exp(s - m)
        l = jnp.sum(e, -1, keepdims=True)
        o2 = jnp.dot(e.astype(BF16), vb, preferred_element_type=F32) / l
        o = jnp.where(first, o2[:GRID_W], o2[GRID_W:])
        o_ref[pl.ds(qoff, GRID_W), :] = o.astype(o_ref.dtype)
        return carry

    lax.fori_loop(0, rows, body, 0, unroll=NA_ROW_UNROLL)


def _na_attention(proj, bias_tab, B, L):
    rows = L // GRID_W
    assert rows >= NA_KH
    npair = MIX_WIDTH // LANES
    return pl.pallas_call(
        functools.partial(_na_kernel, rows=rows),
        grid=(npair, B),
        in_specs=[pl.BlockSpec((None, L, LANES), lambda p, b: (b, 0, p)),
                  pl.BlockSpec((None, L, LANES), lambda p, b: (b, 0, npair + p)),
                  pl.BlockSpec((None, L, LANES), lambda p, b: (b, 0, 2 * npair + p)),
                  pl.BlockSpec((2, NA_KH, GRID_W, NA_KH * GRID_W), lambda p, b: (p, 0, 0, 0))],
        out_specs=pl.BlockSpec((None, L, LANES), lambda p, b: (b, 0, p)),
        out_shape=jax.ShapeDtypeStruct((B, L, MIX_WIDTH), BF16),
        compiler_params=_cparams(("parallel", "parallel")),
        name="na_attention",
    )(proj, proj, proj, bias_tab)


def _pair_lane_dims():
    l = np.arange(LANES)
    return 2 * (l % 32) + (l >= HEAD_DIM).astype(np.int64), ((l % HEAD_DIM) >= 32).astype(np.int64)


def _gqa_column_perms():
    dim, which = _pair_lane_dims()
    q_cols = []
    out_rows = []
    for P in range(KV_HEADS // 2):
        for g in range(GQA_GROUP):
            heads = ((2 * P) * GQA_GROUP + g, (2 * P + 1) * GQA_GROUP + g)
            q_cols.append(np.array([heads[w] * HEAD_DIM + dd for dd, w in zip(dim, which)]))
            out_rows.append(np.concatenate([heads[0] * HEAD_DIM + np.arange(HEAD_DIM),
                                            heads[1] * HEAD_DIM + np.arange(HEAD_DIM)]))
    k_cols = []
    for P in range(KV_HEADS // 2):
        heads = (2 * P, 2 * P + 1)
        k_cols.append(np.array([heads[w] * HEAD_DIM + dd for dd, w in zip(dim, which)]))
    return np.concatenate(q_cols), np.concatenate(k_cols), np.concatenate(out_rows)


def _rope_lane_tables(L):
    pos = jnp.arange(L)
    row = (pos // GRID_W).astype(F32)
    col = (pos % GRID_W).astype(F32)
    axis_dim = HEAD_DIM // 2
    inv_freq = ROPE_THETA ** (-jnp.arange(0, axis_dim, 2, dtype=F32) / axis_dim)
    ang = jnp.concatenate([row[:, None] * inv_freq, col[:, None] * inv_freq], -1)
    c = jnp.cos(ang)
    s = jnp.sin(ang)
    return jnp.concatenate([c, c, c, c], -1), jnp.concatenate([-s, -s, s, s], -1)


def _qk_prep_kernel(q_ref, k_ref, cos_ref, sin_ref, gq_ref, gk_ref, qo_ref, ko_ref):
    tm = q_ref.shape[0]
    lane = lax.broadcasted_iota(jnp.int32, (tm, LANES), 1)
    head_a = (lane % HEAD_DIM) < 32
    c = cos_ref[...]
    s = sin_ref[...]

    def norm_rope(x, g):
        x = x.astype(F32)
        x2 = x * x
        tot = jnp.sum(x2, -1, keepdims=True)
        sa = jnp.sum(jnp.where(head_a, x2, 0.0), -1, keepdims=True)
        inv = jnp.where(head_a, lax.rsqrt(sa * (1.0 / HEAD_DIM) + QK_NORM_EPS),
                        lax.rsqrt((tot - sa) * (1.0 / HEAD_DIM) + QK_NORM_EPS))
        xn = x * inv * g
        return xn * c + pltpu.roll(xn, HEAD_DIM, axis=1) * s

    for j in range(q_ref.shape[1] // LANES):
        sl = slice(j * LANES, (j + 1) * LANES)
        qo_ref[:, sl] = (norm_rope(q_ref[:, sl], gq_ref[...]) * ATT_SCALE).astype(qo_ref.dtype)
    for j in range(k_ref.shape[1] // LANES):
        sl = slice(j * LANES, (j + 1) * LANES)
        ko_ref[:, sl] = norm_rope(k_ref[:, sl], gk_ref[...]).astype(ko_ref.dtype)


def _qk_prep(proj, cos_t, sin_t, gq, gk, L, tm):
    N = proj.shape[0]
    nl = L // tm
    return pl.pallas_call(
        _qk_prep_kernel,
        grid=(N // tm,),
        in_specs=[pl.BlockSpec((tm, MIX_WIDTH), lambda i: (i, 0)),
                  pl.BlockSpec((tm, KV_WIDTH), lambda i: (i, MIX_WIDTH // KV_WIDTH)),
                  pl.BlockSpec((tm, LANES), lambda i: (i % nl, 0)),
                  pl.BlockSpec((tm, LANES), lambda i: (i % nl, 0)),
                  pl.BlockSpec((1, LANES), lambda i: (0, 0)),
                  pl.BlockSpec((1, LANES), lambda i: (0, 0))],
        out_specs=[pl.BlockSpec((tm, MIX_WIDTH), lambda i: (i, 0)),
                   pl.BlockSpec((tm, KV_WIDTH), lambda i: (i, 0))],
        out_shape=[jax.ShapeDtypeStruct((N, MIX_WIDTH), BF16),
                   jax.ShapeDtypeStruct((N, KV_WIDTH), BF16)],
        compiler_params=_cparams(("parallel",)),
        name="qk_prep",
    )(proj, proj, cos_t, sin_t, gq, gk)


GQA_KV_CHUNK = 1024


def _gqa_kernel(q_ref, k_ref, v_ref, o_ref, q6_ref, s_ref, m_ref, l_ref, acc_ref):
    tq = q_ref.shape[0]
    tk = s_ref.shape[2]
    nk = k_ref.shape[0] // tk
    lane = lax.broadcasted_iota(jnp.int32, (tq, LANES), 1)
    head_a = (lane % HEAD_DIM) < 32
    q = q_ref[...]
    for g in range(GQA_GROUP):
        qs = q[:, g * LANES:(g + 1) * LANES]
        zero = jnp.zeros_like(qs)
        q6_ref[(2 * g) * tq:(2 * g + 1) * tq, :] = jnp.where(head_a, qs, zero)
        q6_ref[(2 * g + 1) * tq:(2 * g + 2) * tq, :] = jnp.where(head_a, zero, qs)
    m_ref[...] = jnp.full(m_ref.shape, -jnp.inf, F32)
    l_ref[...] = jnp.zeros(l_ref.shape, F32)
    acc_ref[...] = jnp.zeros(acc_ref.shape, F32)

    def scores(j):
        return lax.dot_general(q6_ref[...], k_ref[j * tk:(j + 1) * tk, :], (((1,), (1,)), ((), ())),
                               preferred_element_type=F32)

    s_ref[0] = scores(0)
    for j in range(nk):
        if j + 1 < nk:
            s_ref[(j + 1) % 2] = scores(j + 1)
        s = s_ref[j % 2]
        m_old = m_ref[...]
        m_new = jnp.maximum(m_old, jnp.max(s, -1, keepdims=True))
        alpha = jnp.exp(m_old - m_new)
        p = jnp.exp(s - jnp.concatenate([m_new] * (tk // LANES), axis=1))
        l_ref[...] = alpha * l_ref[...] + jnp.sum(p, -1, keepdims=True)
        acc_ref[...] = alpha * acc_ref[...] + jnp.dot(p.astype(BF16), v_ref[j * tk:(j + 1) * tk, :],
                                                      preferred_element_type=F32)
        m_ref[...] = m_new
    o6 = acc_ref[...] / l_ref[...]
    for g in range(GQA_GROUP):
        oa = o6[(2 * g) * tq:(2 * g + 1) * tq]
        ob = o6[(2 * g + 1) * tq:(2 * g + 2) * tq]
        o_ref[:, g * LANES:(g + 1) * LANES] = jnp.where(lane < HEAD_DIM, oa, ob).astype(o_ref.dtype)


def _gqa_attention(q_rot, k_rot, proj, B, L, tq):
    slab_w = GQA_GROUP * LANES
    v_blk0 = (MIX_WIDTH + KV_WIDTH) // LANES
    rows6 = 2 * GQA_GROUP * tq
    tk = min(GQA_KV_CHUNK, L)
    return pl.pallas_call(
        _gqa_kernel,
        grid=(B, KV_HEADS // 2, L // tq),
        in_specs=[pl.BlockSpec((None, tq, slab_w), lambda b, p, i: (b, i, p)),
                  pl.BlockSpec((None, L, LANES), lambda b, p, i: (b, 0, p)),
                  pl.BlockSpec((None, L, LANES), lambda b, p, i: (b, 0, v_blk0 + p))],
        out_specs=pl.BlockSpec((None, tq, slab_w), lambda b, p, i: (b, i, p)),
        out_shape=jax.ShapeDtypeStruct((B, L, MIX_WIDTH), BF16),
        scratch_shapes=[pltpu.VMEM((rows6, LANES), BF16),
                        pltpu.VMEM((2, rows6, tk), F32),
                        pltpu.VMEM((rows6, LANES), F32),
                        pltpu.VMEM((rows6, LANES), F32),
                        pltpu.VMEM((rows6, LANES), F32)],
        compiler_params=_cparams(("parallel", "parallel", "arbitrary")),
        name="gqa_attention",
    )(q_rot, k_rot, proj)


def _hyena_features(L):
    pos = jnp.arange(L, dtype=F32)
    t = pos / max(L - 1, 1)
    w = 2.0 * math.pi * pos / L
    f = jnp.linspace(1e-4, HYENA_BANDS - 1, HYENA_BANDS, dtype=F32)
    wf = w[:, None] * f[None, :]
    z = jnp.concatenate([t[:, None], jnp.cos(wf), -jnp.sin(wf)], -1)
    return jnp.pad(z, ((0, 0), (0, LANES - HYENA_EMB))), t[:, None]


def _filter_kernel(z_ref, t_ref, w1_ref, b1_ref, w2_ref, b2_ref, w3_ref, b3_ref, w4f_ref, w4b_ref,
                   b4f_ref, b4b_ref, fr_ref, delta_ref, hsum_ref, hdiff_ref):
    hp = lax.Precision.HIGHEST
    fr = fr_ref[...]
    h = jnp.sin(fr * (jnp.dot(z_ref[...], w1_ref[...], precision=hp, preferred_element_type=F32) + b1_ref[...]))
    h = jnp.sin(fr * (jnp.dot(h, w2_ref[...], precision=hp, preferred_element_type=F32) + b2_ref[...]))
    h = jnp.sin(fr * (jnp.dot(h, w3_ref[...], precision=hp, preferred_element_type=F32) + b3_ref[...]))
    hf = jnp.dot(h, w4f_ref[...], precision=hp, preferred_element_type=F32) + b4f_ref[...]
    hb = jnp.dot(h, w4b_ref[...], precision=hp, preferred_element_type=F32) + b4b_ref[...]
    dec = jnp.exp(-t_ref[...] * delta_ref[...])
    hf = hf * dec
    hb = hb * dec
    norm = jnp.sum(jnp.abs(hf), 0, keepdims=True) + jnp.sum(jnp.abs(hb), 0, keepdims=True)
    inv = 1.0 / norm
    hf = hf * inv
    row = lax.broadcasted_iota(jnp.int32, hb.shape, 0)
    hb = jnp.where(row == 0, 0.0, hb * inv)
    hsum_ref[...] = hf + hb
    hdiff_ref[...] = hf - hb


def _hyena_filters(zf, t, w1, b1, w2, b2, w3, b3, w4, b4, freq, L, tc):
    OC = HYENA_ORDER * MIX_WIDTH
    max_decay = math.log(HYENA_DECAY_TARGET) / HYENA_FAST_DECAY
    min_decay = math.log(HYENA_DECAY_TARGET) / HYENA_SLOW_DECAY
    deltas = jnp.abs(jnp.linspace(min_decay, max_decay, MIX_WIDTH, dtype=F32))
    deltas = jnp.tile(deltas, HYENA_ORDER)[None, :]
    w1p = jnp.pad(w1.astype(F32), ((0, LANES - HYENA_EMB), (0, 0)))
    row = lambda a: a.astype(F32)[None, :]
    full = lambda shape: pl.BlockSpec(shape, lambda j: (0,) * len(shape))
    nb = OC // tc
    return pl.pallas_call(
        _filter_kernel,
        grid=(nb,),
        in_specs=[full((L, LANES)), full((L, 1)),
                  full((LANES, HYENA_FFN)), full((1, HYENA_FFN)),
                  full((HYENA_FFN, HYENA_FFN)), full((1, HYENA_FFN)),
                  full((HYENA_FFN, HYENA_FFN)), full((1, HYENA_FFN)),
                  pl.BlockSpec((HYENA_FFN, tc), lambda j: (0, j)),
                  pl.BlockSpec((HYENA_FFN, tc), lambda j: (0, nb + j)),
                  pl.BlockSpec((1, tc), lambda j: (0, j)),
                  pl.BlockSpec((1, tc), lambda j: (0, nb + j)),
                  full((1, HYENA_FFN)),
                  pl.BlockSpec((1, tc), lambda j: (0, j))],
        out_specs=[pl.BlockSpec((L, tc), lambda j: (0, j)),
                   pl.BlockSpec((L, tc), lambda j: (0, j))],
        out_shape=[jax.ShapeDtypeStruct((L, OC), F32), jax.ShapeDtypeStruct((L, OC), F32)],
        compiler_params=_cparams(("parallel",)),
        name="hyena_filters",
    )(zf, t, w1p, row(b1), w2.astype(F32), row(b2), w3.astype(F32), row(b3), w4.astype(F32), w4.astype(F32),
      row(b4), row(b4), row(freq), deltas)


def _dft_matrices(L):
    n = 2 * L
    k = jnp.arange(L, dtype=jnp.int32)
    ks = (k[:, None] * k[None, :]) % n
    ang = ks.astype(F32) * (2.0 * math.pi / n)
    fc = jnp.cos(ang)
    sn = jnp.sin(ang)
    alt = jnp.where(k % 2 == 0, 1.0, -1.0).astype(F32)
    fs = jnp.where(k[:, None] == 0, alt[None, :], sn)
    fst = jnp.where(k[None, :] == 0, alt[:, None], sn)
    return fc.astype(BF16), fs.astype(BF16), fst.astype(BF16)


def _short_conv_kernel(u_ref, w_ref, b_ref, o_ref):
    L = u_ref.shape[0]
    u = u_ref[...].astype(F32)
    row = lax.broadcasted_iota(jnp.int32, u.shape, 0)
    prev = jnp.where(row == 0, 0.0, pltpu.roll(u, 1, axis=0))
    nxt = jnp.where(row == L - 1, 0.0, pltpu.roll(u, L - 1, axis=0))
    w = w_ref[...]
    o_ref[...] = (prev * w[0:1] + u * w[1:2] + nxt * w[2:3] + b_ref[...]).astype(o_ref.dtype)


def _short_conv(proj, conv_w, conv_b, B, L, tc):
    return pl.pallas_call(
        _short_conv_kernel,
        grid=(B, HYENA_WIDTH // tc),
        in_specs=[pl.BlockSpec((None, L, tc), lambda b, j: (b, 0, j)),
                  pl.BlockSpec((3, tc), lambda b, j: (0, j)),
                  pl.BlockSpec((1, tc), lambda b, j: (0, j))],
        out_specs=pl.BlockSpec((None, L, tc), lambda b, j: (b, 0, j)),
        out_shape=jax.ShapeDtypeStruct((B, L, HYENA_WIDTH), BF16),
        compiler_params=_cparams(("parallel", "parallel")),
        name="short_conv",
    )(proj, conv_w.astype(F32), conv_b.astype(F32)[None, :])


def _hy_fwd_kernel(fc_ref, fs_ref, z_ref, pa_ref, qa_ref, qb_ref, sb_ref, ya_ref, yb_ref):
    z = z_ref[...]
    a = jnp.dot(fc_ref[...], z, preferred_element_type=F32)
    b = jnp.dot(fs_ref[...], z, preferred_element_type=F32)
    ya_ref[...] = (a * pa_ref[...] - b * qa_ref[...]).astype(ya_ref.dtype)
    yb_ref[...] = (a * qb_ref[...] + b * sb_ref[...]).astype(yb_ref.dtype)


def _hy_inv_kernel(fc_ref, fst_ref, ya_ref, yb_ref, z_ref, xg_ref, bias_ref, o_ref):
    y = jnp.dot(fc_ref[...], ya_ref[...], preferred_element_type=F32)
    y = y + jnp.dot(fst_ref[...], yb_ref[...], preferred_element_type=F32)
    z = z_ref[...].astype(F32)
    o_ref[...] = (xg_ref[...].astype(F32) * (y + bias_ref[...] * z)).astype(o_ref.dtype)


def _hyena_long_conv_gate(uc, z_arr, z_blk, xg_blk, fc, fs, fst, coef, bias, B, L, tm_f, tm_i):
    C = MIX_WIDTH
    pa, qa, qb, sb = coef
    ya, yb = pl.pallas_call(
        _hy_fwd_kernel,
        grid=(B, L // tm_f),
        in_specs=[pl.BlockSpec((tm_f, L), lambda b, i: (i, 0)),
                  pl.BlockSpec((tm_f, L), lambda b, i: (i, 0)),
                  pl.BlockSpec((None, L, C), lambda b, i: (b, 0, z_blk)),
                  pl.BlockSpec((tm_f, C), lambda b, i: (i, 0)),
                  pl.BlockSpec((tm_f, C), lambda b, i: (i, 0)),
                  pl.BlockSpec((tm_f, C), lambda b, i: (i, 0)),
                  pl.BlockSpec((tm_f, C), lambda b, i: (i, 0))],
        out_specs=[pl.BlockSpec((None, tm_f, C), lambda b, i: (b, i, 0)),
                   pl.BlockSpec((None, tm_f, C), lambda b, i: (b, i, 0))],
        out_shape=[jax.ShapeDtypeStruct((B, L, C), BF16), jax.ShapeDtypeStruct((B, L, C), BF16)],
        compiler_params=_cparams(("parallel", "arbitrary")),
        name="hyena_dft_fwd",
    )(fc, fs, z_arr, pa, qa, qb, sb)
    return pl.pallas_call(
        _hy_inv_kernel,
        grid=(B, L // tm_i),
        in_specs=[pl.BlockSpec((tm_i, L), lambda b, i: (i, 0)),
                  pl.BlockSpec((tm_i, L), lambda b, i: (i, 0)),
                  pl.BlockSpec((None, L, C), lambda b, i: (b, 0, 0)),
                  pl.BlockSpec((None, L, C), lambda b, i: (b, 0, 0)),
                  pl.BlockSpec((None, tm_i, C), lambda b, i: (b, i, z_blk)),
                  pl.BlockSpec((None, tm_i, C), lambda b, i: (b, i, xg_blk)),
                  pl.BlockSpec((1, C), lambda b, i: (0, 0))],
        out_specs=pl.BlockSpec((None, tm_i, C), lambda b, i: (b, i, 0)),
        out_shape=jax.ShapeDtypeStruct((B, L, C), BF16),
        compiler_params=_cparams(("parallel", "arbitrary")),
        name="hyena_dft_inv",
    )(fc, fst, ya, yb, z_arr, uc, bias)


def _hyena_mixer(proj, j, p, B, L):
    C = MIX_WIDTH
    OC = HYENA_ORDER * C
    n = 2 * L
    zf, t = _hyena_features(L)
    hsum, hdiff = _hyena_filters(zf, t, p['filt_w1'][j], p['filt_b1'][j], p['filt_w2'][j], p['filt_b2'][j],
                                 p['filt_w3'][j], p['filt_b3'][j], p['filt_w4'][j], p['filt_b4'][j],
                                 p['filt_freq'][j], L, 256)
    fc, fs, fst = _dft_matrices(L)
    parts = []
    for h in (hsum, hdiff):
        hi = h.astype(BF16)
        parts += [hi, (h - hi.astype(F32)).astype(BF16)]
    hcat = jnp.concatenate(parts, axis=1)
    spec = _matmul(jnp.concatenate([fc, fs], axis=0), hcat, F32, 512, 512)
    gc = spec[:L, 0:OC] + spec[:L, OC:2 * OC]
    nyq = spec[L:L + 1, 0:OC] + spec[L:L + 1, OC:2 * OC]
    gs = spec[L:, 2 * OC:3 * OC] + spec[L:, 3 * OC:4 * OC]
    k0 = (jnp.arange(L) == 0)[:, None]
    ca = jnp.where(k0, 1.0 / n, 2.0 / n).astype(F32)
    pa = ca * gc
    qa = ca * jnp.where(k0, 0.0, gs)
    qb = qa
    sb = ca * jnp.where(k0, nyq, gc)
    uc = _short_conv(proj, p['conv_w_c'][j], p['conv_b_c'][j], B, L, 256)
    bias = p['long_bias_c'][j].astype(F32)
    tm_f = min(512, L)
    tm_i = min(256, L)
    z = uc
    z_blk = 0
    for o in range(HYENA_ORDER):
        sl = slice(o * C, (o + 1) * C)
        coef = (pa[:, sl], qa[:, sl], qb[:, sl], sb[:, sl])
        z = _hyena_long_conv_gate(uc, z, z_blk, o + 1, fc, fs, fst, coef, bias[o][None, :], B, L, tm_f, tm_i)
        z_blk = 0
    return z


def _pack_halves(x):
    c = x.shape[1] // 2
    bits = lax.bitcast_convert_type(x.astype(BF16).astype(F32), jnp.uint32)
    return (bits[:, :c] & jnp.uint32(0xFFFF0000)) | (bits[:, c:] >> 16)


def _unpack_halves(w):
    hi = lax.bitcast_convert_type(w & jnp.uint32(0xFFFF0000), F32)
    lo = lax.bitcast_convert_type(w << 16, F32)
    return hi, lo


def _post_kernel(x_ref, mix_ref, qm_ref, km_ref, vm_ref, wo_ref, g_ref, b_ref, rw_ref, rb_ref,
                 xo_ref, xb_ref, gate_ref, route_ref, count_ref):
    tm = x_ref.shape[0]
    lane = lax.broadcasted_iota(jnp.int32, (tm, LANES), 1)
    first = lane < HEAD_DIM
    outs = []
    for p in range(MEM_WIDTH // LANES):
        sl = slice(p * LANES, (p + 1) * LANES)
        qs = qm_ref[:, sl]
        zero = jnp.zeros_like(qs)
        q2 = jnp.concatenate([jnp.where(first, qs, zero), jnp.where(first, zero, qs)], axis=0)
        s = lax.dot_general(q2, km_ref[:, sl], (((1,), (1,)), ((), ())), preferred_element_type=F32) * ATT_SCALE
        m = jnp.max(s, -1, keepdims=True)
        e = jnp.exp(s - m)
        l = jnp.sum(e, -1, keepdims=True)
        o2 = jnp.dot(e.astype(BF16), vm_ref[:, sl], preferred_element_type=F32) / l
        outs.append(jnp.where(first, o2[:tm], o2[tm:]).astype(BF16))
    mem_out = jnp.concatenate(outs, axis=1)
    acc = jnp.dot(mix_ref[...], wo_ref[:MIX_WIDTH, :], preferred_element_type=F32)
    acc = acc + jnp.dot(mem_out, wo_ref[MIX_WIDTH:, :], preferred_element_type=F32)
    x1 = _layer_norm(DEEPNORM_ALPHA * x_ref[...] + acc, g_ref[...], b_ref[...])
    xo_ref[...] = x1
    x_hi = x1.astype(BF16)
    xb_ref[...] = x_hi
    x_lo = (x1 - x_hi.astype(F32)).astype(BF16)
    prod = jnp.dot(x_hi, rw_ref[...], preferred_element_type=F32)
    logits = (prod[:, :LANES] + prod[:, LANES:]
              + jnp.dot(x_lo, rw_ref[:, :LANES], preferred_element_type=F32) + rb_ref[...])
    lane_f = lane.astype(F32)
    work = logits
    chosen = jnp.zeros_like(logits)
    gates = jnp.zeros_like(logits)
    denom = jnp.zeros((tm, 1), F32)
    sels, idxs = [], []
    m0 = None
    for k in range(TOP_K):
        m = jnp.max(work, -1, keepdims=True)
        idx = jnp.min(jnp.where(work == m, lane_f, float(LANES)), -1, keepdims=True)
        sel = lane_f == idx
        if k == 0:
            m0 = m
        ek = jnp.exp(m - m0)
        gates = jnp.where(lane == k, ek, gates)
        chosen = chosen + jnp.where(sel, 1.0, 0.0)
        denom = denom + ek
        work = jnp.where(sel, -jnp.inf, work)
        sels.append(sel)
        idxs.append(idx)
    gate_ref[...] = gates / denom

    r_i = lax.broadcasted_iota(jnp.int32, (tm, tm), 0)
    c_i = lax.broadcasted_iota(jnp.int32, (tm, tm), 1)
    before = jnp.where(r_i > c_i, 1.0, 0.0).astype(BF16)
    prefix = jnp.dot(before, chosen.astype(BF16), preferred_element_type=F32)
    route = jnp.zeros((tm, LANES), jnp.int32)
    for k in range(TOP_K):
        rank = jnp.sum(jnp.where(sels[k], prefix, 0.0), -1, keepdims=True)
        route = jnp.where(lane == k, rank.astype(jnp.int32), route)
        route = jnp.where(lane == TOP_K + k, idxs[k].astype(jnp.int32), route)
    route_ref[...] = route
    count_ref[...] = jnp.broadcast_to(jnp.sum(chosen, 0, keepdims=True), count_ref.shape)


def _post_mixer(x, mix, proj, qm_blk, kmem, vmem, wo, g, b, rw, rb, L, tm):
    N, D = x.shape
    nl = L // tm
    CAT = MIX_WIDTH + MEM_WIDTH
    return pl.pallas_call(
        _post_kernel,
        grid=(N // tm,),
        in_specs=[pl.BlockSpec((tm, D), lambda i: (i, 0)),
                  pl.BlockSpec((tm, MIX_WIDTH), lambda i: (i, 0)),
                  pl.BlockSpec((tm, MEM_WIDTH), lambda i: (i, qm_blk)),
                  pl.BlockSpec((None, kmem.shape[1], MEM_WIDTH), lambda i: (i // nl, 0, 0)),
                  pl.BlockSpec((None, vmem.shape[1], MEM_WIDTH), lambda i: (i // nl, 0, 0)),
                  pl.BlockSpec((CAT, D), lambda i: (0, 0)),
                  pl.BlockSpec((1, D), lambda i: (0, 0)),
                  pl.BlockSpec((1, D), lambda i: (0, 0)),
                  pl.BlockSpec((D, 2 * LANES), lambda i: (0, 0)),
                  pl.BlockSpec((1, LANES), lambda i: (0, 0))],
        out_specs=[pl.BlockSpec((tm, D), lambda i: (i, 0)),
                   pl.BlockSpec((tm, D), lambda i: (i, 0)),
                   pl.BlockSpec((tm, LANES), lambda i: (i, 0)),
                   pl.BlockSpec((tm, LANES), lambda i: (i, 0)),
                   pl.BlockSpec((SUBLANES, LANES), lambda i: (i, 0))],
        out_shape=[jax.ShapeDtypeStruct((N, D), F32), jax.ShapeDtypeStruct((N, D), BF16),
                   jax.ShapeDtypeStruct((N, LANES), F32), jax.ShapeDtypeStruct((N, LANES), jnp.int32),
                   jax.ShapeDtypeStruct((N // tm * SUBLANES, LANES), F32)],
        compiler_params=_cparams(("parallel",)),
        name="post_mixer",
    )(x, mix, proj, kmem, vmem, wo, g, b, rw, rb)


EXPERT_TILE = 512
CHUNK = SUBLANES


def _tile_buffer_rows(tm):
    return tm * TOP_K + CHUNK * N_EXPERTS


def _route_tables(route, tile_counts, tm, max_tiles):
    N = route.shape[0]
    nt = N // tm
    cnt = tile_counts.reshape(nt, SUBLANES, LANES)[:, 0, :N_EXPERTS].astype(jnp.int32)
    rc = ((cnt + CHUNK - 1) // CHUNK) * CHUNK
    base = jnp.cumsum(rc, axis=0) - rc
    padded = ((jnp.sum(rc, axis=0) + EXPERT_TILE - 1) // EXPERT_TILE) * EXPERT_TILE
    ends = jnp.cumsum(padded)
    offs = ends - padded
    local = jnp.cumsum(rc, axis=1) - rc
    hbm_start = (offs[None, :] + base).reshape(-1).astype(jnp.int32)
    n_chunks = (rc // CHUNK).reshape(-1).astype(jnp.int32)
    total_chunks = jnp.sum(rc // CHUNK, axis=1).astype(jnp.int32)
    eid = route[:, TOP_K:2 * TOP_K]
    local_tok = jnp.repeat(local, tm, axis=0)[:, None, :]
    hit = eid[:, :, None] == jnp.arange(N_EXPERTS, dtype=jnp.int32)[None, None, :]
    col = jnp.sum(jnp.where(hit, local_tok, 0), axis=2) + route[:, :TOP_K]
    col_lanes = jnp.pad(col.astype(F32), ((0, 0), (0, LANES - TOP_K)), constant_values=-1.0)
    col_rows = jnp.pad(col.reshape(nt, tm, TOP_K).transpose(0, 2, 1), ((0, 0), (0, SUBLANES - TOP_K), (0, 0)),
                       constant_values=-1).reshape(nt * SUBLANES, tm).astype(jnp.int32)
    tile_ends = ends // EXPERT_TILE
    n_tiles = tile_ends[-1]
    t = jnp.arange(max_tiles, dtype=jnp.int32)
    t_live = jnp.minimum(t, n_tiles - 1)
    tile_expert = jnp.sum((t_live[:, None] >= tile_ends[None, :]).astype(jnp.int32), axis=1)
    tile_expert = jnp.minimum(tile_expert, N_EXPERTS - 1)
    zero_start = (ends - EXPERT_TILE).astype(jnp.int32)
    zero_valid = (padded > 0).astype(jnp.int32)
    chunks = (hbm_start, n_chunks, local.reshape(-1).astype(jnp.int32), total_chunks)
    return chunks, col_lanes, col_rows, tile_expert, n_tiles.reshape(1).astype(jnp.int32), zero_start, zero_valid


def _chunk_copies(i, hs_ref, nch_ref, lo_ref, hbm_ref, buf_ref, sem, to_hbm):
    for e in range(N_EXPERTS):
        hs = hs_ref[i * N_EXPERTS + e]
        lo = lo_ref[i * N_EXPERTS + e]

        def start(c, carry, hs=hs, lo=lo):
            h = hbm_ref.at[pl.ds(pl.multiple_of(hs + c * CHUNK, CHUNK), CHUNK)]
            b = buf_ref.at[pl.ds(pl.multiple_of(lo + c * CHUNK, CHUNK), CHUNK)]
            if to_hbm:
                pltpu.make_async_copy(b, h, sem).start()
            else:
                pltpu.make_async_copy(h, b, sem).start()
            return carry

        lax.fori_loop(0, nch_ref[i * N_EXPERTS + e], start, 0)


def _chunk_waits(n, hbm_ref, buf_ref, sem, to_hbm):
    def wait(c, carry):
        h = hbm_ref.at[pl.ds(0, CHUNK)]
        b = buf_ref.at[pl.ds(0, CHUNK)]
        (pltpu.make_async_copy(b, h, sem) if to_hbm else pltpu.make_async_copy(h, b, sem)).wait()
        return carry

    lax.fori_loop(0, n, wait, 0)


def _dispatch_kernel(hs_ref, nch_ref, lo_ref, tot_ref, zstart_ref, zvalid_ref, xb_ref, colr_ref, xs_hbm,
                     perm_ref, zero_ref, sem, zero_sem):
    i = pl.program_id(0)
    tm = xb_ref.shape[0]
    rows = perm_ref.shape[1]
    slot = i % 2

    @pl.when(i == 0)
    def _():
        zero_ref[...] = jnp.zeros_like(zero_ref)

        def zero_copy(e):
            start = pl.multiple_of(zstart_ref[e], EXPERT_TILE)
            return pltpu.make_async_copy(zero_ref, xs_hbm.at[pl.ds(start, EXPERT_TILE)], zero_sem)

        for e in range(N_EXPERTS):
            @pl.when(zvalid_ref[e] > 0)
            def _():
                zero_copy(e).start()
        for e in range(N_EXPERTS):
            @pl.when(zvalid_ref[e] > 0)
            def _():
                zero_copy(e).wait()

    r = lax.broadcasted_iota(jnp.int32, (rows, tm), 0)
    onehot = jnp.zeros((rows, tm), F32)
    for k in range(TOP_K):
        onehot = jnp.where(colr_ref[k:k + 1, :] == r, 1.0, onehot)
    perm = jnp.dot(onehot.astype(BF16), xb_ref[...], preferred_element_type=F32)
    perm_ref[slot] = _pack_halves(perm)
    _chunk_copies(i, hs_ref, nch_ref, lo_ref, xs_hbm, perm_ref.at[slot], sem.at[slot], True)

    @pl.when(i > 0)
    def _():
        _chunk_waits(tot_ref[jnp.maximum(i - 1, 0)], xs_hbm, perm_ref.at[1 - slot], sem.at[1 - slot], True)

    @pl.when(i == pl.num_programs(0) - 1)
    def _():
        _chunk_waits(tot_ref[i], xs_hbm, perm_ref.at[slot], sem.at[slot], True)


def _dispatch(xb, col_rows, chunks, zero_start, zero_valid, rows, tm):
    N, D = xb.shape
    pb = _tile_buffer_rows(tm)
    return pl.pallas_call(
        _dispatch_kernel,
        grid_spec=pltpu.PrefetchScalarGridSpec(
            num_scalar_prefetch=6,
            grid=(N // tm,),
            in_specs=[pl.BlockSpec((tm, D), lambda i, *_: (i, 0)),
                      pl.BlockSpec((SUBLANES, tm), lambda i, *_: (i, 0))],
            out_specs=pl.BlockSpec(memory_space=pl.ANY),
            scratch_shapes=[pltpu.VMEM((2, pb, D // 2), jnp.uint32),
                            pltpu.VMEM((EXPERT_TILE, D // 2), jnp.uint32),
                            pltpu.SemaphoreType.DMA((2,)),
                            pltpu.SemaphoreType.DMA]),
        out_shape=jax.ShapeDtypeStruct((rows, D // 2), jnp.uint32),
        compiler_params=_cparams(("arbitrary",)),
        name="moe_dispatch",
    )(*chunks, zero_start, zero_valid, xb, col_rows)


def _expert_kernel(te_ref, nt_ref, xs_ref, w1_ref, b1_ref, w2_ref, b2_ref, o_ref, w1b_ref, w2b_ref):
    t = pl.program_id(0)

    @pl.when((t == 0) | (te_ref[t] != te_ref[jnp.maximum(t - 1, 0)]))
    def _():
        w1b_ref[...] = w1_ref[...].astype(BF16)
        w2b_ref[...] = w2_ref[...].astype(BF16)

    @pl.when(t < nt_ref[0])
    def _():
        hi, lo = _unpack_halves(xs_ref[...])
        half = hi.shape[1]
        h = jnp.dot(hi.astype(BF16), w1b_ref[:half, :], preferred_element_type=F32)
        h = h + jnp.dot(lo.astype(BF16), w1b_ref[half:, :], preferred_element_type=F32) + b1_ref[...]
        gl = jnp.minimum(h[:, :D_EXPERT], SWIGLU_LIMIT)
        ul = jnp.clip(h[:, D_EXPERT:], -SWIGLU_LIMIT, SWIGLU_LIMIT)
        a = (ul + 1.0) * (gl * jax.nn.sigmoid(SWIGLU_ALPHA * gl))
        o = jnp.dot(a.astype(BF16), w2b_ref[...], preferred_element_type=F32) + b2_ref[...]
        o_ref[...] = _pack_halves(o)


def _experts(xs, tile_expert, n_tiles, w1, b1, w2, b2, layer):
    rows, W = xs.shape
    D = 2 * W
    last = lambda t, te, nt: jnp.minimum(t, nt[0] - 1)
    return pl.pallas_call(
        _expert_kernel,
        grid_spec=pltpu.PrefetchScalarGridSpec(
            num_scalar_prefetch=2,
            grid=(rows // EXPERT_TILE,),
            in_specs=[pl.BlockSpec((EXPERT_TILE, W), lambda t, te, nt: (last(t, te, nt), 0)),
                      pl.BlockSpec((None, None, D, 2 * D_EXPERT), lambda t, te, nt: (layer, te[t], 0, 0)),
                      pl.BlockSpec((None, None, 1, 2 * D_EXPERT), lambda t, te, nt: (layer, te[t], 0, 0)),
                      pl.BlockSpec((None, None, D_EXPERT, D), lambda t, te, nt: (layer, te[t], 0, 0)),
                      pl.BlockSpec((None, None, 1, D), lambda t, te, nt: (layer, te[t], 0, 0))],
            out_specs=pl.BlockSpec((EXPERT_TILE, W), lambda t, te, nt: (last(t, te, nt), 0)),
            scratch_shapes=[pltpu.VMEM((D, 2 * D_EXPERT), BF16), pltpu.VMEM((D_EXPERT, D), BF16)]),
        out_shape=jax.ShapeDtypeStruct((rows, W), jnp.uint32),
        compiler_params=_cparams(("arbitrary",)),
        name="moe_experts",
    )(tile_expert, n_tiles, xs, w1, b1, w2, b2)


def _combine_kernel(hs_ref, nch_ref, lo_ref, tot_ref, x_ref, gate_ref, col_ref, g_ref, b_ref, eo_hbm, o_ref,
                    buf_ref, sem):
    i = pl.program_id(0)
    tm = x_ref.shape[0]
    rows = buf_ref.shape[1]
    slot = i % 2

    @pl.when(i == 0)
    def _():
        buf_ref[...] = jnp.zeros_like(buf_ref)
        _chunk_copies(0, hs_ref, nch_ref, lo_ref, eo_hbm, buf_ref.at[0], sem.at[0], False)

    @pl.when(i + 1 < pl.num_programs(0))
    def _():
        _chunk_copies(i + 1, hs_ref, nch_ref, lo_ref, eo_hbm, buf_ref.at[1 - slot], sem.at[1 - slot], False)

    _chunk_waits(tot_ref[i], eo_hbm, buf_ref.at[slot], sem.at[slot], False)

    gates = gate_ref[...]
    cols = col_ref[...]
    lane = lax.broadcasted_iota(jnp.int32, gates.shape, 1)
    r = lax.broadcasted_iota(jnp.int32, (tm, rows), 1).astype(F32)
    w = jnp.zeros((tm, rows), F32)
    for k in range(TOP_K):
        gk = jnp.sum(jnp.where(lane == k, gates, 0.0), -1, keepdims=True)
        ck = jnp.sum(jnp.where(lane == k, cols, 0.0), -1, keepdims=True)
        w = jnp.where(ck == r, gk, w)
    hi, lo = _unpack_halves(buf_ref[slot])
    vals = jnp.concatenate([hi.astype(BF16), lo.astype(BF16)], axis=1)
    y = jnp.dot(w.astype(BF16), vals, preferred_element_type=F32)
    o_ref[...] = _layer_norm(DEEPNORM_ALPHA * x_ref[...] + y, g_ref[...], b_ref[...])


def _combine(x, gates, col_lanes, g, b, chunks, eo, tm):
    N, D = x.shape
    pb = _tile_buffer_rows(tm)
    return pl.pallas_call(
        _combine_kernel,
        grid_spec=pltpu.PrefetchScalarGridSpec(
            num_scalar_prefetch=4,
            grid=(N // tm,),
            in_specs=[pl.BlockSpec((tm, D), lambda i, *_: (i, 0)),
                      pl.BlockSpec((tm, LANES), lambda i, *_: (i, 0)),
                      pl.BlockSpec((tm, LANES), lambda i, *_: (i, 0)),
                      pl.BlockSpec((1, D), lambda i, *_: (0, 0)),
                      pl.BlockSpec((1, D), lambda i, *_: (0, 0)),
                      pl.BlockSpec(memory_space=pl.ANY)],
            out_specs=pl.BlockSpec((tm, D), lambda i, *_: (i, 0)),
            scratch_shapes=[pltpu.VMEM((2, pb, D // 2), jnp.uint32),
                            pltpu.SemaphoreType.DMA((2,))]),
        out_shape=jax.ShapeDtypeStruct((N, D), F32),
        compiler_params=_cparams(("arbitrary",)),
        name="moe_combine",
    )(*chunks, x, gates, col_lanes, g, b, eo)


def _moe(x1, x1b, gates, route, tile_counts, w1, b1, w2, b2, g, b, layer, tm):
    N = x1.shape[0]
    max_tiles = (N * TOP_K + CHUNK * N_EXPERTS * (N // tm)) // EXPERT_TILE + N_EXPERTS
    chunks, col_lanes, col_rows, tile_expert, n_tiles, zero_start, zero_valid = _route_tables(
        route, tile_counts, tm, max_tiles)
    xs = _dispatch(x1b, col_rows, chunks, zero_start, zero_valid, max_tiles * EXPERT_TILE, tm)
    eo = _experts(xs, tile_expert, n_tiles, w1, b1, w2, b2, layer)
    return _combine(x1, gates, col_lanes, g, b, chunks, eo, tm)


def kernel(x, mem, w_in_a, rpb_a, w_in_b, q_norm_b, k_norm_b, w_in_c, conv_w_c, conv_b_c, filt_w1, filt_b1, filt_w2, filt_b2, filt_w3, filt_b3, filt_w4, filt_b4, filt_freq, long_bias_c, w_mem_kv, w_out, ln1_g, ln1_b, ln2_g, ln2_b, router_w, router_b, moe_w1, moe_b1, moe_w2, moe_b2):
    B, L, D = x.shape
    N = B * L
    M = mem.shape[1]
    tm = min(512, L)
    hy = dict(filt_w1=filt_w1, filt_b1=filt_b1, filt_w2=filt_w2, filt_b2=filt_b2, filt_w3=filt_w3,
              filt_b3=filt_b3, filt_w4=filt_w4, filt_b4=filt_b4, filt_freq=filt_freq,
              conv_w_c=conv_w_c, conv_b_c=conv_b_c, long_bias_c=long_bias_c)

    n_layers = w_mem_kv.shape[0]
    wkv = jnp.transpose(w_mem_kv, (1, 0, 2)).reshape(D, n_layers * 2 * MEM_WIDTH).astype(BF16)
    kv_all = _matmul(mem.reshape(B * M, D), wkv, BF16, 512, 512).reshape(B, M, n_layers, 2, MEM_WIDTH)

    w1_all = moe_w1.astype(F32)
    w2_all = moe_w2.astype(F32)
    b1_all = moe_b1.astype(F32)[:, :, None, :]
    b2_all = moe_b2.astype(F32)[:, :, None, :]

    q_cols, k_cols, out_rows_b = _gqa_column_perms()

    xf = x.reshape(N, D).astype(F32)
    for i in range(DEPTH):
        kind = i % N_MIXERS
        j = i // N_MIXERS
        wo = w_out[i]
        if kind == 0:
            proj = _matmul(xf, w_in_a[j].astype(BF16), BF16, tm, 3 * MIX_WIDTH + MEM_WIDTH)
            bias_tab = _na_bias_table(rpb_a[j], L // GRID_W)
            mix = _na_attention(proj.reshape(B, L, -1), bias_tab, B, L).reshape(N, MIX_WIDTH)
            qm_blk = (3 * MIX_WIDTH) // MEM_WIDTH
        elif kind == 1:
            w = w_in_b[j]
            w = jnp.concatenate([w[:, :MIX_WIDTH][:, q_cols],
                                 w[:, MIX_WIDTH:MIX_WIDTH + KV_WIDTH][:, k_cols],
                                 w[:, MIX_WIDTH + KV_WIDTH:]], axis=1)
            proj = _matmul(xf, w.astype(BF16), BF16, tm, w.shape[1])
            cos_t, sin_t = _rope_lane_tables(L)
            dim, _ = _pair_lane_dims()
            gq = q_norm_b[j].astype(F32)[dim][None, :]
            gk = k_norm_b[j].astype(F32)[dim][None, :]
            q_rot, k_rot = _qk_prep(proj, cos_t, sin_t, gq, gk, L, tm)
            mix = _gqa_attention(q_rot.reshape(B, L, -1), k_rot.reshape(B, L, -1), proj.reshape(B, L, -1),
                                 B, L, min(128, L)).reshape(N, MIX_WIDTH)
            wo = jnp.concatenate([wo[:MIX_WIDTH][out_rows_b], wo[MIX_WIDTH:]], axis=0)
            qm_blk = (MIX_WIDTH + 2 * KV_WIDTH) // MEM_WIDTH
        else:
            proj = _matmul(xf, w_in_c[j].astype(BF16), BF16, tm, HYENA_WIDTH + MEM_WIDTH)
            mix = _hyena_mixer(proj.reshape(B, L, -1), j, hy, B, L).reshape(N, MIX_WIDTH)
            qm_blk = HYENA_WIDTH // MEM_WIDTH
        kmem = kv_all[:, :, i, 0, :]
        vmem = kv_all[:, :, i, 1, :]
        rw = jnp.pad(router_w[i].astype(F32), ((0, 0), (0, LANES - N_EXPERTS)))
        rw_hi = rw.astype(BF16)
        rw = jnp.concatenate([rw_hi, (rw - rw_hi.astype(F32)).astype(BF16)], axis=1)
        rb = jnp.pad(router_b[i].astype(F32), (0, LANES - N_EXPERTS), constant_values=NEG_BIG)[None, :]
        x1, x1b, gates, route, counts = _post_mixer(
            xf, mix, proj, qm_blk, kmem, vmem, wo.astype(BF16),
            ln1_g[i].astype(F32)[None, :], ln1_b[i].astype(F32)[None, :], rw, rb, L, tm)
        xf = _moe(x1, x1b, gates, route, counts, w1_all, b1_all, w2_all, b2_all,
                  ln2_g[i].astype(F32)[None, :], ln2_b[i].astype(F32)[None, :], i, tm)
    return xf.reshape(B, L, D).astype(x.dtype)
```

```python
import functools
import math

import numpy as np
import jax
import jax.numpy as jnp
from jax import lax
from jax.experimental import pallas as pl
from jax.experimental.pallas import tpu as pltpu

F32 = jnp.float32
BF16 = jnp.bfloat16

D_MODEL = 1024
DEPTH = 4
GRID_W = 64
N_MIXERS = 3
HEAD_DIM = 64
MIX_WIDTH = 768
MIX_HEADS = MIX_WIDTH // HEAD_DIM
MEM_HEADS = 4
MEM_WIDTH = MEM_HEADS * HEAD_DIM
NA_KH = 8
NA_KW = 16
NA_ROW_UNROLL = 8
KV_HEADS = 4
KV_WIDTH = KV_HEADS * HEAD_DIM
GQA_GROUP = MIX_HEADS // KV_HEADS
ROPE_THETA = 10000.0
QK_NORM_EPS = 1e-6
HYENA_ORDER = 2
HYENA_BANDS = 16
HYENA_EMB = 2 * HYENA_BANDS + 1
HYENA_FFN = 64
HYENA_WIDTH = (HYENA_ORDER + 1) * MIX_WIDTH
HYENA_DECAY_TARGET = 1e-2
HYENA_FAST_DECAY = 0.3
HYENA_SLOW_DECAY = 1.5
N_EXPERTS = 32
TOP_K = 4
D_EXPERT = D_MODEL
SWIGLU_LIMIT = 7.0
SWIGLU_ALPHA = 1.702
DEEPNORM_ALPHA = (2 * DEPTH) ** 0.25
LN_EPS = 1e-5

LANES = 128
SUBLANES = 8
VMEM_LIMIT = 56 * 1024 * 1024
NEG_BIG = -1e30
ATT_SCALE = HEAD_DIM ** -0.5


def _cparams(sem):
    return pltpu.CompilerParams(dimension_semantics=sem, vmem_limit_bytes=VMEM_LIMIT)


def _layer_norm(x, g, b):
    mu = jnp.mean(x, -1, keepdims=True)
    xc = x - mu
    var = jnp.mean(xc * xc, -1, keepdims=True)
    return xc * lax.rsqrt(var + LN_EPS) * g + b


def _mm_kernel(x_ref, w_ref, o_ref):
    o_ref[...] = jnp.dot(x_ref[...].astype(BF16), w_ref[...],
                         preferred_element_type=F32).astype(o_ref.dtype)


def _matmul(x, w, out_dtype, tm, tn):
    M, K = x.shape
    N = w.shape[1]
    tm = min(tm, M)
    tn = min(tn, N)
    assert M % tm == 0 and N % tn == 0
    return pl.pallas_call(
        _mm_kernel,
        grid=(M // tm, N // tn),
        in_specs=[pl.BlockSpec((tm, K), lambda i, j: (i, 0)),
                  pl.BlockSpec((K, tn), lambda i, j: (0, j))],
        out_specs=pl.BlockSpec((tm, tn), lambda i, j: (i, j)),
        out_shape=jax.ShapeDtypeStruct((M, N), out_dtype),
        compiler_params=_cparams(("parallel", "arbitrary")),
        name="matmul",
    )(x, w)


def _na_bias_table(rpb, rows):
    del rows
    col = np.arange(GRID_W)
    c0 = np.clip(col - NA_KW // 2, 0, GRID_W - NA_KW)
    cc = np.arange(GRID_W)
    valid = (cc[None, :] >= c0[:, None]) & (cc[None, :] < c0[:, None] + NA_KW)
    coff = np.clip(cc[None, :] - col[:, None] + NA_KW - 1, 0, 2 * NA_KW - 2)
    d = np.arange(NA_KH)
    i = np.arange(NA_KH)
    roff = i[None, :] - d[:, None] + NA_KH - 1
    t = rpb.astype(F32)[:, roff]
    t = t[:, :, :, coff]
    t = t.transpose(0, 1, 3, 2, 4)
    t = jnp.where(valid[None, None, :, None, :], t, NEG_BIG)
    return t.reshape(rpb.shape[0], NA_KH, GRID_W, NA_KH * GRID_W)


def _na_kernel(q_ref, k_ref, v_ref, bias_ref, o_ref, *, rows):
    lane = lax.broadcasted_iota(jnp.int32, (GRID_W, LANES), 1)
    first = lane < HEAD_DIM
    band = NA_KH * GRID_W

    def body(r, carry):
        r0 = jnp.clip(r - NA_KH // 2, 0, rows - NA_KH)
        d = r - r0
        qoff = pl.multiple_of(r * GRID_W, GRID_W)
        koff = pl.multiple_of(r0 * GRID_W, GRID_W)
        q = q_ref[pl.ds(qoff, GRID_W), :]
        kb = k_ref[pl.ds(koff, band), :]
        vb = v_ref[pl.ds(koff, band), :]
        zero = jnp.zeros_like(q)
        q2 = jnp.concatenate([jnp.where(first, q, zero), jnp.where(first, zero, q)], axis=0)
        s = lax.dot_general(q2, kb, (((1,), (1,)), ((), ())), preferred_element_type=F32)
        bias = jnp.concatenate([bias_ref[0, d], bias_ref[1, d]], axis=0)
        s = s * ATT_SCALE + bias
        m = jnp.max(s, -1, keepdims=True)
        e = jnp.exp(s - m)
        l = jnp.sum(e, -1, keepdims=True)
        o2 = jnp.dot(e.astype(BF16), vb, preferred_element_type=F32) / l
        o = jnp.where(first, o2[:GRID_W], o2[GRID_W:])
        o_ref[pl.ds(qoff, GRID_W), :] = o.astype(o_ref.dtype)
        return carry

    lax.fori_loop(0, rows, body, 0, unroll=NA_ROW_UNROLL)


def _na_attention(proj, bias_tab, B, L):
    rows = L // GRID_W
    assert rows >= NA_KH
    npair = MIX_WIDTH // LANES
    return pl.pallas_call(
        functools.partial(_na_kernel, rows=rows),
        grid=(npair, B),
        in_specs=[pl.BlockSpec((None, L, LANES), lambda p, b: (b, 0, p)),
                  pl.BlockSpec((None, L, LANES), lambda p, b: (b, 0, npair + p)),
                  pl.BlockSpec((None, L, LANES), lambda p, b: (b, 0, 2 * npair + p)),
                  pl.BlockSpec((2, NA_KH, GRID_W, NA_KH * GRID_W), lambda p, b: (p, 0, 0, 0))],
        out_specs=pl.BlockSpec((None, L, LANES), lambda p, b: (b, 0, p)),
        out_shape=jax.ShapeDtypeStruct((B, L, MIX_WIDTH), BF16),
        compiler_params=_cparams(("parallel", "parallel")),
        name="na_attention",
    )(proj, proj, proj, bias_tab)


def _pair_lane_dims():
    l = np.arange(LANES)
    return 2 * (l % 32) + (l >= HEAD_DIM).astype(np.int64), ((l % HEAD_DIM) >= 32).astype(np.int64)


def _gqa_column_perms():
    dim, which = _pair_lane_dims()
    q_cols = []
    out_rows = []
    for P in range(KV_HEADS // 2):
        for g in range(GQA_GROUP):
            heads = ((2 * P) * GQA_GROUP + g, (2 * P + 1) * GQA_GROUP + g)
            q_cols.append(np.array([heads[w] * HEAD_DIM + dd for dd, w in zip(dim, which)]))
            out_rows.append(np.concatenate([heads[0] * HEAD_DIM + np.arange(HEAD_DIM),
                                            heads[1] * HEAD_DIM + np.arange(HEAD_DIM)]))
    k_cols = []
    for P in range(KV_HEADS // 2):
        heads = (2 * P, 2 * P + 1)
        k_cols.append(np.array([heads[w] * HEAD_DIM + dd for dd, w in zip(dim, which)]))
    return np.concatenate(q_cols), np.concatenate(k_cols), np.concatenate(out_rows)


def _rope_lane_tables(L):
    pos = jnp.arange(L)
    row = (pos // GRID_W).astype(F32)
    col = (pos % GRID_W).astype(F32)
    axis_dim = HEAD_DIM // 2
    inv_freq = ROPE_THETA ** (-jnp.arange(0, axis_dim, 2, dtype=F32) / axis_dim)
    ang = jnp.concatenate([row[:, None] * inv_freq, col[:, None] * inv_freq], -1)
    c = jnp.cos(ang)
    s = jnp.sin(ang)
    return jnp.concatenate([c, c, c, c], -1), jnp.concatenate([-s, -s, s, s], -1)


def _qk_prep_kernel(q_ref, k_ref, cos_ref, sin_ref, gq_ref, gk_ref, qo_ref, ko_ref):
    tm = q_ref.shape[0]
    lane = lax.broadcasted_iota(jnp.int32, (tm, LANES), 1)
    head_a = (lane % HEAD_DIM) < 32
    c = cos_ref[...]
    s = sin_ref[...]

    def norm_rope(x, g):
        x = x.astype(F32)
        x2 = x * x
        tot = jnp.sum(x2, -1, keepdims=True)
        sa = jnp.sum(jnp.where(head_a, x2, 0.0), -1, keepdims=True)
        inv = jnp.where(head_a, lax.rsqrt(sa * (1.0 / HEAD_DIM) + QK_NORM_EPS),
                        lax.rsqrt((tot - sa) * (1.0 / HEAD_DIM) + QK_NORM_EPS))
        xn = x * inv * g
        return xn * c + pltpu.roll(xn, HEAD_DIM, axis=1) * s

    for j in range(q_ref.shape[1] // LANES):
        sl = slice(j * LANES, (j + 1) * LANES)
        qo_ref[:, sl] = (norm_rope(q_ref[:, sl], gq_ref[...]) * ATT_SCALE).astype(qo_ref.dtype)
    for j in range(k_ref.shape[1] // LANES):
        sl = slice(j * LANES, (j + 1) * LANES)
        ko_ref[:, sl] = norm_rope(k_ref[:, sl], gk_ref[...]).astype(ko_ref.dtype)


def _qk_prep(proj, cos_t, sin_t, gq, gk, L, tm):
    N = proj.shape[0]
    nl = L // tm
    return pl.pallas_call(
        _qk_prep_kernel,
        grid=(N // tm,),
        in_specs=[pl.BlockSpec((tm, MIX_WIDTH), lambda i: (i, 0)),
                  pl.BlockSpec((tm, KV_WIDTH), lambda i: (i, MIX_WIDTH // KV_WIDTH)),
                  pl.BlockSpec((tm, LANES), lambda i: (i % nl, 0)),
                  pl.BlockSpec((tm, LANES), lambda i: (i % nl, 0)),
                  pl.BlockSpec((1, LANES), lambda i: (0, 0)),
                  pl.BlockSpec((1, LANES), lambda i: (0, 0))],
        out_specs=[pl.BlockSpec((tm, MIX_WIDTH), lambda i: (i, 0)),
                   pl.BlockSpec((tm, KV_WIDTH), lambda i: (i, 0))],
        out_shape=[jax.ShapeDtypeStruct((N, MIX_WIDTH), BF16),
                   jax.ShapeDtypeStruct((N, KV_WIDTH), BF16)],
        compiler_params=_cparams(("parallel",)),
        name="qk_prep",
    )(proj, proj, cos_t, sin_t, gq, gk)


GQA_KV_CHUNK = 1024


def _gqa_kernel(q_ref, k_ref, v_ref, o_ref, q6_ref, s_ref, m_ref, l_ref, acc_ref):
    tq = q_ref.shape[0]
    tk = s_ref.shape[2]
    nk = k_ref.shape[0] // tk
    lane = lax.broadcasted_iota(jnp.int32, (tq, LANES), 1)
    head_a = (lane % HEAD_DIM) < 32
    q = q_ref[...]
    for g in range(GQA_GROUP):
        qs = q[:, g * LANES:(g + 1) * LANES]
        zero = jnp.zeros_like(qs)
        q6_ref[(2 * g) * tq:(2 * g + 1) * tq, :] = jnp.where(head_a, qs, zero)
        q6_ref[(2 * g + 1) * tq:(2 * g + 2) * tq, :] = jnp.where(head_a, zero, qs)
    m_ref[...] = jnp.full(m_ref.shape, -jnp.inf, F32)
    l_ref[...] = jnp.zeros(l_ref.shape, F32)
    acc_ref[...] = jnp.zeros(acc_ref.shape, F32)

    def scores(j):
        return lax.dot_general(q6_ref[...], k_ref[j * tk:(j + 1) * tk, :], (((1,), (1,)), ((), ())),
                               preferred_element_type=F32)

    s_ref[0] = scores(0)
    for j in range(nk):
        if j + 1 < nk:
            s_ref[(j + 1) % 2] = scores(j + 1)
        s = s_ref[j % 2]
        m_old = m_ref[...]
        m_new = jnp.maximum(m_old, jnp.max(s, -1, keepdims=True))
        alpha = jnp.exp(m_old - m_new)
        p = jnp.exp(s - jnp.concatenate([m_new] * (tk // LANES), axis=1))
        l_ref[...] = alpha * l_ref[...] + jnp.sum(p, -1, keepdims=True)
        acc_ref[...] = alpha * acc_ref[...] + jnp.dot(p.astype(BF16), v_ref[j * tk:(j + 1) * tk, :],
                                                      preferred_element_type=F32)
        m_ref[...] = m_new
    o6 = acc_ref[...] / l_ref[...]
    for g in range(GQA_GROUP):
        oa = o6[(2 * g) * tq:(2 * g + 1) * tq]
        ob = o6[(2 * g + 1) * tq:(2 * g + 2) * tq]
        o_ref[:, g * LANES:(g + 1) * LANES] = jnp.where(lane < HEAD_DIM, oa, ob).astype(o_ref.dtype)


def _gqa_attention(q_rot, k_rot, proj, B, L, tq):
    slab_w = GQA_GROUP * LANES
    v_blk0 = (MIX_WIDTH + KV_WIDTH) // LANES
    rows6 = 2 * GQA_GROUP * tq
    tk = min(GQA_KV_CHUNK, L)
    return pl.pallas_call(
        _gqa_kernel,
        grid=(B, KV_HEADS // 2, L // tq),
        in_specs=[pl.BlockSpec((None, tq, slab_w), lambda b, p, i: (b, i, p)),
                  pl.BlockSpec((None, L, LANES), lambda b, p, i: (b, 0, p)),
                  pl.BlockSpec((None, L, LANES), lambda b, p, i: (b, 0, v_blk0 + p))],
        out_specs=pl.BlockSpec((None, tq, slab_w), lambda b, p, i: (b, i, p)),
        out_shape=jax.ShapeDtypeStruct((B, L, MIX_WIDTH), BF16),
        scratch_shapes=[pltpu.VMEM((rows6, LANES), BF16),
                        pltpu.VMEM((2, rows6, tk), F32),
                        pltpu.VMEM((rows6, LANES), F32),
                        pltpu.VMEM((rows6, LANES), F32),
                        pltpu.VMEM((rows6, LANES), F32)],
        compiler_params=_cparams(("parallel", "parallel", "arbitrary")),
        name="gqa_attention",
    )(q_rot, k_rot, proj)


def _hyena_features(L):
    pos = jnp.arange(L, dtype=F32)
    t = pos / max(L - 1, 1)
    w = 2.0 * math.pi * pos / L
    f = jnp.linspace(1e-4, HYENA_BANDS - 1, HYENA_BANDS, dtype=F32)
    wf = w[:, None] * f[None, :]
    z = jnp.concatenate([t[:, None], jnp.cos(wf), -jnp.sin(wf)], -1)
    return jnp.pad(z, ((0, 0), (0, LANES - HYENA_EMB))), t[:, None]


def _filter_kernel(z_ref, t_ref, w1_ref, b1_ref, w2_ref, b2_ref, w3_ref, b3_ref, w4f_ref, w4b_ref,
                   b4f_ref, b4b_ref, fr_ref, delta_ref, hsum_ref, hdiff_ref):
    hp = lax.Precision.HIGHEST
    fr = fr_ref[...]
    h = jnp.sin(fr * (jnp.dot(z_ref[...], w1_ref[...], precision=hp, preferred_element_type=F32) + b1_ref[...]))
    h = jnp.sin(fr * (jnp.dot(h, w2_ref[...], precision=hp, preferred_element_type=F32) + b2_ref[...]))
    h = jnp.sin(fr * (jnp.dot(h, w3_ref[...], precision=hp, preferred_element_type=F32) + b3_ref[...]))
    hf = jnp.dot(h, w4f_ref[...], precision=hp, preferred_element_type=F32) + b4f_ref[...]
    hb = jnp.dot(h, w4b_ref[...], precision=hp, preferred_element_type=F32) + b4b_ref[...]
    dec = jnp.exp(-t_ref[...] * delta_ref[...])
    hf = hf * dec
    hb = hb * dec
    norm = jnp.sum(jnp.abs(hf), 0, keepdims=True) + jnp.sum(jnp.abs(hb), 0, keepdims=True)
    inv = 1.0 / norm
    hf = hf * inv
    row = lax.broadcasted_iota(jnp.int32, hb.shape, 0)
    hb = jnp.where(row == 0, 0.0, hb * inv)
    hsum_ref[...] = hf + hb
    hdiff_ref[...] = hf - hb


def _hyena_filters(zf, t, w1, b1, w2, b2, w3, b3, w4, b4, freq, L, tc):
    OC = HYENA_ORDER * MIX_WIDTH
    max_decay = math.log(HYENA_DECAY_TARGET) / HYENA_FAST_DECAY
    min_decay = math.log(HYENA_DECAY_TARGET) / HYENA_SLOW_DECAY
    deltas = jnp.abs(jnp.linspace(min_decay, max_decay, MIX_WIDTH, dtype=F32))
    deltas = jnp.tile(deltas, HYENA_ORDER)[None, :]
    w1p = jnp.pad(w1.astype(F32), ((0, LANES - HYENA_EMB), (0, 0)))
    row = lambda a: a.astype(F32)[None, :]
    full = lambda shape: pl.BlockSpec(shape, lambda j: (0,) * len(shape))
    nb = OC // tc
    return pl.pallas_call(
        _filter_kernel,
        grid=(nb,),
        in_specs=[full((L, LANES)), full((L, 1)),
                  full((LANES, HYENA_FFN)), full((1, HYENA_FFN)),
                  full((HYENA_FFN, HYENA_FFN)), full((1, HYENA_FFN)),
                  full((HYENA_FFN, HYENA_FFN)), full((1, HYENA_FFN)),
                  pl.BlockSpec((HYENA_FFN, tc), lambda j: (0, j)),
                  pl.BlockSpec((HYENA_FFN, tc), lambda j: (0, nb + j)),
                  pl.BlockSpec((1, tc), lambda j: (0, j)),
                  pl.BlockSpec((1, tc), lambda j: (0, nb + j)),
                  full((1, HYENA_FFN)),
                  pl.BlockSpec((1, tc), lambda j: (0, j))],
        out_specs=[pl.BlockSpec((L, tc), lambda j: (0, j)),
                   pl.BlockSpec((L, tc), lambda j: (0, j))],
        out_shape=[jax.ShapeDtypeStruct((L, OC), F32), jax.ShapeDtypeStruct((L, OC), F32)],
        compiler_params=_cparams(("parallel",)),
        name="hyena_filters",
    )(zf, t, w1p, row(b1), w2.astype(F32), row(b2), w3.astype(F32), row(b3), w4.astype(F32), w4.astype(F32),
      row(b4), row(b4), row(freq), deltas)


def _dft_matrices(L):
    n = 2 * L
    k = jnp.arange(L, dtype=jnp.int32)
    kb = 64 if L % 64 == 0 else 1
    s = k[None, :]

    def table(mult):
        ang = ((mult[:, None] * s) % n).astype(F32) * (2.0 * math.pi / n)
        return jnp.cos(ang), jnp.sin(ang)

    ca, sa = table(jnp.arange(L // kb, dtype=jnp.int32) * kb)
    cb, sb = table(jnp.arange(kb, dtype=jnp.int32))
    fc = (ca[:, None, :] * cb[None, :, :] - sa[:, None, :] * sb[None, :, :]).reshape(L, L)
    sn = (sa[:, None, :] * cb[None, :, :] + ca[:, None, :] * sb[None, :, :]).reshape(L, L)
    alt = jnp.where(k % 2 == 0, 1.0, -1.0).astype(F32)
    fs = jnp.where(k[:, None] == 0, alt[None, :], sn)
    fst = jnp.where(k[None, :] == 0, alt[:, None], sn)
    return fc.astype(BF16), fs.astype(BF16), fst.astype(BF16)


def _short_conv_kernel(u_ref, w_ref, b_ref, o_ref):
    L = u_ref.shape[0]
    u = u_ref[...].astype(F32)
    row = lax.broadcasted_iota(jnp.int32, u.shape, 0)
    prev = jnp.where(row == 0, 0.0, pltpu.roll(u, 1, axis=0))
    nxt = jnp.where(row == L - 1, 0.0, pltpu.roll(u, L - 1, axis=0))
    w = w_ref[...]
    o_ref[...] = (prev * w[0:1] + u * w[1:2] + nxt * w[2:3] + b_ref[...]).astype(o_ref.dtype)


def _short_conv(proj, conv_w, conv_b, B, L, tc):
    return pl.pallas_call(
        _short_conv_kernel,
        grid=(B, HYENA_WIDTH // tc),
        in_specs=[pl.BlockSpec((None, L, tc), lambda b, j: (b, 0, j)),
                  pl.BlockSpec((3, tc), lambda b, j: (0, j)),
                  pl.BlockSpec((1, tc), lambda b, j: (0, j))],
        out_specs=pl.BlockSpec((None, L, tc), lambda b, j: (b, 0, j)),
        out_shape=jax.ShapeDtypeStruct((B, L, HYENA_WIDTH), BF16),
        compiler_params=_cparams(("parallel", "parallel")),
        name="short_conv",
    )(proj, conv_w.astype(F32), conv_b.astype(F32)[None, :])


def _hy_fwd_kernel(fc_ref, fs_ref, z_ref, pa_ref, qa_ref, qb_ref, sb_ref, ya_ref, yb_ref):
    z = z_ref[...]
    a = jnp.dot(fc_ref[...], z, preferred_element_type=F32)
    b = jnp.dot(fs_ref[...], z, preferred_element_type=F32)
    ya_ref[...] = (a * pa_ref[...] - b * qa_ref[...]).astype(ya_ref.dtype)
    yb_ref[...] = (a * qb_ref[...] + b * sb_ref[...]).astype(yb_ref.dtype)


def _hy_inv_kernel(fc_ref, fst_ref, ya_ref, yb_ref, z_ref, xg_ref, bias_ref, o_ref):
    y = jnp.dot(fc_ref[...], ya_ref[...], preferred_element_type=F32)
    y = y + jnp.dot(fst_ref[...], yb_ref[...], preferred_element_type=F32)
    z = z_ref[...].astype(F32)
    o_ref[...] = (xg_ref[...].astype(F32) * (y + bias_ref[...] * z)).astype(o_ref.dtype)


def _hyena_long_conv_gate(uc, z_arr, z_blk, xg_blk, fc, fs, fst, coef, bias, B, L, tm_f, tm_i):
    C = MIX_WIDTH
    pa, qa, qb, sb = coef
    ya, yb = pl.pallas_call(
        _hy_fwd_kernel,
        grid=(B, L // tm_f),
        in_specs=[pl.BlockSpec((tm_f, L), lambda b, i: (i, 0)),
                  pl.BlockSpec((tm_f, L), lambda b, i: (i, 0)),
                  pl.BlockSpec((None, L, C), lambda b, i: (b, 0, z_blk)),
                  pl.BlockSpec((tm_f, C), lambda b, i: (i, 0)),
                  pl.BlockSpec((tm_f, C), lambda b, i: (i, 0)),
                  pl.BlockSpec((tm_f, C), lambda b, i: (i, 0)),
                  pl.BlockSpec((tm_f, C), lambda b, i: (i, 0))],
        out_specs=[pl.BlockSpec((None, tm_f, C), lambda b, i: (b, i, 0)),
                   pl.BlockSpec((None, tm_f, C), lambda b, i: (b, i, 0))],
        out_shape=[jax.ShapeDtypeStruct((B, L, C), BF16), jax.ShapeDtypeStruct((B, L, C), BF16)],
        compiler_params=_cparams(("parallel", "arbitrary")),
        name="hyena_dft_fwd",
    )(fc, fs, z_arr, pa, qa, qb, sb)
    return pl.pallas_call(
        _hy_inv_kernel,
        grid=(B, L // tm_i),
        in_specs=[pl.BlockSpec((tm_i, L), lambda b, i: (i, 0)),
                  pl.BlockSpec((tm_i, L), lambda b, i: (i, 0)),
                  pl.BlockSpec((None, L, C), lambda b, i: (b, 0, 0)),
                  pl.BlockSpec((None, L, C), lambda b, i: (b, 0, 0)),
                  pl.BlockSpec((None, tm_i, C), lambda b, i: (b, i, z_blk)),
                  pl.BlockSpec((None, tm_i, C), lambda b, i: (b, i, xg_blk)),
                  pl.BlockSpec((1, C), lambda b, i: (0, 0))],
        out_specs=pl.BlockSpec((None, tm_i, C), lambda b, i: (b, i, 0)),
        out_shape=jax.ShapeDtypeStruct((B, L, C), BF16),
        compiler_params=_cparams(("parallel", "arbitrary")),
        name="hyena_dft_inv",
    )(fc, fst, ya, yb, z_arr, uc, bias)


def _hyena_mixer(proj, j, p, B, L):
    C = MIX_WIDTH
    OC = HYENA_ORDER * C
    n = 2 * L
    zf, t = _hyena_features(L)
    hsum, hdiff = _hyena_filters(zf, t, p['filt_w1'][j], p['filt_b1'][j], p['filt_w2'][j], p['filt_b2'][j],
                                 p['filt_w3'][j], p['filt_b3'][j], p['filt_w4'][j], p['filt_b4'][j],
                                 p['filt_freq'][j], L, 256)
    fc, fs, fst = _dft_matrices(L)
    parts = []
    for h in (hsum, hdiff):
        hi = h.astype(BF16)
        parts += [hi, (h - hi.astype(F32)).astype(BF16)]
    hcat = jnp.concatenate(parts, axis=1)
    spec = _matmul(jnp.concatenate([fc, fs], axis=0), hcat, F32, 512, 512)
    gc = spec[:L, 0:OC] + spec[:L, OC:2 * OC]
    nyq = spec[L:L + 1, 0:OC] + spec[L:L + 1, OC:2 * OC]
    gs = spec[L:, 2 * OC:3 * OC] + spec[L:, 3 * OC:4 * OC]
    k0 = (jnp.arange(L) == 0)[:, None]
    ca = jnp.where(k0, 1.0 / n, 2.0 / n).astype(F32)
    pa = ca * gc
    qa = ca * jnp.where(k0, 0.0, gs)
    qb = qa
    sb = ca * jnp.where(k0, nyq, gc)
    uc = _short_conv(proj, p['conv_w_c'][j], p['conv_b_c'][j], B, L, 256)
    bias = p['long_bias_c'][j].astype(F32)
    tm_f = min(512, L)
    tm_i = min(256, L)
    z = uc
    z_blk = 0
    for o in range(HYENA_ORDER):
        sl = slice(o * C, (o + 1) * C)
        coef = (pa[:, sl], qa[:, sl], qb[:, sl], sb[:, sl])
        z = _hyena_long_conv_gate(uc, z, z_blk, o + 1, fc, fs, fst, coef, bias[o][None, :], B, L, tm_f, tm_i)
        z_blk = 0
    return z


def _pack_halves(x):
    c = x.shape[1] // 2
    bits = lax.bitcast_convert_type(x.astype(BF16).astype(F32), jnp.uint32)
    return (bits[:, :c] & jnp.uint32(0xFFFF0000)) | (bits[:, c:] >> 16)


def _unpack_halves(w):
    hi = lax.bitcast_convert_type(w & jnp.uint32(0xFFFF0000), F32)
    lo = lax.bitcast_convert_type(w << 16, F32)
    return hi, lo


def _post_kernel(x_ref, mix_ref, qm_ref, km_ref, vm_ref, wo_ref, g_ref, b_ref, rw_ref, rb_ref,
                 xo_ref, xb_ref, gate_ref, route_ref, count_ref):
    tm = x_ref.shape[0]
    lane = lax.broadcasted_iota(jnp.int32, (tm, LANES), 1)
    first = lane < HEAD_DIM
    outs = []
    for p in range(MEM_WIDTH // LANES):
        sl = slice(p * LANES, (p + 1) * LANES)
        qs = qm_ref[:, sl]
        zero = jnp.zeros_like(qs)
        q2 = jnp.concatenate([jnp.where(first, qs, zero), jnp.where(first, zero, qs)], axis=0)
        s = lax.dot_general(q2, km_ref[:, sl], (((1,), (1,)), ((), ())), preferred_element_type=F32) * ATT_SCALE
        m = jnp.max(s, -1, keepdims=True)
        e = jnp.exp(s - m)
        l = jnp.sum(e, -1, keepdims=True)
        o2 = jnp.dot(e.astype(BF16), vm_ref[:, sl], preferred_element_type=F32) / l
        outs.append(jnp.where(first, o2[:tm], o2[tm:]).astype(BF16))
    mem_out = jnp.concatenate(outs, axis=1)
    acc = jnp.dot(mix_ref[...], wo_ref[:MIX_WIDTH, :], preferred_element_type=F32)
    acc = acc + jnp.dot(mem_out, wo_ref[MIX_WIDTH:, :], preferred_element_type=F32)
    x1 = _layer_norm(DEEPNORM_ALPHA * x_ref[...] + acc, g_ref[...], b_ref[...])
    xo_ref[...] = x1
    x_hi = x1.astype(BF16)
    xb_ref[...] = x_hi
    x_lo = (x1 - x_hi.astype(F32)).astype(BF16)
    prod = jnp.dot(x_hi, rw_ref[...], preferred_element_type=F32)
    logits = (prod[:, :LANES] + prod[:, LANES:]
              + jnp.dot(x_lo, rw_ref[:, :LANES], preferred_element_type=F32) + rb_ref[...])
    lane_f = lane.astype(F32)
    work = logits
    chosen = jnp.zeros_like(logits)
    gates = jnp.zeros_like(logits)
    denom = jnp.zeros((tm, 1), F32)
    sels, idxs = [], []
    m0 = None
    for k in range(TOP_K):
        m = jnp.max(work, -1, keepdims=True)
        idx = jnp.min(jnp.where(work == m, lane_f, float(LANES)), -1, keepdims=True)
        sel = lane_f == idx
        if k == 0:
            m0 = m
        ek = jnp.exp(m - m0)
        gates = jnp.where(lane == k, ek, gates)
        chosen = chosen + jnp.where(sel, 1.0, 0.0)
        denom = denom + ek
        work = jnp.where(sel, -jnp.inf, work)
        sels.append(sel)
        idxs.append(idx)
    gate_ref[...] = gates / denom

    r_i = lax.broadcasted_iota(jnp.int32, (tm, tm), 0)
    c_i = lax.broadcasted_iota(jnp.int32, (tm, tm), 1)
    before = jnp.where(r_i > c_i, 1.0, 0.0).astype(BF16)
    prefix = jnp.dot(before, chosen.astype(BF16), preferred_element_type=F32)
    route = jnp.zeros((tm, LANES), jnp.int32)
    for k in range(TOP_K):
        rank = jnp.sum(jnp.where(sels[k], prefix, 0.0), -1, keepdims=True)
        route = jnp.where(lane == k, rank.astype(jnp.int32), route)
        route = jnp.where(lane == TOP_K + k, idxs[k].astype(jnp.int32), route)
    route_ref[...] = route
    count_ref[...] = jnp.broadcast_to(jnp.sum(chosen, 0, keepdims=True), count_ref.shape)


def _post_mixer(x, mix, proj, qm_blk, kmem, vmem, wo, g, b, rw, rb, L, tm):
    N, D = x.shape
    nl = L // tm
    CAT = MIX_WIDTH + MEM_WIDTH
    return pl.pallas_call(
        _post_kernel,
        grid=(N // tm,),
        in_specs=[pl.BlockSpec((tm, D), lambda i: (i, 0)),
                  pl.BlockSpec((tm, MIX_WIDTH), lambda i: (i, 0)),
                  pl.BlockSpec((tm, MEM_WIDTH), lambda i: (i, qm_blk)),
                  pl.BlockSpec((None, kmem.shape[1], MEM_WIDTH), lambda i: (i // nl, 0, 0)),
                  pl.BlockSpec((None, vmem.shape[1], MEM_WIDTH), lambda i: (i // nl, 0, 0)),
                  pl.BlockSpec((CAT, D), lambda i: (0, 0)),
                  pl.BlockSpec((1, D), lambda i: (0, 0)),
                  pl.BlockSpec((1, D), lambda i: (0, 0)),
                  pl.BlockSpec((D, 2 * LANES), lambda i: (0, 0)),
                  pl.BlockSpec((1, LANES), lambda i: (0, 0))],
        out_specs=[pl.BlockSpec((tm, D), lambda i: (i, 0)),
                   pl.BlockSpec((tm, D), lambda i: (i, 0)),
                   pl.BlockSpec((tm, LANES), lambda i: (i, 0)),
                   pl.BlockSpec((tm, LANES), lambda i: (i, 0)),
                   pl.BlockSpec((SUBLANES, LANES), lambda i: (i, 0))],
        out_shape=[jax.ShapeDtypeStruct((N, D), F32), jax.ShapeDtypeStruct((N, D), BF16),
                   jax.ShapeDtypeStruct((N, LANES), F32), jax.ShapeDtypeStruct((N, LANES), jnp.int32),
                   jax.ShapeDtypeStruct((N // tm * SUBLANES, LANES), F32)],
        compiler_params=_cparams(("parallel",)),
        name="post_mixer",
    )(x, mix, proj, kmem, vmem, wo, g, b, rw, rb)


EXPERT_TILE = 512
CHUNK = SUBLANES
BIG_CHUNK = 4 * CHUNK


def _tile_buffer_rows(tm):
    return tm * TOP_K + CHUNK * N_EXPERTS


def _route_tables(route, tile_counts, tm, max_tiles):
    N = route.shape[0]
    nt = N // tm
    cnt = tile_counts.reshape(nt, SUBLANES, LANES)[:, 0, :N_EXPERTS].astype(jnp.int32)
    rc = ((cnt + CHUNK - 1) // CHUNK) * CHUNK
    base = jnp.cumsum(rc, axis=0) - rc
    padded = ((jnp.sum(rc, axis=0) + EXPERT_TILE - 1) // EXPERT_TILE) * EXPERT_TILE
    ends = jnp.cumsum(padded)
    offs = ends - padded
    local = jnp.cumsum(rc, axis=1) - rc
    hbm_start = (offs[None, :] + base).reshape(-1).astype(jnp.int32)
    n_big = rc // BIG_CHUNK
    n_small = (rc % BIG_CHUNK) // CHUNK
    n_chunks = jnp.concatenate([n_big.reshape(-1), n_small.reshape(-1)]).astype(jnp.int32)
    total_chunks = jnp.concatenate([jnp.sum(n_big, axis=1), jnp.sum(n_small, axis=1)]).astype(jnp.int32)
    eid = route[:, TOP_K:2 * TOP_K]
    local_tok = jnp.repeat(local, tm, axis=0)[:, None, :]
    hit = eid[:, :, None] == jnp.arange(N_EXPERTS, dtype=jnp.int32)[None, None, :]
    col = jnp.sum(jnp.where(hit, local_tok, 0), axis=2) + route[:, :TOP_K]
    col_lanes = jnp.pad(col.astype(F32), ((0, 0), (0, LANES - TOP_K)), constant_values=-1.0)
    col_rows = jnp.pad(col.reshape(nt, tm, TOP_K).transpose(0, 2, 1), ((0, 0), (0, SUBLANES - TOP_K), (0, 0)),
                       constant_values=-1).reshape(nt * SUBLANES, tm).astype(jnp.int32)
    tile_ends = ends // EXPERT_TILE
    n_tiles = tile_ends[-1]
    t = jnp.arange(max_tiles, dtype=jnp.int32)
    t_live = jnp.minimum(t, n_tiles - 1)
    tile_expert = jnp.sum((t_live[:, None] >= tile_ends[None, :]).astype(jnp.int32), axis=1)
    tile_expert = jnp.minimum(tile_expert, N_EXPERTS - 1)
    zero_start = (ends - EXPERT_TILE).astype(jnp.int32)
    zero_valid = (padded > 0).astype(jnp.int32)
    chunks = (hbm_start, n_chunks, local.reshape(-1).astype(jnp.int32), total_chunks)
    return chunks, col_lanes, col_rows, tile_expert, n_tiles.reshape(1).astype(jnp.int32), zero_start, zero_valid


def _chunk_copy(hbm_ref, buf_ref, h0, b0, size, sem, to_hbm):
    aligned = lambda v: v if isinstance(v, int) else pl.multiple_of(v, CHUNK)
    h = hbm_ref.at[pl.ds(aligned(h0), size)]
    b = buf_ref.at[pl.ds(aligned(b0), size)]
    return pltpu.make_async_copy(b, h, sem) if to_hbm else pltpu.make_async_copy(h, b, sem)


def _chunk_copies(i, hs_ref, nch_ref, lo_ref, hbm_ref, buf_ref, sem, to_hbm):
    n_pairs = pl.num_programs(0) * N_EXPERTS
    for e in range(N_EXPERTS):
        hs = hs_ref[i * N_EXPERTS + e]
        lo = lo_ref[i * N_EXPERTS + e]
        n_big = nch_ref[i * N_EXPERTS + e]

        def start_big(c, carry, hs=hs, lo=lo):
            _chunk_copy(hbm_ref, buf_ref, hs + c * BIG_CHUNK, lo + c * BIG_CHUNK, BIG_CHUNK, sem, to_hbm).start()
            return carry

        def start_small(c, carry, hs=hs + n_big * BIG_CHUNK, lo=lo + n_big * BIG_CHUNK):
            _chunk_copy(hbm_ref, buf_ref, hs + c * CHUNK, lo + c * CHUNK, CHUNK, sem, to_hbm).start()
            return carry

        lax.fori_loop(0, n_big, start_big, 0)
        lax.fori_loop(0, nch_ref[n_pairs + i * N_EXPERTS + e], start_small, 0)


def _chunk_waits(step, tot_ref, hbm_ref, buf_ref, sem, to_hbm):
    def wait_big(c, carry):
        _chunk_copy(hbm_ref, buf_ref, 0, 0, BIG_CHUNK, sem, to_hbm).wait()
        return carry

    def wait_small(c, carry):
        _chunk_copy(hbm_ref, buf_ref, 0, 0, CHUNK, sem, to_hbm).wait()
        return carry

    lax.fori_loop(0, tot_ref[step], wait_big, 0)
    lax.fori_loop(0, tot_ref[pl.num_programs(0) + step], wait_small, 0)


def _dispatch_kernel(hs_ref, nch_ref, lo_ref, tot_ref, zstart_ref, zvalid_ref, xb_ref, colr_ref, xs_hbm,
                     perm_ref, zero_ref, sem, zero_sem):
    i = pl.program_id(0)
    tm = xb_ref.shape[0]
    rows = perm_ref.shape[1]
    slot = i % 2

    @pl.when(i == 0)
    def _():
        zero_ref[...] = jnp.zeros_like(zero_ref)

        def zero_copy(e):
            start = pl.multiple_of(zstart_ref[e], EXPERT_TILE)
            return pltpu.make_async_copy(zero_ref, xs_hbm.at[pl.ds(start, EXPERT_TILE)], zero_sem)

        for e in range(N_EXPERTS):
            @pl.when(zvalid_ref[e] > 0)
            def _():
                zero_copy(e).start()
        for e in range(N_EXPERTS):
            @pl.when(zvalid_ref[e] > 0)
            def _():
                zero_copy(e).wait()

    r = lax.broadcasted_iota(jnp.int32, (rows, tm), 0)
    onehot = jnp.zeros((rows, tm), F32)
    for k in range(TOP_K):
        onehot = jnp.where(colr_ref[k:k + 1, :] == r, 1.0, onehot)
    perm = jnp.dot(onehot.astype(BF16), xb_ref[...], preferred_element_type=F32)
    perm_ref[slot] = _pack_halves(perm)
    _chunk_copies(i, hs_ref, nch_ref, lo_ref, xs_hbm, perm_ref.at[slot], sem.at[slot], True)

    @pl.when(i > 0)
    def _():
        _chunk_waits(jnp.maximum(i - 1, 0), tot_ref, xs_hbm, perm_ref.at[1 - slot], sem.at[1 - slot], True)

    @pl.when(i == pl.num_programs(0) - 1)
    def _():
        _chunk_waits(i, tot_ref, xs_hbm, perm_ref.at[slot], sem.at[slot], True)


def _dispatch(xb, col_rows, chunks, zero_start, zero_valid, rows, tm):
    N, D = xb.shape
    pb = _tile_buffer_rows(tm)
    return pl.pallas_call(
        _dispatch_kernel,
        grid_spec=pltpu.PrefetchScalarGridSpec(
            num_scalar_prefetch=6,
            grid=(N // tm,),
            in_specs=[pl.BlockSpec((tm, D), lambda i, *_: (i, 0)),
                      pl.BlockSpec((SUBLANES, tm), lambda i, *_: (i, 0))],
            out_specs=pl.BlockSpec(memory_space=pl.ANY),
            scratch_shapes=[pltpu.VMEM((2, pb, D // 2), jnp.uint32),
                            pltpu.VMEM((EXPERT_TILE, D // 2), jnp.uint32),
                            pltpu.SemaphoreType.DMA((2,)),
                            pltpu.SemaphoreType.DMA]),
        out_shape=jax.ShapeDtypeStruct((rows, D // 2), jnp.uint32),
        compiler_params=_cparams(("arbitrary",)),
        name="moe_dispatch",
    )(*chunks, zero_start, zero_valid, xb, col_rows)


def _expert_kernel(te_ref, nt_ref, xs_ref, w1_ref, b1_ref, w2_ref, b2_ref, o_ref, w1b_ref, w2b_ref):
    t = pl.program_id(0)

    @pl.when((t == 0) | (te_ref[t] != te_ref[jnp.maximum(t - 1, 0)]))
    def _():
        w1b_ref[...] = w1_ref[...].astype(BF16)
        w2b_ref[...] = w2_ref[...].astype(BF16)

    @pl.when(t < nt_ref[0])
    def _():
        hi, lo = _unpack_halves(xs_ref[...])
        half = hi.shape[1]
        h = jnp.dot(hi.astype(BF16), w1b_ref[:half, :], preferred_element_type=F32)
        h = h + jnp.dot(lo.astype(BF16), w1b_ref[half:, :], preferred_element_type=F32) + b1_ref[...]
        gl = jnp.minimum(h[:, :D_EXPERT], SWIGLU_LIMIT)
        ul = jnp.clip(h[:, D_EXPERT:], -SWIGLU_LIMIT, SWIGLU_LIMIT)
        a = (ul + 1.0) * (gl * jax.nn.sigmoid(SWIGLU_ALPHA * gl))
        o = jnp.dot(a.astype(BF16), w2b_ref[...], preferred_element_type=F32) + b2_ref[...]
        o_ref[...] = _pack_halves(o)


def _experts(xs, tile_expert, n_tiles, w1, b1, w2, b2, layer):
    rows, W = xs.shape
    D = 2 * W
    last = lambda t, te, nt: jnp.minimum(t, nt[0] - 1)
    return pl.pallas_call(
        _expert_kernel,
        grid_spec=pltpu.PrefetchScalarGridSpec(
            num_scalar_prefetch=2,
            grid=(rows // EXPERT_TILE,),
            in_specs=[pl.BlockSpec((EXPERT_TILE, W), lambda t, te, nt: (last(t, te, nt), 0)),
                      pl.BlockSpec((None, None, D, 2 * D_EXPERT), lambda t, te, nt: (layer, te[t], 0, 0)),
                      pl.BlockSpec((None, None, 1, 2 * D_EXPERT), lambda t, te, nt: (layer, te[t], 0, 0)),
                      pl.BlockSpec((None, None, D_EXPERT, D), lambda t, te, nt: (layer, te[t], 0, 0)),
                      pl.BlockSpec((None, None, 1, D), lambda t, te, nt: (layer, te[t], 0, 0))],
            out_specs=pl.BlockSpec((EXPERT_TILE, W), lambda t, te, nt: (last(t, te, nt), 0)),
            scratch_shapes=[pltpu.VMEM((D, 2 * D_EXPERT), BF16), pltpu.VMEM((D_EXPERT, D), BF16)]),
        out_shape=jax.ShapeDtypeStruct((rows, W), jnp.uint32),
        compiler_params=_cparams(("arbitrary",)),
        name="moe_experts",
    )(tile_expert, n_tiles, xs, w1, b1, w2, b2)


def _combine_kernel(hs_ref, nch_ref, lo_ref, tot_ref, x_ref, gate_ref, col_ref, g_ref, b_ref, eo_hbm, o_ref,
                    buf_ref, sem):
    i = pl.program_id(0)
    tm = x_ref.shape[0]
    rows = buf_ref.shape[1]
    slot = i % 2

    @pl.when(i == 0)
    def _():
        buf_ref[...] = jnp.zeros_like(buf_ref)
        _chunk_copies(0, hs_ref, nch_ref, lo_ref, eo_hbm, buf_ref.at[0], sem.at[0], False)

    @pl.when(i + 1 < pl.num_programs(0))
    def _():
        _chunk_copies(i + 1, hs_ref, nch_ref, lo_ref, eo_hbm, buf_ref.at[1 - slot], sem.at[1 - slot], False)

    _chunk_waits(i, tot_ref, eo_hbm, buf_ref.at[slot], sem.at[slot], False)

    gates = gate_ref[...]
    cols = col_ref[...]
    lane = lax.broadcasted_iota(jnp.int32, gates.shape, 1)
    r = lax.broadcasted_iota(jnp.int32, (tm, rows), 1).astype(F32)
    w = jnp.zeros((tm, rows), F32)
    for k in range(TOP_K):
        gk = jnp.sum(jnp.where(lane == k, gates, 0.0), -1, keepdims=True)
        ck = jnp.sum(jnp.where(lane == k, cols, 0.0), -1, keepdims=True)
        w = jnp.where(ck == r, gk, w)
    hi, lo = _unpack_halves(buf_ref[slot])
    vals = jnp.concatenate([hi.astype(BF16), lo.astype(BF16)], axis=1)
    y = jnp.dot(w.astype(BF16), vals, preferred_element_type=F32)
    o_ref[...] = _layer_norm(DEEPNORM_ALPHA * x_ref[...] + y, g_ref[...], b_ref[...])


def _combine(x, gates, col_lanes, g, b, chunks, eo, tm):
    N, D = x.shape
    pb = _tile_buffer_rows(tm)
    return pl.pallas_call(
        _combine_kernel,
        grid_spec=pltpu.PrefetchScalarGridSpec(
            num_scalar_prefetch=4,
            grid=(N // tm,),
            in_specs=[pl.BlockSpec((tm, D), lambda i, *_: (i, 0)),
                      pl.BlockSpec((tm, LANES), lambda i, *_: (i, 0)),
                      pl.BlockSpec((tm, LANES), lambda i, *_: (i, 0)),
                      pl.BlockSpec((1, D), lambda i, *_: (0, 0)),
                      pl.BlockSpec((1, D), lambda i, *_: (0, 0)),
                      pl.BlockSpec(memory_space=pl.ANY)],
            out_specs=pl.BlockSpec((tm, D), lambda i, *_: (i, 0)),
            scratch_shapes=[pltpu.VMEM((2, pb, D // 2), jnp.uint32),
                            pltpu.SemaphoreType.DMA((2,))]),
        out_shape=jax.ShapeDtypeStruct((N, D), F32),
        compiler_params=_cparams(("arbitrary",)),
        name="moe_combine",
    )(*chunks, x, gates, col_lanes, g, b, eo)


def _moe(x1, x1b, gates, route, tile_counts, w1, b1, w2, b2, g, b, layer, tm):
    N = x1.shape[0]
    max_tiles = (N * TOP_K + CHUNK * N_EXPERTS * (N // tm)) // EXPERT_TILE + N_EXPERTS
    chunks, col_lanes, col_rows, tile_expert, n_tiles, zero_start, zero_valid = _route_tables(
        route, tile_counts, tm, max_tiles)
    xs = _dispatch(x1b, col_rows, chunks, zero_start, zero_valid, max_tiles * EXPERT_TILE, tm)
    eo = _experts(xs, tile_expert, n_tiles, w1, b1, w2, b2, layer)
    return _combine(x1, gates, col_lanes, g, b, chunks, eo, tm)


def kernel(x, mem, w_in_a, rpb_a, w_in_b, q_norm_b, k_norm_b, w_in_c, conv_w_c, conv_b_c, filt_w1, filt_b1, filt_w2, filt_b2, filt_w3, filt_b3, filt_w4, filt_b4, filt_freq, long_bias_c, w_mem_kv, w_out, ln1_g, ln1_b, ln2_g, ln2_b, router_w, router_b, moe_w1, moe_b1, moe_w2, moe_b2):
    B, L, D = x.shape
    N = B * L
    M = mem.shape[1]
    tm = min(512, L)
    hy = dict(filt_w1=filt_w1, filt_b1=filt_b1, filt_w2=filt_w2, filt_b2=filt_b2, filt_w3=filt_w3,
              filt_b3=filt_b3, filt_w4=filt_w4, filt_b4=filt_b4, filt_freq=filt_freq,
              conv_w_c=conv_w_c, conv_b_c=conv_b_c, long_bias_c=long_bias_c)

    n_layers = w_mem_kv.shape[0]
    wkv = jnp.transpose(w_mem_kv, (1, 0, 2)).reshape(D, n_layers * 2 * MEM_WIDTH).astype(BF16)
    kv_all = _matmul(mem.reshape(B * M, D), wkv, BF16, 512, 512).reshape(B, M, n_layers, 2, MEM_WIDTH)

    w1_all = moe_w1.astype(F32)
    w2_all = moe_w2.astype(F32)
    b1_all = moe_b1.astype(F32)[:, :, None, :]
    b2_all = moe_b2.astype(F32)[:, :, None, :]

    q_cols, k_cols, out_rows_b = _gqa_column_perms()

    xf = x.reshape(N, D).astype(F32)
    for i in range(DEPTH):
        kind = i % N_MIXERS
        j = i // N_MIXERS
        wo = w_out[i]
        if kind == 0:
            proj = _matmul(xf, w_in_a[j].astype(BF16), BF16, tm, 3 * MIX_WIDTH + MEM_WIDTH)
            bias_tab = _na_bias_table(rpb_a[j], L // GRID_W)
            mix = _na_attention(proj.reshape(B, L, -1), bias_tab, B, L).reshape(N, MIX_WIDTH)
            qm_blk = (3 * MIX_WIDTH) // MEM_WIDTH
        elif kind == 1:
            w = w_in_b[j]
            w = jnp.concatenate([w[:, :MIX_WIDTH][:, q_cols],
                                 w[:, MIX_WIDTH:MIX_WIDTH + KV_WIDTH][:, k_cols],
                                 w[:, MIX_WIDTH + KV_WIDTH:]], axis=1)
            proj = _matmul(xf, w.astype(BF16), BF16, tm, w.shape[1])
            cos_t, sin_t = _rope_lane_tables(L)
            dim, _ = _pair_lane_dims()
            gq = q_norm_b[j].astype(F32)[dim][None, :]
            gk = k_norm_b[j].astype(F32)[dim][None, :]
            q_rot, k_rot = _qk_prep(proj, cos_t, sin_t, gq, gk, L, tm)
            mix = _gqa_attention(q_rot.reshape(B, L, -1), k_rot.reshape(B, L, -1), proj.reshape(B, L, -1),
                                 B, L, min(128, L)).reshape(N, MIX_WIDTH)
            wo = jnp.concatenate([wo[:MIX_WIDTH][out_rows_b], wo[MIX_WIDTH:]], axis=0)
            qm_blk = (MIX_WIDTH + 2 * KV_WIDTH) // MEM_WIDTH
        else:
            proj = _matmul(xf, w_in_c[j].astype(BF16), BF16, tm, HYENA_WIDTH + MEM_WIDTH)
            mix = _hyena_mixer(proj.reshape(B, L, -1), j, hy, B, L).reshape(N, MIX_WIDTH)
            qm_blk = HYENA_WIDTH // MEM_WIDTH
        kmem = kv_all[:, :, i, 0, :]
        vmem = kv_all[:, :, i, 1, :]
        rw = jnp.pad(router_w[i].astype(F32), ((0, 0), (0, LANES - N_EXPERTS)))
        rw_hi = rw.astype(BF16)
        rw = jnp.concatenate([rw_hi, (rw - rw_hi.astype(F32)).astype(BF16)], axis=1)
        rb = jnp.pad(router_b[i].astype(F32), (0, LANES - N_EXPERTS), constant_values=NEG_BIG)[None, :]
        x1, x1b, gates, route, counts = _post_mixer(
            xf, mix, proj, qm_blk, kmem, vmem, wo.astype(BF16),
            ln1_g[i].astype(F32)[None, :], ln1_b[i].astype(F32)[None, :], rw, rb, L, tm)
        xf = _moe(x1, x1b, gates, route, counts, w1_all, b1_all, w2_all, b2_all,
                  ln2_g[i].astype(F32)[None, :], ln2_b[i].astype(F32)[None, :], i, tm)
    return xf.reshape(B, L, D).astype(x.dtype)
```

```python
import functools
import math

import numpy as np
import jax
import jax.numpy as jnp
from jax import lax
from jax.experimental import pallas as pl
from jax.experimental.pallas import tpu as pltpu

F32 = jnp.float32
BF16 = jnp.bfloat16

D_MODEL = 1024
DEPTH = 4
GRID_W = 64
N_MIXERS = 3
HEAD_DIM = 64
MIX_WIDTH = 768
MIX_HEADS = MIX_WIDTH // HEAD_DIM
MEM_HEADS = 4
MEM_WIDTH = MEM_HEADS * HEAD_DIM
NA_KH = 8
NA_KW = 16
NA_ROW_UNROLL = 8
KV_HEADS = 4
KV_WIDTH = KV_HEADS * HEAD_DIM
GQA_GROUP = MIX_HEADS // KV_HEADS
ROPE_THETA = 10000.0
QK_NORM_EPS = 1e-6
HYENA_ORDER = 2
HYENA_BANDS = 16
HYENA_EMB = 2 * HYENA_BANDS + 1
HYENA_FFN = 64
HYENA_WIDTH = (HYENA_ORDER + 1) * MIX_WIDTH
HYENA_DECAY_TARGET = 1e-2
HYENA_FAST_DECAY = 0.3
HYENA_SLOW_DECAY = 1.5
N_EXPERTS = 32
TOP_K = 4
D_EXPERT = D_MODEL
SWIGLU_LIMIT = 7.0
SWIGLU_ALPHA = 1.702
DEEPNORM_ALPHA = (2 * DEPTH) ** 0.25
LN_EPS = 1e-5

LANES = 128
SUBLANES = 8
VMEM_LIMIT = 56 * 1024 * 1024
NEG_BIG = -1e30
ATT_SCALE = HEAD_DIM ** -0.5


def _cparams(sem):
    return pltpu.CompilerParams(dimension_semantics=sem, vmem_limit_bytes=VMEM_LIMIT)


def _layer_norm(x, g, b):
    mu = jnp.mean(x, -1, keepdims=True)
    xc = x - mu
    var = jnp.mean(xc * xc, -1, keepdims=True)
    return xc * lax.rsqrt(var + LN_EPS) * g + b


def _mm_kernel(x_ref, w_ref, o_ref):
    o_ref[...] = jnp.dot(x_ref[...].astype(BF16), w_ref[...],
                         preferred_element_type=F32).astype(o_ref.dtype)


def _matmul(x, w, out_dtype, tm, tn):
    M, K = x.shape
    N = w.shape[1]
    tm = min(tm, M)
    tn = min(tn, N)
    assert M % tm == 0 and N % tn == 0
    return pl.pallas_call(
        _mm_kernel,
        grid=(M // tm, N // tn),
        in_specs=[pl.BlockSpec((tm, K), lambda i, j: (i, 0)),
                  pl.BlockSpec((K, tn), lambda i, j: (0, j))],
        out_specs=pl.BlockSpec((tm, tn), lambda i, j: (i, j)),
        out_shape=jax.ShapeDtypeStruct((M, N), out_dtype),
        compiler_params=_cparams(("parallel", "arbitrary")),
        name="matmul",
    )(x, w)


def _na_bias_table(rpb, rows):
    del rows
    col = np.arange(GRID_W)
    c0 = np.clip(col - NA_KW // 2, 0, GRID_W - NA_KW)
    cc = np.arange(GRID_W)
    valid = (cc[None, :] >= c0[:, None]) & (cc[None, :] < c0[:, None] + NA_KW)
    coff = np.clip(cc[None, :] - col[:, None] + NA_KW - 1, 0, 2 * NA_KW - 2)
    d = np.arange(NA_KH)
    i = np.arange(NA_KH)
    roff = i[None, :] - d[:, None] + NA_KH - 1
    t = rpb.astype(F32)[:, roff]
    t = t[:, :, :, coff]
    t = t.transpose(0, 1, 3, 2, 4)
    t = jnp.where(valid[None, None, :, None, :], t, NEG_BIG)
    return t.reshape(rpb.shape[0], NA_KH, GRID_W, NA_KH * GRID_W)


def _na_kernel(q_ref, k_ref, v_ref, bias_ref, o_ref, *, rows):
    lane = lax.broadcasted_iota(jnp.int32, (GRID_W, LANES), 1)
    first = lane < HEAD_DIM
    band = NA_KH * GRID_W

    def body(r, carry):
        r0 = jnp.clip(r - NA_KH // 2, 0, rows - NA_KH)
        d = r - r0
        qoff = pl.multiple_of(r * GRID_W, GRID_W)
        koff = pl.multiple_of(r0 * GRID_W, GRID_W)
        q = q_ref[pl.ds(qoff, GRID_W), :]
        kb = k_ref[pl.ds(koff, band), :]
        vb = v_ref[pl.ds(koff, band), :]
        zero = jnp.zeros_like(q)
        q2 = jnp.concatenate([jnp.where(first, q, zero), jnp.where(first, zero, q)], axis=0)
        s = lax.dot_general(q2, kb, (((1,), (1,)), ((), ())), preferred_element_type=F32)
        bias = jnp.concatenate([bias_ref[0, d], bias_ref[1, d]], axis=0)
        s = s * ATT_SCALE + bias
        m = jnp.max(s, -1, keepdims=True)
        e = jnp.exp(s - m)
        l = jnp.sum(e, -1, keepdims=True)
        o2 = jnp.dot(e.astype(BF16), vb, preferred_element_type=F32) / l
        o = jnp.where(first, o2[:GRID_W], o2[GRID_W:])
        o_ref[pl.ds(qoff, GRID_W), :] = o.astype(o_ref.dtype)
        return carry

    lax.fori_loop(0, rows, body, 0, unroll=NA_ROW_UNROLL)


def _na_attention(proj, bias_tab, B, L):
    rows = L // GRID_W
    assert rows >= NA_KH
    npair = MIX_WIDTH // LANES
    return pl.pallas_call(
        functools.partial(_na_kernel, rows=rows),
        grid=(npair, B),
        in_specs=[pl.BlockSpec((None, L, LANES), lambda p, b: (b, 0, p)),
                  pl.BlockSpec((None, L, LANES), lambda p, b: (b, 0, npair + p)),
                  pl.BlockSpec((None, L, LANES), lambda p, b: (b, 0, 2 * npair + p)),
                  pl.BlockSpec((2, NA_KH, GRID_W, NA_KH * GRID_W), lambda p, b: (p, 0, 0, 0))],
        out_specs=pl.BlockSpec((None, L, LANES), lambda p, b: (b, 0, p)),
        out_shape=jax.ShapeDtypeStruct((B, L, MIX_WIDTH), BF16),
        compiler_params=_cparams(("parallel", "parallel")),
        name="na_attention",
    )(proj, proj, proj, bias_tab)


def _pair_lane_dims():
    l = np.arange(LANES)
    return 2 * (l % 32) + (l >= HEAD_DIM).astype(np.int64), ((l % HEAD_DIM) >= 32).astype(np.int64)


def _gqa_column_perms():
    dim, which = _pair_lane_dims()
    q_cols = []
    out_rows = []
    for P in range(KV_HEADS // 2):
        for g in range(GQA_GROUP):
            heads = ((2 * P) * GQA_GROUP + g, (2 * P + 1) * GQA_GROUP + g)
            q_cols.append(np.array([heads[w] * HEAD_DIM + dd for dd, w in zip(dim, which)]))
            out_rows.append(np.concatenate([heads[0] * HEAD_DIM + np.arange(HEAD_DIM),
                                            heads[1] * HEAD_DIM + np.arange(HEAD_DIM)]))
    k_cols = []
    for P in range(KV_HEADS // 2):
        heads = (2 * P, 2 * P + 1)
        k_cols.append(np.array([heads[w] * HEAD_DIM + dd for dd, w in zip(dim, which)]))
    return np.concatenate(q_cols), np.concatenate(k_cols), np.concatenate(out_rows)


def _rope_lane_tables(L):
    pos = jnp.arange(L)
    row = (pos // GRID_W).astype(F32)
    col = (pos % GRID_W).astype(F32)
    axis_dim = HEAD_DIM // 2
    inv_freq = ROPE_THETA ** (-jnp.arange(0, axis_dim, 2, dtype=F32) / axis_dim)
    ang = jnp.concatenate([row[:, None] * inv_freq, col[:, None] * inv_freq], -1)
    c = jnp.cos(ang)
    s = jnp.sin(ang)
    return jnp.concatenate([c, c, c, c], -1), jnp.concatenate([-s, -s, s, s], -1)


def _qk_prep_kernel(q_ref, k_ref, cos_ref, sin_ref, gq_ref, gk_ref, qo_ref, ko_ref):
    tm = q_ref.shape[0]
    lane = lax.broadcasted_iota(jnp.int32, (tm, LANES), 1)
    head_a = (lane % HEAD_DIM) < 32
    c = cos_ref[...]
    s = sin_ref[...]

    def norm_rope(x, g):
        x = x.astype(F32)
        x2 = x * x
        tot = jnp.sum(x2, -1, keepdims=True)
        sa = jnp.sum(jnp.where(head_a, x2, 0.0), -1, keepdims=True)
        inv = jnp.where(head_a, lax.rsqrt(sa * (1.0 / HEAD_DIM) + QK_NORM_EPS),
                        lax.rsqrt((tot - sa) * (1.0 / HEAD_DIM) + QK_NORM_EPS))
        xn = x * inv * g
        return xn * c + pltpu.roll(xn, HEAD_DIM, axis=1) * s

    for j in range(q_ref.shape[1] // LANES):
        sl = slice(j * LANES, (j + 1) * LANES)
        qo_ref[:, sl] = (norm_rope(q_ref[:, sl], gq_ref[...]) * ATT_SCALE).astype(qo_ref.dtype)
    for j in range(k_ref.shape[1] // LANES):
        sl = slice(j * LANES, (j + 1) * LANES)
        ko_ref[:, sl] = norm_rope(k_ref[:, sl], gk_ref[...]).astype(ko_ref.dtype)


def _qk_prep(proj, cos_t, sin_t, gq, gk, L, tm):
    N = proj.shape[0]
    nl = L // tm
    return pl.pallas_call(
        _qk_prep_kernel,
        grid=(N // tm,),
        in_specs=[pl.BlockSpec((tm, MIX_WIDTH), lambda i: (i, 0)),
                  pl.BlockSpec((tm, KV_WIDTH), lambda i: (i, MIX_WIDTH // KV_WIDTH)),
                  pl.BlockSpec((tm, LANES), lambda i: (i % nl, 0)),
                  pl.BlockSpec((tm, LANES), lambda i: (i % nl, 0)),
                  pl.BlockSpec((1, LANES), lambda i: (0, 0)),
                  pl.BlockSpec((1, LANES), lambda i: (0, 0))],
        out_specs=[pl.BlockSpec((tm, MIX_WIDTH), lambda i: (i, 0)),
                   pl.BlockSpec((tm, KV_WIDTH), lambda i: (i, 0))],
        out_shape=[jax.ShapeDtypeStruct((N, MIX_WIDTH), BF16),
                   jax.ShapeDtypeStruct((N, KV_WIDTH), BF16)],
        compiler_params=_cparams(("parallel",)),
        name="qk_prep",
    )(proj, proj, cos_t, sin_t, gq, gk)


GQA_KV_CHUNK = 1024
GQA_STREAMS = 1


def _gqa_kernel(q_ref, k_ref, v_ref, o_ref, q6_ref, s_ref, m_ref, l_ref, acc_ref):
    ns = q6_ref.shape[0]
    tq = q_ref.shape[0] // ns
    tk = s_ref.shape[3]
    nk = k_ref.shape[0] // tk
    lane = lax.broadcasted_iota(jnp.int32, (tq, LANES), 1)
    head_a = (lane % HEAD_DIM) < 32
    for st in range(ns):
        q = q_ref[st * tq:(st + 1) * tq, :]
        for g in range(GQA_GROUP):
            qs = q[:, g * LANES:(g + 1) * LANES]
            zero = jnp.zeros_like(qs)
            q6_ref[st, (2 * g) * tq:(2 * g + 1) * tq, :] = jnp.where(head_a, qs, zero)
            q6_ref[st, (2 * g + 1) * tq:(2 * g + 2) * tq, :] = jnp.where(head_a, zero, qs)
    m_ref[...] = jnp.full(m_ref.shape, -jnp.inf, F32)
    l_ref[...] = jnp.zeros(l_ref.shape, F32)
    acc_ref[...] = jnp.zeros(acc_ref.shape, F32)

    def scores(st, j):
        return lax.dot_general(q6_ref[st], k_ref[j * tk:(j + 1) * tk, :], (((1,), (1,)), ((), ())),
                               preferred_element_type=F32)

    for st in range(ns):
        s_ref[st, 0] = scores(st, 0)
    for j in range(nk):
        for st in range(ns):
            if j + 1 < nk:
                s_ref[st, (j + 1) % 2] = scores(st, j + 1)
            s = s_ref[st, j % 2]
            m_old = m_ref[st]
            m_new = jnp.maximum(m_old, jnp.max(s, -1, keepdims=True))
            alpha = jnp.exp(m_old - m_new)
            p = jnp.exp(s - jnp.concatenate([m_new] * (tk // LANES), axis=1))
            l_ref[st] = alpha * l_ref[st] + jnp.sum(p, -1, keepdims=True)
            acc_ref[st] = alpha * acc_ref[st] + jnp.dot(p.astype(BF16), v_ref[j * tk:(j + 1) * tk, :],
                                                        preferred_element_type=F32)
            m_ref[st] = m_new
    for st in range(ns):
        o6 = acc_ref[st] / l_ref[st]
        for g in range(GQA_GROUP):
            oa = o6[(2 * g) * tq:(2 * g + 1) * tq]
            ob = o6[(2 * g + 1) * tq:(2 * g + 2) * tq]
            o_ref[st * tq:(st + 1) * tq, g * LANES:(g + 1) * LANES] = (
                jnp.where(lane < HEAD_DIM, oa, ob).astype(o_ref.dtype))


def _gqa_attention(q_rot, k_rot, proj, B, L, tq):
    slab_w = GQA_GROUP * LANES
    v_blk0 = (MIX_WIDTH + KV_WIDTH) // LANES
    rows6 = 2 * GQA_GROUP * tq
    tk = min(GQA_KV_CHUNK, L)
    ns = GQA_STREAMS if L % (GQA_STREAMS * tq) == 0 else 1
    tqb = ns * tq
    return pl.pallas_call(
        _gqa_kernel,
        grid=(B, KV_HEADS // 2, L // tqb),
        in_specs=[pl.BlockSpec((None, tqb, slab_w), lambda b, p, i: (b, i, p)),
                  pl.BlockSpec((None, L, LANES), lambda b, p, i: (b, 0, p)),
                  pl.BlockSpec((None, L, LANES), lambda b, p, i: (b, 0, v_blk0 + p))],
        out_specs=pl.BlockSpec((None, tqb, slab_w), lambda b, p, i: (b, i, p)),
        out_shape=jax.ShapeDtypeStruct((B, L, MIX_WIDTH), BF16),
        scratch_shapes=[pltpu.VMEM((ns, rows6, LANES), BF16),
                        pltpu.VMEM((ns, 2, rows6, tk), F32),
                        pltpu.VMEM((ns, rows6, LANES), F32),
                        pltpu.VMEM((ns, rows6, LANES), F32),
                        pltpu.VMEM((ns, rows6, LANES), F32)],
        compiler_params=_cparams(("parallel", "parallel", "arbitrary")),
        name="gqa_attention",
    )(q_rot, k_rot, proj)


def _hyena_features(L):
    pos = jnp.arange(L, dtype=F32)
    t = pos / max(L - 1, 1)
    w = 2.0 * math.pi * pos / L
    f = jnp.linspace(1e-4, HYENA_BANDS - 1, HYENA_BANDS, dtype=F32)
    wf = w[:, None] * f[None, :]
    z = jnp.concatenate([t[:, None], jnp.cos(wf), -jnp.sin(wf)], -1)
    return jnp.pad(z, ((0, 0), (0, LANES - HYENA_EMB))), t[:, None]


def _filter_kernel(z_ref, t_ref, w1_ref, b1_ref, w2_ref, b2_ref, w3_ref, b3_ref, w4f_ref, w4b_ref,
                   b4f_ref, b4b_ref, fr_ref, delta_ref, hsum_ref, hdiff_ref):
    hp = lax.Precision.HIGHEST
    fr = fr_ref[...]
    h = jnp.sin(fr * (jnp.dot(z_ref[...], w1_ref[...], precision=hp, preferred_element_type=F32) + b1_ref[...]))
    h = jnp.sin(fr * (jnp.dot(h, w2_ref[...], precision=hp, preferred_element_type=F32) + b2_ref[...]))
    h = jnp.sin(fr * (jnp.dot(h, w3_ref[...], precision=hp, preferred_element_type=F32) + b3_ref[...]))
    hf = jnp.dot(h, w4f_ref[...], precision=hp, preferred_element_type=F32) + b4f_ref[...]
    hb = jnp.dot(h, w4b_ref[...], precision=hp, preferred_element_type=F32) + b4b_ref[...]
    dec = jnp.exp(-t_ref[...] * delta_ref[...])
    hf = hf * dec
    hb = hb * dec
    norm = jnp.sum(jnp.abs(hf), 0, keepdims=True) + jnp.sum(jnp.abs(hb), 0, keepdims=True)
    inv = 1.0 / norm
    hf = hf * inv
    row = lax.broadcasted_iota(jnp.int32, hb.shape, 0)
    hb = jnp.where(row == 0, 0.0, hb * inv)
    hsum_ref[...] = hf + hb
    hdiff_ref[...] = hf - hb


def _hyena_filters(zf, t, w1, b1, w2, b2, w3, b3, w4, b4, freq, L, tc):
    OC = HYENA_ORDER * MIX_WIDTH
    max_decay = math.log(HYENA_DECAY_TARGET) / HYENA_FAST_DECAY
    min_decay = math.log(HYENA_DECAY_TARGET) / HYENA_SLOW_DECAY
    deltas = jnp.abs(jnp.linspace(min_decay, max_decay, MIX_WIDTH, dtype=F32))
    deltas = jnp.tile(deltas, HYENA_ORDER)[None, :]
    w1p = jnp.pad(w1.astype(F32), ((0, LANES - HYENA_EMB), (0, 0)))
    row = lambda a: a.astype(F32)[None, :]
    full = lambda shape: pl.BlockSpec(shape, lambda j: (0,) * len(shape))
    nb = OC // tc
    return pl.pallas_call(
        _filter_kernel,
        grid=(nb,),
        in_specs=[full((L, LANES)), full((L, 1)),
                  full((LANES, HYENA_FFN)), full((1, HYENA_FFN)),
                  full((HYENA_FFN, HYENA_FFN)), full((1, HYENA_FFN)),
                  full((HYENA_FFN, HYENA_FFN)), full((1, HYENA_FFN)),
                  pl.BlockSpec((HYENA_FFN, tc), lambda j: (0, j)),
                  pl.BlockSpec((HYENA_FFN, tc), lambda j: (0, nb + j)),
                  pl.BlockSpec((1, tc), lambda j: (0, j)),
                  pl.BlockSpec((1, tc), lambda j: (0, nb + j)),
                  full((1, HYENA_FFN)),
                  pl.BlockSpec((1, tc), lambda j: (0, j))],
        out_specs=[pl.BlockSpec((L, tc), lambda j: (0, j)),
                   pl.BlockSpec((L, tc), lambda j: (0, j))],
        out_shape=[jax.ShapeDtypeStruct((L, OC), F32), jax.ShapeDtypeStruct((L, OC), F32)],
        compiler_params=_cparams(("parallel",)),
        name="hyena_filters",
    )(zf, t, w1p, row(b1), w2.astype(F32), row(b2), w3.astype(F32), row(b3), w4.astype(F32), w4.astype(F32),
      row(b4), row(b4), row(freq), deltas)


def _dft_matrices(L):
    n = 2 * L
    k = jnp.arange(L, dtype=jnp.int32)
    kb = 64 if L % 64 == 0 else 1
    s = k[None, :]

    def table(mult):
        ang = ((mult[:, None] * s) % n).astype(F32) * (2.0 * math.pi / n)
        return jnp.cos(ang), jnp.sin(ang)

    ca, sa = table(jnp.arange(L // kb, dtype=jnp.int32) * kb)
    cb, sb = table(jnp.arange(kb, dtype=jnp.int32))
    fc = (ca[:, None, :] * cb[None, :, :] - sa[:, None, :] * sb[None, :, :]).reshape(L, L)
    sn = (sa[:, None, :] * cb[None, :, :] + ca[:, None, :] * sb[None, :, :]).reshape(L, L)
    alt = jnp.where(k % 2 == 0, 1.0, -1.0).astype(F32)
    fs = jnp.where(k[:, None] == 0, alt[None, :], sn)
    fst = jnp.where(k[None, :] == 0, alt[:, None], sn)
    return fc.astype(BF16), fs.astype(BF16), fst.astype(BF16)


def _short_conv_kernel(u_ref, w_ref, b_ref, o_ref):
    L = u_ref.shape[0]
    u = u_ref[...].astype(F32)
    row = lax.broadcasted_iota(jnp.int32, u.shape, 0)
    prev = jnp.where(row == 0, 0.0, pltpu.roll(u, 1, axis=0))
    nxt = jnp.where(row == L - 1, 0.0, pltpu.roll(u, L - 1, axis=0))
    w = w_ref[...]
    o_ref[...] = (prev * w[0:1] + u * w[1:2] + nxt * w[2:3] + b_ref[...]).astype(o_ref.dtype)


def _short_conv(proj, conv_w, conv_b, B, L, tc):
    return pl.pallas_call(
        _short_conv_kernel,
        grid=(B, HYENA_WIDTH // tc),
        in_specs=[pl.BlockSpec((None, L, tc), lambda b, j: (b, 0, j)),
                  pl.BlockSpec((3, tc), lambda b, j: (0, j)),
                  pl.BlockSpec((1, tc), lambda b, j: (0, j))],
        out_specs=pl.BlockSpec((None, L, tc), lambda b, j: (b, 0, j)),
        out_shape=jax.ShapeDtypeStruct((B, L, HYENA_WIDTH), BF16),
        compiler_params=_cparams(("parallel", "parallel")),
        name="short_conv",
    )(proj, conv_w.astype(F32), conv_b.astype(F32)[None, :])


def _hy_fwd_kernel(fc_ref, fs_ref, z_ref, pa_ref, qa_ref, qb_ref, sb_ref, ya_ref, yb_ref):
    z = z_ref[...]
    a = jnp.dot(fc_ref[...], z, preferred_element_type=F32)
    b = jnp.dot(fs_ref[...], z, preferred_element_type=F32)
    ya_ref[...] = (a * pa_ref[...] - b * qa_ref[...]).astype(ya_ref.dtype)
    yb_ref[...] = (a * qb_ref[...] + b * sb_ref[...]).astype(yb_ref.dtype)


def _hy_inv_kernel(fc_ref, fst_ref, ya_ref, yb_ref, z_ref, xg_ref, bias_ref, o_ref):
    y = jnp.dot(fc_ref[...], ya_ref[...], preferred_element_type=F32)
    y = y + jnp.dot(fst_ref[...], yb_ref[...], preferred_element_type=F32)
    z = z_ref[...].astype(F32)
    o_ref[...] = (xg_ref[...].astype(F32) * (y + bias_ref[...] * z)).astype(o_ref.dtype)


def _hyena_long_conv_gate(uc, z_arr, z_blk, xg_blk, fc, fs, fst, coef, bias, B, L, tm_f, tm_i):
    C = MIX_WIDTH
    pa, qa, qb, sb = coef
    ya, yb = pl.pallas_call(
        _hy_fwd_kernel,
        grid=(B, L // tm_f),
        in_specs=[pl.BlockSpec((tm_f, L), lambda b, i: (i, 0)),
                  pl.BlockSpec((tm_f, L), lambda b, i: (i, 0)),
                  pl.BlockSpec((None, L, C), lambda b, i: (b, 0, z_blk)),
                  pl.BlockSpec((tm_f, C), lambda b, i: (i, 0)),
                  pl.BlockSpec((tm_f, C), lambda b, i: (i, 0)),
                  pl.BlockSpec((tm_f, C), lambda b, i: (i, 0)),
                  pl.BlockSpec((tm_f, C), lambda b, i: (i, 0))],
        out_specs=[pl.BlockSpec((None, tm_f, C), lambda b, i: (b, i, 0)),
                   pl.BlockSpec((None, tm_f, C), lambda b, i: (b, i, 0))],
        out_shape=[jax.ShapeDtypeStruct((B, L, C), BF16), jax.ShapeDtypeStruct((B, L, C), BF16)],
        compiler_params=_cparams(("parallel", "arbitrary")),
        name="hyena_dft_fwd",
    )(fc, fs, z_arr, pa, qa, qb, sb)
    return pl.pallas_call(
        _hy_inv_kernel,
        grid=(B, L // tm_i),
        in_specs=[pl.BlockSpec((tm_i, L), lambda b, i: (i, 0)),
                  pl.BlockSpec((tm_i, L), lambda b, i: (i, 0)),
                  pl.BlockSpec((None, L, C), lambda b, i: (b, 0, 0)),
                  pl.BlockSpec((None, L, C), lambda b, i: (b, 0, 0)),
                  pl.BlockSpec((None, tm_i, C), lambda b, i: (b, i, z_blk)),
                  pl.BlockSpec((None, tm_i, C), lambda b, i: (b, i, xg_blk)),
                  pl.BlockSpec((1, C), lambda b, i: (0, 0))],
        out_specs=pl.BlockSpec((None, tm_i, C), lambda b, i: (b, i, 0)),
        out_shape=jax.ShapeDtypeStruct((B, L, C), BF16),
        compiler_params=_cparams(("parallel", "arbitrary")),
        name="hyena_dft_inv",
    )(fc, fst, ya, yb, z_arr, uc, bias)


def _hyena_mixer(proj, j, p, B, L):
    C = MIX_WIDTH
    OC = HYENA_ORDER * C
    n = 2 * L
    zf, t = _hyena_features(L)
    hsum, hdiff = _hyena_filters(zf, t, p['filt_w1'][j], p['filt_b1'][j], p['filt_w2'][j], p['filt_b2'][j],
                                 p['filt_w3'][j], p['filt_b3'][j], p['filt_w4'][j], p['filt_b4'][j],
                                 p['filt_freq'][j], L, 256)
    fc, fs, fst = _dft_matrices(L)
    def hi_lo(h):
        hi = h.astype(BF16)
        return jnp.concatenate([hi, (h - hi.astype(F32)).astype(BF16)], axis=1)

    hs_cat = hi_lo(hsum)
    sc = _matmul(fc, hs_cat, F32, 512, 512)
    ss = _matmul(fs, hi_lo(hdiff), F32, 512, 512)
    ny = _matmul(fs[:2 * SUBLANES], hs_cat, F32, 2 * SUBLANES, 512)
    gc = sc[:, :OC] + sc[:, OC:]
    nyq = ny[0:1, :OC] + ny[0:1, OC:]
    gs = ss[:, :OC] + ss[:, OC:]
    k0 = (jnp.arange(L) == 0)[:, None]
    ca = jnp.where(k0, 1.0 / n, 2.0 / n).astype(F32)
    pa = ca * gc
    qa = ca * jnp.where(k0, 0.0, gs)
    qb = qa
    sb = ca * jnp.where(k0, nyq, gc)
    uc = _short_conv(proj, p['conv_w_c'][j], p['conv_b_c'][j], B, L, 256)
    bias = p['long_bias_c'][j].astype(F32)
    tm_f = min(512, L)
    tm_i = min(256, L)
    z = uc
    z_blk = 0
    for o in range(HYENA_ORDER):
        sl = slice(o * C, (o + 1) * C)
        coef = (pa[:, sl], qa[:, sl], qb[:, sl], sb[:, sl])
        z = _hyena_long_conv_gate(uc, z, z_blk, o + 1, fc, fs, fst, coef, bias[o][None, :], B, L, tm_f, tm_i)
        z_blk = 0
    return z


def _pack_halves(x):
    c = x.shape[1] // 2
    bits = lax.bitcast_convert_type(x.astype(BF16).astype(F32), jnp.uint32)
    return (bits[:, :c] & jnp.uint32(0xFFFF0000)) | (bits[:, c:] >> 16)


def _unpack_halves(w):
    hi = lax.bitcast_convert_type(w & jnp.uint32(0xFFFF0000), F32)
    lo = lax.bitcast_convert_type(w << 16, F32)
    return hi, lo


def _post_kernel(x_ref, mix_ref, qm_ref, km_ref, vm_ref, wo_ref, g_ref, b_ref, rw_ref, rb_ref,
                 xo_ref, xb_ref, gate_ref, route_ref, count_ref):
    tm = x_ref.shape[0]
    lane = lax.broadcasted_iota(jnp.int32, (tm, LANES), 1)
    first = lane < HEAD_DIM
    outs = []
    for p in range(MEM_WIDTH // LANES):
        sl = slice(p * LANES, (p + 1) * LANES)
        qs = qm_ref[:, sl]
        zero = jnp.zeros_like(qs)
        q2 = jnp.concatenate([jnp.where(first, qs, zero), jnp.where(first, zero, qs)], axis=0)
        s = lax.dot_general(q2, km_ref[:, sl], (((1,), (1,)), ((), ())), preferred_element_type=F32) * ATT_SCALE
        m = jnp.max(s, -1, keepdims=True)
        e = jnp.exp(s - m)
        l = jnp.sum(e, -1, keepdims=True)
        o2 = jnp.dot(e.astype(BF16), vm_ref[:, sl], preferred_element_type=F32) / l
        outs.append(jnp.where(first, o2[:tm], o2[tm:]).astype(BF16))
    mem_out = jnp.concatenate(outs, axis=1)
    acc = jnp.dot(mix_ref[...], wo_ref[:MIX_WIDTH, :], preferred_element_type=F32)
    acc = acc + jnp.dot(mem_out, wo_ref[MIX_WIDTH:, :], preferred_element_type=F32)
    x1 = _layer_norm(DEEPNORM_ALPHA * x_ref[...] + acc, g_ref[...], b_ref[...])
    xo_ref[...] = x1
    x_hi = x1.astype(BF16)
    xb_ref[...] = x_hi
    x_lo = (x1 - x_hi.astype(F32)).astype(BF16)
    prod = jnp.dot(x_hi, rw_ref[...], preferred_element_type=F32)
    logits = (prod[:, :LANES] + prod[:, LANES:]
              + jnp.dot(x_lo, rw_ref[:, :LANES], preferred_element_type=F32) + rb_ref[...])
    lane_f = lane.astype(F32)
    work = logits
    chosen = jnp.zeros_like(logits)
    gates = jnp.zeros_like(logits)
    denom = jnp.zeros((tm, 1), F32)
    sels, idxs = [], []
    m0 = None
    for k in range(TOP_K):
        m = jnp.max(work, -1, keepdims=True)
        idx = jnp.min(jnp.where(work == m, lane_f, float(LANES)), -1, keepdims=True)
        sel = lane_f == idx
        if k == 0:
            m0 = m
        ek = jnp.exp(m - m0)
        gates = jnp.where(lane == k, ek, gates)
        chosen = chosen + jnp.where(sel, 1.0, 0.0)
        denom = denom + ek
        work = jnp.where(sel, -jnp.inf, work)
        sels.append(sel)
        idxs.append(idx)
    gate_ref[...] = gates / denom

    r_i = lax.broadcasted_iota(jnp.int32, (tm, tm), 0)
    c_i = lax.broadcasted_iota(jnp.int32, (tm, tm), 1)
    before = jnp.where(r_i > c_i, 1.0, 0.0).astype(BF16)
    prefix = jnp.dot(before, chosen.astype(BF16), preferred_element_type=F32)
    route = jnp.zeros((tm, LANES), jnp.int32)
    for k in range(TOP_K):
        rank = jnp.sum(jnp.where(sels[k], prefix, 0.0), -1, keepdims=True)
        route = jnp.where(lane == k, rank.astype(jnp.int32), route)
        route = jnp.where(lane == TOP_K + k, idxs[k].astype(jnp.int32), route)
    route_ref[...] = route
    count_ref[...] = jnp.broadcast_to(jnp.sum(chosen, 0, keepdims=True), count_ref.shape)


def _post_mixer(x, mix, proj, qm_blk, kmem, vmem, wo, g, b, rw, rb, L, tm):
    N, D = x.shape
    nl = L // tm
    CAT = MIX_WIDTH + MEM_WIDTH
    return pl.pallas_call(
        _post_kernel,
        grid=(N // tm,),
        in_specs=[pl.BlockSpec((tm, D), lambda i: (i, 0)),
                  pl.BlockSpec((tm, MIX_WIDTH), lambda i: (i, 0)),
                  pl.BlockSpec((tm, MEM_WIDTH), lambda i: (i, qm_blk)),
                  pl.BlockSpec((None, kmem.shape[1], MEM_WIDTH), lambda i: (i // nl, 0, 0)),
                  pl.BlockSpec((None, vmem.shape[1], MEM_WIDTH), lambda i: (i // nl, 0, 0)),
                  pl.BlockSpec((CAT, D), lambda i: (0, 0)),
                  pl.BlockSpec((1, D), lambda i: (0, 0)),
                  pl.BlockSpec((1, D), lambda i: (0, 0)),
                  pl.BlockSpec((D, 2 * LANES), lambda i: (0, 0)),
                  pl.BlockSpec((1, LANES), lambda i: (0, 0))],
        out_specs=[pl.BlockSpec((tm, D), lambda i: (i, 0)),
                   pl.BlockSpec((tm, D), lambda i: (i, 0)),
                   pl.BlockSpec((tm, LANES), lambda i: (i, 0)),
                   pl.BlockSpec((tm, LANES), lambda i: (i, 0)),
                   pl.BlockSpec((SUBLANES, LANES), lambda i: (i, 0))],
        out_shape=[jax.ShapeDtypeStruct((N, D), F32), jax.ShapeDtypeStruct((N, D), BF16),
                   jax.ShapeDtypeStruct((N, LANES), F32), jax.ShapeDtypeStruct((N, LANES), jnp.int32),
                   jax.ShapeDtypeStruct((N // tm * SUBLANES, LANES), F32)],
        compiler_params=_cparams(("parallel",)),
        name="post_mixer",
    )(x, mix, proj, kmem, vmem, wo, g, b, rw, rb)


EXPERT_TILE = 512
CHUNK = SUBLANES
BIG_CHUNK = 4 * CHUNK
EXPERT_ROW_SPLIT = 2


def _tile_buffer_rows(tm):
    return tm * TOP_K + CHUNK * N_EXPERTS


def _route_tables(route, tile_counts, tm, max_tiles):
    N = route.shape[0]
    nt = N // tm
    cnt = tile_counts.reshape(nt, SUBLANES, LANES)[:, 0, :N_EXPERTS].astype(jnp.int32)
    rc = ((cnt + CHUNK - 1) // CHUNK) * CHUNK
    base = jnp.cumsum(rc, axis=0) - rc
    padded = ((jnp.sum(rc, axis=0) + EXPERT_TILE - 1) // EXPERT_TILE) * EXPERT_TILE
    ends = jnp.cumsum(padded)
    offs = ends - padded
    local = jnp.cumsum(rc, axis=1) - rc
    hbm_start = (offs[None, :] + base).reshape(-1).astype(jnp.int32)
    n_big = rc // BIG_CHUNK
    n_small = (rc % BIG_CHUNK) // CHUNK
    n_chunks = jnp.concatenate([n_big.reshape(-1), n_small.reshape(-1)]).astype(jnp.int32)
    total_chunks = jnp.concatenate([jnp.sum(n_big, axis=1), jnp.sum(n_small, axis=1)]).astype(jnp.int32)
    eid = route[:, TOP_K:2 * TOP_K]
    local_tok = jnp.repeat(local, tm, axis=0)[:, None, :]
    hit = eid[:, :, None] == jnp.arange(N_EXPERTS, dtype=jnp.int32)[None, None, :]
    col = jnp.sum(jnp.where(hit, local_tok, 0), axis=2) + route[:, :TOP_K]
    col_lanes = jnp.pad(col.astype(F32), ((0, 0), (0, LANES - TOP_K)), constant_values=-1.0)
    col_rows = jnp.pad(col.reshape(nt, tm, TOP_K).transpose(0, 2, 1), ((0, 0), (0, SUBLANES - TOP_K), (0, 0)),
                       constant_values=-1).reshape(nt * SUBLANES, tm).astype(jnp.int32)
    tile_ends = ends // EXPERT_TILE
    n_tiles = tile_ends[-1]
    t = jnp.arange(max_tiles, dtype=jnp.int32)
    t_live = jnp.minimum(t, n_tiles - 1)
    tile_expert = jnp.sum((t_live[:, None] >= tile_ends[None, :]).astype(jnp.int32), axis=1)
    tile_expert = jnp.minimum(tile_expert, N_EXPERTS - 1)
    zero_start = (ends - EXPERT_TILE).astype(jnp.int32)
    zero_valid = (padded > 0).astype(jnp.int32)
    chunks = (hbm_start, n_chunks, local.reshape(-1).astype(jnp.int32), total_chunks)
    return chunks, col_lanes, col_rows, tile_expert, n_tiles.reshape(1).astype(jnp.int32), zero_start, zero_valid


def _chunk_copy(hbm_ref, buf_ref, h0, b0, size, sem, to_hbm):
    aligned = lambda v: v if isinstance(v, int) else pl.multiple_of(v, CHUNK)
    h = hbm_ref.at[pl.ds(aligned(h0), size)]
    b = buf_ref.at[pl.ds(aligned(b0), size)]
    return pltpu.make_async_copy(b, h, sem) if to_hbm else pltpu.make_async_copy(h, b, sem)


def _chunk_copies(i, hs_ref, nch_ref, lo_ref, hbm_ref, buf_ref, sem, to_hbm):
    n_pairs = pl.num_programs(0) * N_EXPERTS
    for e in range(N_EXPERTS):
        hs = hs_ref[i * N_EXPERTS + e]
        lo = lo_ref[i * N_EXPERTS + e]
        n_big = nch_ref[i * N_EXPERTS + e]

        def start_big(c, carry, hs=hs, lo=lo):
            _chunk_copy(hbm_ref, buf_ref, hs + c * BIG_CHUNK, lo + c * BIG_CHUNK, BIG_CHUNK, sem, to_hbm).start()
            return carry

        def start_small(c, carry, hs=hs + n_big * BIG_CHUNK, lo=lo + n_big * BIG_CHUNK):
            _chunk_copy(hbm_ref, buf_ref, hs + c * CHUNK, lo + c * CHUNK, CHUNK, sem, to_hbm).start()
            return carry

        lax.fori_loop(0, n_big, start_big, 0)
        lax.fori_loop(0, nch_ref[n_pairs + i * N_EXPERTS + e], start_small, 0)


def _chunk_waits(step, tot_ref, hbm_ref, buf_ref, sem, to_hbm):
    def wait_big(c, carry):
        _chunk_copy(hbm_ref, buf_ref, 0, 0, BIG_CHUNK, sem, to_hbm).wait()
        return carry

    def wait_small(c, carry):
        _chunk_copy(hbm_ref, buf_ref, 0, 0, CHUNK, sem, to_hbm).wait()
        return carry

    lax.fori_loop(0, tot_ref[step], wait_big, 0)
    lax.fori_loop(0, tot_ref[pl.num_programs(0) + step], wait_small, 0)


def _dispatch_kernel(hs_ref, nch_ref, lo_ref, tot_ref, zstart_ref, zvalid_ref, xb_ref, colr_ref, xs_hbm,
                     perm_ref, zero_ref, sem, zero_sem):
    i = pl.program_id(0)
    tm = xb_ref.shape[0]
    rows = perm_ref.shape[1]
    slot = i % 2

    @pl.when(i == 0)
    def _():
        zero_ref[...] = jnp.zeros_like(zero_ref)

        def zero_copy(e):
            start = pl.multiple_of(zstart_ref[e], EXPERT_TILE)
            return pltpu.make_async_copy(zero_ref, xs_hbm.at[pl.ds(start, EXPERT_TILE)], zero_sem)

        for e in range(N_EXPERTS):
            @pl.when(zvalid_ref[e] > 0)
            def _():
                zero_copy(e).start()
        for e in range(N_EXPERTS):
            @pl.when(zvalid_ref[e] > 0)
            def _():
                zero_copy(e).wait()

    r = lax.broadcasted_iota(jnp.int32, (rows, tm), 0)
    onehot = jnp.zeros((rows, tm), F32)
    for k in range(TOP_K):
        onehot = jnp.where(colr_ref[k:k + 1, :] == r, 1.0, onehot)
    perm = jnp.dot(onehot.astype(BF16), xb_ref[...], preferred_element_type=F32)
    perm_ref[slot] = _pack_halves(perm)
    _chunk_copies(i, hs_ref, nch_ref, lo_ref, xs_hbm, perm_ref.at[slot], sem.at[slot], True)

    @pl.when(i > 0)
    def _():
        _chunk_waits(jnp.maximum(i - 1, 0), tot_ref, xs_hbm, perm_ref.at[1 - slot], sem.at[1 - slot], True)

    @pl.when(i == pl.num_programs(0) - 1)
    def _():
        _chunk_waits(i, tot_ref, xs_hbm, perm_ref.at[slot], sem.at[slot], True)


def _dispatch(xb, col_rows, chunks, zero_start, zero_valid, rows, tm):
    N, D = xb.shape
    pb = _tile_buffer_rows(tm)
    return pl.pallas_call(
        _dispatch_kernel,
        grid_spec=pltpu.PrefetchScalarGridSpec(
            num_scalar_prefetch=6,
            grid=(N // tm,),
            in_specs=[pl.BlockSpec((tm, D), lambda i, *_: (i, 0)),
                      pl.BlockSpec((SUBLANES, tm), lambda i, *_: (i, 0))],
            out_specs=pl.BlockSpec(memory_space=pl.ANY),
            scratch_shapes=[pltpu.VMEM((2, pb, D // 2), jnp.uint32),
                            pltpu.VMEM((EXPERT_TILE, D // 2), jnp.uint32),
                            pltpu.SemaphoreType.DMA((2,)),
                            pltpu.SemaphoreType.DMA]),
        out_shape=jax.ShapeDtypeStruct((rows, D // 2), jnp.uint32),
        compiler_params=_cparams(("arbitrary",)),
        name="moe_dispatch",
    )(*chunks, zero_start, zero_valid, xb, col_rows)


def _expert_kernel(te_ref, nt_ref, xs_ref, w1_ref, b1_ref, w2_ref, b2_ref, o_ref, w1b_ref, w2b_ref):
    t = pl.program_id(0)

    @pl.when((t == 0) | (te_ref[t] != te_ref[jnp.maximum(t - 1, 0)]))
    def _():
        w1b_ref[...] = w1_ref[...].astype(BF16)
        w2b_ref[...] = w2_ref[...].astype(BF16)

    @pl.when(t < nt_ref[0])
    def _():
        rows = xs_ref.shape[0] // EXPERT_ROW_SPLIT
        for blk in range(EXPERT_ROW_SPLIT):
            sl = slice(blk * rows, (blk + 1) * rows)
            hi, lo = _unpack_halves(xs_ref[sl, :])
            half = hi.shape[1]
            h = jnp.dot(hi.astype(BF16), w1b_ref[:half, :], preferred_element_type=F32)
            h = h + jnp.dot(lo.astype(BF16), w1b_ref[half:, :], preferred_element_type=F32) + b1_ref[...]
            gl = jnp.minimum(h[:, :D_EXPERT], SWIGLU_LIMIT)
            ul = jnp.clip(h[:, D_EXPERT:], -SWIGLU_LIMIT, SWIGLU_LIMIT)
            a = (ul + 1.0) * (gl * jax.nn.sigmoid(SWIGLU_ALPHA * gl))
            o = jnp.dot(a.astype(BF16), w2b_ref[...], preferred_element_type=F32) + b2_ref[...]
            o_ref[sl, :] = _pack_halves(o)


def _experts(xs, tile_expert, n_tiles, w1, b1, w2, b2, layer):
    rows, W = xs.shape
    D = 2 * W
    last = lambda t, te, nt: jnp.minimum(t, nt[0] - 1)
    return pl.pallas_call(
        _expert_kernel,
        grid_spec=pltpu.PrefetchScalarGridSpec(
            num_scalar_prefetch=2,
            grid=(rows // EXPERT_TILE,),
            in_specs=[pl.BlockSpec((EXPERT_TILE, W), lambda t, te, nt: (last(t, te, nt), 0)),
                      pl.BlockSpec((None, None, D, 2 * D_EXPERT), lambda t, te, nt: (layer, te[t], 0, 0)),
                      pl.BlockSpec((None, None, 1, 2 * D_EXPERT), lambda t, te, nt: (layer, te[t], 0, 0)),
                      pl.BlockSpec((None, None, D_EXPERT, D), lambda t, te, nt: (layer, te[t], 0, 0)),
                      pl.BlockSpec((None, None, 1, D), lambda t, te, nt: (layer, te[t], 0, 0))],
            out_specs=pl.BlockSpec((EXPERT_TILE, W), lambda t, te, nt: (last(t, te, nt), 0)),
            scratch_shapes=[pltpu.VMEM((D, 2 * D_EXPERT), BF16), pltpu.VMEM((D_EXPERT, D), BF16)]),
        out_shape=jax.ShapeDtypeStruct((rows, W), jnp.uint32),
        compiler_params=_cparams(("arbitrary",)),
        name="moe_experts",
    )(tile_expert, n_tiles, xs, w1, b1, w2, b2)


def _combine_kernel(hs_ref, nch_ref, lo_ref, tot_ref, x_ref, gate_ref, col_ref, g_ref, b_ref, eo_hbm, o_ref,
                    buf_ref, sem):
    i = pl.program_id(0)
    tm = x_ref.shape[0]
    rows = buf_ref.shape[1]
    slot = i % 2

    @pl.when(i == 0)
    def _():
        buf_ref[...] = jnp.zeros_like(buf_ref)
        _chunk_copies(0, hs_ref, nch_ref, lo_ref, eo_hbm, buf_ref.at[0], sem.at[0], False)

    @pl.when(i + 1 < pl.num_programs(0))
    def _():
        _chunk_copies(i + 1, hs_ref, nch_ref, lo_ref, eo_hbm, buf_ref.at[1 - slot], sem.at[1 - slot], False)

    _chunk_waits(i, tot_ref, eo_hbm, buf_ref.at[slot], sem.at[slot], False)

    gates = gate_ref[...]
    cols = col_ref[...]
    lane = lax.broadcasted_iota(jnp.int32, gates.shape, 1)
    r = lax.broadcasted_iota(jnp.int32, (tm, rows), 1).astype(F32)
    w = jnp.zeros((tm, rows), F32)
    for k in range(TOP_K):
        gk = jnp.sum(jnp.where(lane == k, gates, 0.0), -1, keepdims=True)
        ck = jnp.sum(jnp.where(lane == k, cols, 0.0), -1, keepdims=True)
        w = jnp.where(ck == r, gk, w)
    hi, lo = _unpack_halves(buf_ref[slot])
    vals = jnp.concatenate([hi.astype(BF16), lo.astype(BF16)], axis=1)
    y = jnp.dot(w.astype(BF16), vals, preferred_element_type=F32)
    o_ref[...] = _layer_norm(DEEPNORM_ALPHA * x_ref[...] + y, g_ref[...], b_ref[...])


def _combine(x, gates, col_lanes, g, b, chunks, eo, tm):
    N, D = x.shape
    pb = _tile_buffer_rows(tm)
    return pl.pallas_call(
        _combine_kernel,
        grid_spec=pltpu.PrefetchScalarGridSpec(
            num_scalar_prefetch=4,
            grid=(N // tm,),
            in_specs=[pl.BlockSpec((tm, D), lambda i, *_: (i, 0)),
                      pl.BlockSpec((tm, LANES), lambda i, *_: (i, 0)),
                      pl.BlockSpec((tm, LANES), lambda i, *_: (i, 0)),
                      pl.BlockSpec((1, D), lambda i, *_: (0, 0)),
                      pl.BlockSpec((1, D), lambda i, *_: (0, 0)),
                      pl.BlockSpec(memory_space=pl.ANY)],
            out_specs=pl.BlockSpec((tm, D), lambda i, *_: (i, 0)),
            scratch_shapes=[pltpu.VMEM((2, pb, D // 2), jnp.uint32),
                            pltpu.SemaphoreType.DMA((2,))]),
        out_shape=jax.ShapeDtypeStruct((N, D), F32),
        compiler_params=_cparams(("arbitrary",)),
        name="moe_combine",
    )(*chunks, x, gates, col_lanes, g, b, eo)


def _moe(x1, x1b, gates, route, tile_counts, w1, b1, w2, b2, g, b, layer, tm):
    N = x1.shape[0]
    max_tiles = (N * TOP_K + CHUNK * N_EXPERTS * (N // tm)) // EXPERT_TILE + N_EXPERTS
    chunks, col_lanes, col_rows, tile_expert, n_tiles, zero_start, zero_valid = _route_tables(
        route, tile_counts, tm, max_tiles)
    xs = _dispatch(x1b, col_rows, chunks, zero_start, zero_valid, max_tiles * EXPERT_TILE, tm)
    eo = _experts(xs, tile_expert, n_tiles, w1, b1, w2, b2, layer)
    return _combine(x1, gates, col_lanes, g, b, chunks, eo, tm)


def kernel(x, mem, w_in_a, rpb_a, w_in_b, q_norm_b, k_norm_b, w_in_c, conv_w_c, conv_b_c, filt_w1, filt_b1, filt_w2, filt_b2, filt_w3, filt_b3, filt_w4, filt_b4, filt_freq, long_bias_c, w_mem_kv, w_out, ln1_g, ln1_b, ln2_g, ln2_b, router_w, router_b, moe_w1, moe_b1, moe_w2, moe_b2):
    B, L, D = x.shape
    N = B * L
    M = mem.shape[1]
    tm = min(512, L)
    hy = dict(filt_w1=filt_w1, filt_b1=filt_b1, filt_w2=filt_w2, filt_b2=filt_b2, filt_w3=filt_w3,
              filt_b3=filt_b3, filt_w4=filt_w4, filt_b4=filt_b4, filt_freq=filt_freq,
              conv_w_c=conv_w_c, conv_b_c=conv_b_c, long_bias_c=long_bias_c)

    n_layers = w_mem_kv.shape[0]
    wkv = jnp.transpose(w_mem_kv, (1, 0, 2)).reshape(D, n_layers * 2 * MEM_WIDTH).astype(BF16)
    kv_all = _matmul(mem.reshape(B * M, D), wkv, BF16, 512, 512).reshape(B, M, n_layers, 2, MEM_WIDTH)

    w1_all = moe_w1.astype(F32)
    w2_all = moe_w2.astype(F32)
    b1_all = moe_b1.astype(F32)[:, :, None, :]
    b2_all = moe_b2.astype(F32)[:, :, None, :]

    q_cols, k_cols, out_rows_b = _gqa_column_perms()

    xf = x.reshape(N, D).astype(F32)
    for i in range(DEPTH):
        kind = i % N_MIXERS
        j = i // N_MIXERS
        wo = w_out[i]
        if kind == 0:
            proj = _matmul(xf, w_in_a[j].astype(BF16), BF16, tm, 3 * MIX_WIDTH + MEM_WIDTH)
            bias_tab = _na_bias_table(rpb_a[j], L // GRID_W)
            mix = _na_attention(proj.reshape(B, L, -1), bias_tab, B, L).reshape(N, MIX_WIDTH)
            qm_blk = (3 * MIX_WIDTH) // MEM_WIDTH
        elif kind == 1:
            w = w_in_b[j]
            w = jnp.concatenate([w[:, :MIX_WIDTH][:, q_cols],
                                 w[:, MIX_WIDTH:MIX_WIDTH + KV_WIDTH][:, k_cols],
                                 w[:, MIX_WIDTH + KV_WIDTH:]], axis=1)
            proj = _matmul(xf, w.astype(BF16), BF16, tm, w.shape[1])
            cos_t, sin_t = _rope_lane_tables(L)
            dim, _ = _pair_lane_dims()
            gq = q_norm_b[j].astype(F32)[dim][None, :]
            gk = k_norm_b[j].astype(F32)[dim][None, :]
            q_rot, k_rot = _qk_prep(proj, cos_t, sin_t, gq, gk, L, tm)
            mix = _gqa_attention(q_rot.reshape(B, L, -1), k_rot.reshape(B, L, -1), proj.reshape(B, L, -1),
                                 B, L, min(128, L)).reshape(N, MIX_WIDTH)
            wo = jnp.concatenate([wo[:MIX_WIDTH][out_rows_b], wo[MIX_WIDTH:]], axis=0)
            qm_blk = (MIX_WIDTH + 2 * KV_WIDTH) // MEM_WIDTH
        else:
            proj = _matmul(xf, w_in_c[j].astype(BF16), BF16, tm, HYENA_WIDTH + MEM_WIDTH)
            mix = _hyena_mixer(proj.reshape(B, L, -1), j, hy, B, L).reshape(N, MIX_WIDTH)
            qm_blk = HYENA_WIDTH // MEM_WIDTH
        kmem = kv_all[:, :, i, 0, :]
        vmem = kv_all[:, :, i, 1, :]
        rw = jnp.pad(router_w[i].astype(F32), ((0, 0), (0, LANES - N_EXPERTS)))
        rw_hi = rw.astype(BF16)
        rw = jnp.concatenate([rw_hi, (rw - rw_hi.astype(F32)).astype(BF16)], axis=1)
        rb = jnp.pad(router_b[i].astype(F32), (0, LANES - N_EXPERTS), constant_values=NEG_BIG)[None, :]
        x1, x1b, gates, route, counts = _post_mixer(
            xf, mix, proj, qm_blk, kmem, vmem, wo.astype(BF16),
            ln1_g[i].astype(F32)[None, :], ln1_b[i].astype(F32)[None, :], rw, rb, L, tm)
        xf = _moe(x1, x1b, gates, route, counts, w1_all, b1_all, w2_all, b2_all,
                  ln2_g[i].astype(F32)[None, :], ln2_b[i].astype(F32)[None, :], i, tm)
    return xf.reshape(B, L, D).astype(x.dtype)
```

```python
import functools
import math

import numpy as np
import jax
import jax.numpy as jnp
from jax import lax
from jax.experimental import pallas as pl
from jax.experimental.pallas import tpu as pltpu

F32 = jnp.float32
BF16 = jnp.bfloat16

D_MODEL = 1024
DEPTH = 4
GRID_W = 64
N_MIXERS = 3
HEAD_DIM = 64
MIX_WIDTH = 768
MIX_HEADS = MIX_WIDTH // HEAD_DIM
MEM_HEADS = 4
MEM_WIDTH = MEM_HEADS * HEAD_DIM
NA_KH = 8
NA_KW = 16
NA_ROW_UNROLL = 8
KV_HEADS = 4
KV_WIDTH = KV_HEADS * HEAD_DIM
GQA_GROUP = MIX_HEADS // KV_HEADS
ROPE_THETA = 10000.0
QK_NORM_EPS = 1e-6
HYENA_ORDER = 2
HYENA_BANDS = 16
HYENA_EMB = 2 * HYENA_BANDS + 1
HYENA_FFN = 64
HYENA_WIDTH = (HYENA_ORDER + 1) * MIX_WIDTH
HYENA_DECAY_TARGET = 1e-2
HYENA_FAST_DECAY = 0.3
HYENA_SLOW_DECAY = 1.5
N_EXPERTS = 32
TOP_K = 4
D_EXPERT = D_MODEL
SWIGLU_LIMIT = 7.0
SWIGLU_ALPHA = 1.702
DEEPNORM_ALPHA = (2 * DEPTH) ** 0.25
LN_EPS = 1e-5

LANES = 128
SUBLANES = 8
VMEM_LIMIT = 56 * 1024 * 1024
NEG_BIG = -1e30
ATT_SCALE = HEAD_DIM ** -0.5


def _cparams(sem):
    return pltpu.CompilerParams(dimension_semantics=sem, vmem_limit_bytes=VMEM_LIMIT)


def _layer_norm(x, g, b):
    mu = jnp.mean(x, -1, keepdims=True)
    xc = x - mu
    var = jnp.mean(xc * xc, -1, keepdims=True)
    return xc * lax.rsqrt(var + LN_EPS) * g + b


def _mm_kernel(x_ref, w_ref, o_ref):
    o_ref[...] = jnp.dot(x_ref[...].astype(BF16), w_ref[...],
                         preferred_element_type=F32).astype(o_ref.dtype)


def _matmul(x, w, out_dtype, tm, tn):
    M, K = x.shape
    N = w.shape[1]
    tm = min(tm, M)
    tn = min(tn, N)
    assert M % tm == 0 and N % tn == 0
    return pl.pallas_call(
        _mm_kernel,
        grid=(M // tm, N // tn),
        in_specs=[pl.BlockSpec((tm, K), lambda i, j: (i, 0)),
                  pl.BlockSpec((K, tn), lambda i, j: (0, j))],
        out_specs=pl.BlockSpec((tm, tn), lambda i, j: (i, j)),
        out_shape=jax.ShapeDtypeStruct((M, N), out_dtype),
        compiler_params=_cparams(("parallel", "arbitrary")),
        name="matmul",
    )(x, w)


def _na_bias_table(rpb, rows):
    del rows
    col = np.arange(GRID_W)
    c0 = np.clip(col - NA_KW // 2, 0, GRID_W - NA_KW)
    cc = np.arange(GRID_W)
    valid = (cc[None, :] >= c0[:, None]) & (cc[None, :] < c0[:, None] + NA_KW)
    coff = np.clip(cc[None, :] - col[:, None] + NA_KW - 1, 0, 2 * NA_KW - 2)
    d = np.arange(NA_KH)
    i = np.arange(NA_KH)
    roff = i[None, :] - d[:, None] + NA_KH - 1
    t = rpb.astype(F32)[:, roff]
    t = t[:, :, :, coff]
    t = t.transpose(0, 1, 3, 2, 4)
    t = jnp.where(valid[None, None, :, None, :], t, NEG_BIG)
    return t.reshape(rpb.shape[0], NA_KH, GRID_W, NA_KH * GRID_W)


def _na_kernel(q_ref, k_ref, v_ref, bias_ref, o_ref, *, rows):
    lane = lax.broadcasted_iota(jnp.int32, (GRID_W, LANES), 1)
    first = lane < HEAD_DIM
    band = NA_KH * GRID_W

    def body(r, carry):
        r0 = jnp.clip(r - NA_KH // 2, 0, rows - NA_KH)
        d = r - r0
        qoff = pl.multiple_of(r * GRID_W, GRID_W)
        koff = pl.multiple_of(r0 * GRID_W, GRID_W)
        q = q_ref[pl.ds(qoff, GRID_W), :]
        kb = k_ref[pl.ds(koff, band), :]
        vb = v_ref[pl.ds(koff, band), :]
        zero = jnp.zeros_like(q)
        q2 = jnp.concatenate([jnp.where(first, q, zero), jnp.where(first, zero, q)], axis=0)
        s = lax.dot_general(q2, kb, (((1,), (1,)), ((), ())), preferred_element_type=F32)
        bias = jnp.concatenate([bias_ref[0, d], bias_ref[1, d]], axis=0)
        s = s * ATT_SCALE + bias
        m = jnp.max(s, -1, keepdims=True)
        e = jnp.exp(s - m)
        l = jnp.sum(e, -1, keepdims=True)
        o2 = jnp.dot(e.astype(BF16), vb, preferred_element_type=F32) / l
        o = jnp.where(first, o2[:GRID_W], o2[GRID_W:])
        o_ref[pl.ds(qoff, GRID_W), :] = o.astype(o_ref.dtype)
        return carry

    lax.fori_loop(0, rows, body, 0, unroll=NA_ROW_UNROLL)


def _na_attention(proj, bias_tab, B, L):
    rows = L // GRID_W
    assert rows >= NA_KH
    npair = MIX_WIDTH // LANES
    return pl.pallas_call(
        functools.partial(_na_kernel, rows=rows),
        grid=(npair, B),
        in_specs=[pl.BlockSpec((None, L, LANES), lambda p, b: (b, 0, p)),
                  pl.BlockSpec((None, L, LANES), lambda p, b: (b, 0, npair + p)),
                  pl.BlockSpec((None, L, LANES), lambda p, b: (b, 0, 2 * npair + p)),
                  pl.BlockSpec((2, NA_KH, GRID_W, NA_KH * GRID_W), lambda p, b: (p, 0, 0, 0))],
        out_specs=pl.BlockSpec((None, L, LANES), lambda p, b: (b, 0, p)),
        out_shape=jax.ShapeDtypeStruct((B, L, MIX_WIDTH), BF16),
        compiler_params=_cparams(("parallel", "parallel")),
        name="na_attention",
    )(proj, proj, proj, bias_tab)


def _pair_lane_dims():
    l = np.arange(LANES)
    return 2 * (l % 32) + (l >= HEAD_DIM).astype(np.int64), ((l % HEAD_DIM) >= 32).astype(np.int64)


def _gqa_column_perms():
    dim, which = _pair_lane_dims()
    q_cols = []
    out_rows = []
    for P in range(KV_HEADS // 2):
        for g in range(GQA_GROUP):
            heads = ((2 * P) * GQA_GROUP + g, (2 * P + 1) * GQA_GROUP + g)
            q_cols.append(np.array([heads[w] * HEAD_DIM + dd for dd, w in zip(dim, which)]))
            out_rows.append(np.concatenate([heads[0] * HEAD_DIM + np.arange(HEAD_DIM),
                                            heads[1] * HEAD_DIM + np.arange(HEAD_DIM)]))
    k_cols = []
    for P in range(KV_HEADS // 2):
        heads = (2 * P, 2 * P + 1)
        k_cols.append(np.array([heads[w] * HEAD_DIM + dd for dd, w in zip(dim, which)]))
    return np.concatenate(q_cols), np.concatenate(k_cols), np.concatenate(out_rows)


def _rope_lane_tables(L):
    pos = jnp.arange(L)
    row = (pos // GRID_W).astype(F32)
    col = (pos % GRID_W).astype(F32)
    axis_dim = HEAD_DIM // 2
    inv_freq = ROPE_THETA ** (-jnp.arange(0, axis_dim, 2, dtype=F32) / axis_dim)
    ang = jnp.concatenate([row[:, None] * inv_freq, col[:, None] * inv_freq], -1)
    c = jnp.cos(ang)
    s = jnp.sin(ang)
    return jnp.concatenate([c, c, c, c], -1), jnp.concatenate([-s, -s, s, s], -1)


def _qk_prep_kernel(q_ref, k_ref, cos_ref, sin_ref, gq_ref, gk_ref, qo_ref, ko_ref):
    tm = q_ref.shape[0]
    lane = lax.broadcasted_iota(jnp.int32, (tm, LANES), 1)
    head_a = (lane % HEAD_DIM) < 32
    c = cos_ref[...]
    s = sin_ref[...]

    def norm_rope(x, g):
        x = x.astype(F32)
        x2 = x * x
        tot = jnp.sum(x2, -1, keepdims=True)
        sa = jnp.sum(jnp.where(head_a, x2, 0.0), -1, keepdims=True)
        inv = jnp.where(head_a, lax.rsqrt(sa * (1.0 / HEAD_DIM) + QK_NORM_EPS),
                        lax.rsqrt((tot - sa) * (1.0 / HEAD_DIM) + QK_NORM_EPS))
        xn = x * inv * g
        return xn * c + pltpu.roll(xn, HEAD_DIM, axis=1) * s

    for j in range(q_ref.shape[1] // LANES):
        sl = slice(j * LANES, (j + 1) * LANES)
        qo_ref[:, sl] = (norm_rope(q_ref[:, sl], gq_ref[...]) * ATT_SCALE).astype(qo_ref.dtype)
    for j in range(k_ref.shape[1] // LANES):
        sl = slice(j * LANES, (j + 1) * LANES)
        ko_ref[:, sl] = norm_rope(k_ref[:, sl], gk_ref[...]).astype(ko_ref.dtype)


def _qk_prep(proj, cos_t, sin_t, gq, gk, L, tm):
    N = proj.shape[0]
    nl = L // tm
    return pl.pallas_call(
        _qk_prep_kernel,
        grid=(N // tm,),
        in_specs=[pl.BlockSpec((tm, MIX_WIDTH), lambda i: (i, 0)),
                  pl.BlockSpec((tm, KV_WIDTH), lambda i: (i, MIX_WIDTH // KV_WIDTH)),
                  pl.BlockSpec((tm, LANES), lambda i: (i % nl, 0)),
                  pl.BlockSpec((tm, LANES), lambda i: (i % nl, 0)),
                  pl.BlockSpec((1, LANES), lambda i: (0, 0)),
                  pl.BlockSpec((1, LANES), lambda i: (0, 0))],
        out_specs=[pl.BlockSpec((tm, MIX_WIDTH), lambda i: (i, 0)),
                   pl.BlockSpec((tm, KV_WIDTH), lambda i: (i, 0))],
        out_shape=[jax.ShapeDtypeStruct((N, MIX_WIDTH), BF16),
                   jax.ShapeDtypeStruct((N, KV_WIDTH), BF16)],
        compiler_params=_cparams(("parallel",)),
        name="qk_prep",
    )(proj, proj, cos_t, sin_t, gq, gk)


GQA_KV_CHUNK = 1024
GQA_STREAMS = 1


def _gqa_kernel(q_ref, k_ref, v_ref, o_ref, q6_ref, s_ref, m_ref, l_ref, acc_ref):
    ns = q6_ref.shape[0]
    tq = q_ref.shape[0] // ns
    tk = s_ref.shape[3]
    nk = k_ref.shape[0] // tk
    lane = lax.broadcasted_iota(jnp.int32, (tq, LANES), 1)
    head_a = (lane % HEAD_DIM) < 32
    for st in range(ns):
        q = q_ref[st * tq:(st + 1) * tq, :]
        for g in range(GQA_GROUP):
            qs = q[:, g * LANES:(g + 1) * LANES]
            zero = jnp.zeros_like(qs)
            q6_ref[st, (2 * g) * tq:(2 * g + 1) * tq, :] = jnp.where(head_a, qs, zero)
            q6_ref[st, (2 * g + 1) * tq:(2 * g + 2) * tq, :] = jnp.where(head_a, zero, qs)
    m_ref[...] = jnp.full(m_ref.shape, -jnp.inf, F32)
    l_ref[...] = jnp.zeros(l_ref.shape, F32)
    acc_ref[...] = jnp.zeros(acc_ref.shape, F32)

    def scores(st, j):
        return lax.dot_general(q6_ref[st], k_ref[j * tk:(j + 1) * tk, :], (((1,), (1,)), ((), ())),
                               preferred_element_type=F32)

    for st in range(ns):
        s_ref[st, 0] = scores(st, 0)
    for j in range(nk):
        for st in range(ns):
            if j + 1 < nk:
                s_ref[st, (j + 1) % 2] = scores(st, j + 1)
            s = s_ref[st, j % 2]
            m_old = m_ref[st]
            m_new = jnp.maximum(m_old, jnp.max(s, -1, keepdims=True))
            alpha = jnp.exp(m_old - m_new)
            p = jnp.exp(s - jnp.concatenate([m_new] * (tk // LANES), axis=1))
            l_ref[st] = alpha * l_ref[st] + jnp.sum(p, -1, keepdims=True)
            acc_ref[st] = alpha * acc_ref[st] + jnp.dot(p.astype(BF16), v_ref[j * tk:(j + 1) * tk, :],
                                                        preferred_element_type=F32)
            m_ref[st] = m_new
    for st in range(ns):
        o6 = acc_ref[st] / l_ref[st]
        for g in range(GQA_GROUP):
            oa = o6[(2 * g) * tq:(2 * g + 1) * tq]
            ob = o6[(2 * g + 1) * tq:(2 * g + 2) * tq]
            o_ref[st * tq:(st + 1) * tq, g * LANES:(g + 1) * LANES] = (
                jnp.where(lane < HEAD_DIM, oa, ob).astype(o_ref.dtype))


def _gqa_attention(q_rot, k_rot, proj, B, L, tq):
    slab_w = GQA_GROUP * LANES
    v_blk0 = (MIX_WIDTH + KV_WIDTH) // LANES
    rows6 = 2 * GQA_GROUP * tq
    tk = min(GQA_KV_CHUNK, L)
    ns = GQA_STREAMS if L % (GQA_STREAMS * tq) == 0 else 1
    tqb = ns * tq
    return pl.pallas_call(
        _gqa_kernel,
        grid=(B, KV_HEADS // 2, L // tqb),
        in_specs=[pl.BlockSpec((None, tqb, slab_w), lambda b, p, i: (b, i, p)),
                  pl.BlockSpec((None, L, LANES), lambda b, p, i: (b, 0, p)),
                  pl.BlockSpec((None, L, LANES), lambda b, p, i: (b, 0, v_blk0 + p))],
        out_specs=pl.BlockSpec((None, tqb, slab_w), lambda b, p, i: (b, i, p)),
        out_shape=jax.ShapeDtypeStruct((B, L, MIX_WIDTH), BF16),
        scratch_shapes=[pltpu.VMEM((ns, rows6, LANES), BF16),
                        pltpu.VMEM((ns, 2, rows6, tk), F32),
                        pltpu.VMEM((ns, rows6, LANES), F32),
                        pltpu.VMEM((ns, rows6, LANES), F32),
                        pltpu.VMEM((ns, rows6, LANES), F32)],
        compiler_params=_cparams(("parallel", "parallel", "arbitrary")),
        name="gqa_attention",
    )(q_rot, k_rot, proj)


def _hyena_features(L):
    pos = jnp.arange(L, dtype=F32)
    t = pos / max(L - 1, 1)
    w = 2.0 * math.pi * pos / L
    f = jnp.linspace(1e-4, HYENA_BANDS - 1, HYENA_BANDS, dtype=F32)
    wf = w[:, None] * f[None, :]
    z = jnp.concatenate([t[:, None], jnp.cos(wf), -jnp.sin(wf)], -1)
    return jnp.pad(z, ((0, 0), (0, LANES - HYENA_EMB))), t[:, None]


def _filter_kernel(z_ref, t_ref, w1_ref, b1_ref, w2_ref, b2_ref, w3_ref, b3_ref, w4f_ref, w4b_ref,
                   b4f_ref, b4b_ref, fr_ref, delta_ref, hsum_ref, hdiff_ref):
    hp = lax.Precision.HIGHEST
    fr = fr_ref[...]
    h = jnp.sin(fr * (jnp.dot(z_ref[...], w1_ref[...], precision=hp, preferred_element_type=F32) + b1_ref[...]))
    h = jnp.sin(fr * (jnp.dot(h, w2_ref[...], precision=hp, preferred_element_type=F32) + b2_ref[...]))
    h = jnp.sin(fr * (jnp.dot(h, w3_ref[...], precision=hp, preferred_element_type=F32) + b3_ref[...]))
    hf = jnp.dot(h, w4f_ref[...], precision=hp, preferred_element_type=F32) + b4f_ref[...]
    hb = jnp.dot(h, w4b_ref[...], precision=hp, preferred_element_type=F32) + b4b_ref[...]
    dec = jnp.exp(-t_ref[...] * delta_ref[...])
    hf = hf * dec
    hb = hb * dec
    norm = jnp.sum(jnp.abs(hf), 0, keepdims=True) + jnp.sum(jnp.abs(hb), 0, keepdims=True)
    inv = 1.0 / norm
    hf = hf * inv
    row = lax.broadcasted_iota(jnp.int32, hb.shape, 0)
    hb = jnp.where(row == 0, 0.0, hb * inv)
    hsum_ref[...] = hf + hb
    hdiff_ref[...] = hf - hb


def _hyena_filters(zf, t, w1, b1, w2, b2, w3, b3, w4, b4, freq, L, tc):
    OC = HYENA_ORDER * MIX_WIDTH
    max_decay = math.log(HYENA_DECAY_TARGET) / HYENA_FAST_DECAY
    min_decay = math.log(HYENA_DECAY_TARGET) / HYENA_SLOW_DECAY
    deltas = jnp.abs(jnp.linspace(min_decay, max_decay, MIX_WIDTH, dtype=F32))
    deltas = jnp.tile(deltas, HYENA_ORDER)[None, :]
    w1p = jnp.pad(w1.astype(F32), ((0, LANES - HYENA_EMB), (0, 0)))
    row = lambda a: a.astype(F32)[None, :]
    full = lambda shape: pl.BlockSpec(shape, lambda j: (0,) * len(shape))
    nb = OC // tc
    return pl.pallas_call(
        _filter_kernel,
        grid=(nb,),
        in_specs=[full((L, LANES)), full((L, 1)),
                  full((LANES, HYENA_FFN)), full((1, HYENA_FFN)),
                  full((HYENA_FFN, HYENA_FFN)), full((1, HYENA_FFN)),
                  full((HYENA_FFN, HYENA_FFN)), full((1, HYENA_FFN)),
                  pl.BlockSpec((HYENA_FFN, tc), lambda j: (0, j)),
                  pl.BlockSpec((HYENA_FFN, tc), lambda j: (0, nb + j)),
                  pl.BlockSpec((1, tc), lambda j: (0, j)),
                  pl.BlockSpec((1, tc), lambda j: (0, nb + j)),
                  full((1, HYENA_FFN)),
                  pl.BlockSpec((1, tc), lambda j: (0, j))],
        out_specs=[pl.BlockSpec((L, tc), lambda j: (0, j)),
                   pl.BlockSpec((L, tc), lambda j: (0, j))],
        out_shape=[jax.ShapeDtypeStruct((L, OC), F32), jax.ShapeDtypeStruct((L, OC), F32)],
        compiler_params=_cparams(("parallel",)),
        name="hyena_filters",
    )(zf, t, w1p, row(b1), w2.astype(F32), row(b2), w3.astype(F32), row(b3), w4.astype(F32), w4.astype(F32),
      row(b4), row(b4), row(freq), deltas)


def _dft_matrices(L):
    n = 2 * L
    k = jnp.arange(L, dtype=jnp.int32)
    kb = 64 if L % 64 == 0 else 1
    s = k[None, :]

    def table(mult):
        ang = ((mult[:, None] * s) % n).astype(F32) * (2.0 * math.pi / n)
        return jnp.cos(ang), jnp.sin(ang)

    ca, sa = table(jnp.arange(L // kb, dtype=jnp.int32) * kb)
    cb, sb = table(jnp.arange(kb, dtype=jnp.int32))
    fc = (ca[:, None, :] * cb[None, :, :] - sa[:, None, :] * sb[None, :, :]).reshape(L, L)
    sn = (sa[:, None, :] * cb[None, :, :] + ca[:, None, :] * sb[None, :, :]).reshape(L, L)
    alt = jnp.where(k % 2 == 0, 1.0, -1.0).astype(F32)
    fs = jnp.where(k[:, None] == 0, alt[None, :], sn)
    fst = jnp.where(k[None, :] == 0, alt[:, None], sn)
    return fc.astype(BF16), fs.astype(BF16), fst.astype(BF16)


def _short_conv_kernel(u_ref, w_ref, b_ref, o_ref):
    L = u_ref.shape[0]
    u = u_ref[...].astype(F32)
    row = lax.broadcasted_iota(jnp.int32, u.shape, 0)
    prev = jnp.where(row == 0, 0.0, pltpu.roll(u, 1, axis=0))
    nxt = jnp.where(row == L - 1, 0.0, pltpu.roll(u, L - 1, axis=0))
    w = w_ref[...]
    o_ref[...] = (prev * w[0:1] + u * w[1:2] + nxt * w[2:3] + b_ref[...]).astype(o_ref.dtype)


def _short_conv(proj, conv_w, conv_b, B, L, tc):
    return pl.pallas_call(
        _short_conv_kernel,
        grid=(B, HYENA_WIDTH // tc),
        in_specs=[pl.BlockSpec((None, L, tc), lambda b, j: (b, 0, j)),
                  pl.BlockSpec((3, tc), lambda b, j: (0, j)),
                  pl.BlockSpec((1, tc), lambda b, j: (0, j))],
        out_specs=pl.BlockSpec((None, L, tc), lambda b, j: (b, 0, j)),
        out_shape=jax.ShapeDtypeStruct((B, L, HYENA_WIDTH), BF16),
        compiler_params=_cparams(("parallel", "parallel")),
        name="short_conv",
    )(proj, conv_w.astype(F32), conv_b.astype(F32)[None, :])


def _hy_fwd_kernel(fc_ref, fs_ref, z_ref, pa_ref, qa_ref, qb_ref, sb_ref, ya_ref, yb_ref):
    z = z_ref[...]
    a = jnp.dot(fc_ref[...], z, preferred_element_type=F32)
    b = jnp.dot(fs_ref[...], z, preferred_element_type=F32)
    ya_ref[...] = (a * pa_ref[...] - b * qa_ref[...]).astype(ya_ref.dtype)
    yb_ref[...] = (a * qb_ref[...] + b * sb_ref[...]).astype(yb_ref.dtype)


def _hy_inv_kernel(fc_ref, fst_ref, ya_ref, yb_ref, z_ref, xg_ref, bias_ref, o_ref):
    y = jnp.dot(fc_ref[...], ya_ref[...], preferred_element_type=F32)
    y = y + jnp.dot(fst_ref[...], yb_ref[...], preferred_element_type=F32)
    z = z_ref[...].astype(F32)
    o_ref[...] = (xg_ref[...].astype(F32) * (y + bias_ref[...] * z)).astype(o_ref.dtype)


def _hyena_long_conv_gate(uc, z_arr, z_blk, xg_blk, fc, fs, fst, coef, bias, B, L, tm_f, tm_i):
    C = MIX_WIDTH
    pa, qa, qb, sb = coef
    ya, yb = pl.pallas_call(
        _hy_fwd_kernel,
        grid=(B, L // tm_f),
        in_specs=[pl.BlockSpec((tm_f, L), lambda b, i: (i, 0)),
                  pl.BlockSpec((tm_f, L), lambda b, i: (i, 0)),
                  pl.BlockSpec((None, L, C), lambda b, i: (b, 0, z_blk)),
                  pl.BlockSpec((tm_f, C), lambda b, i: (i, 0)),
                  pl.BlockSpec((tm_f, C), lambda b, i: (i, 0)),
                  pl.BlockSpec((tm_f, C), lambda b, i: (i, 0)),
                  pl.BlockSpec((tm_f, C), lambda b, i: (i, 0))],
        out_specs=[pl.BlockSpec((None, tm_f, C), lambda b, i: (b, i, 0)),
                   pl.BlockSpec((None, tm_f, C), lambda b, i: (b, i, 0))],
        out_shape=[jax.ShapeDtypeStruct((B, L, C), BF16), jax.ShapeDtypeStruct((B, L, C), BF16)],
        compiler_params=_cparams(("parallel", "arbitrary")),
        name="hyena_dft_fwd",
    )(fc, fs, z_arr, pa, qa, qb, sb)
    return pl.pallas_call(
        _hy_inv_kernel,
        grid=(B, L // tm_i),
        in_specs=[pl.BlockSpec((tm_i, L), lambda b, i: (i, 0)),
                  pl.BlockSpec((tm_i, L), lambda b, i: (i, 0)),
                  pl.BlockSpec((None, L, C), lambda b, i: (b, 0, 0)),
                  pl.BlockSpec((None, L, C), lambda b, i: (b, 0, 0)),
                  pl.BlockSpec((None, tm_i, C), lambda b, i: (b, i, z_blk)),
                  pl.BlockSpec((None, tm_i, C), lambda b, i: (b, i, xg_blk)),
                  pl.BlockSpec((1, C), lambda b, i: (0, 0))],
        out_specs=pl.BlockSpec((None, tm_i, C), lambda b, i: (b, i, 0)),
        out_shape=jax.ShapeDtypeStruct((B, L, C), BF16),
        compiler_params=_cparams(("parallel", "arbitrary")),
        name="hyena_dft_inv",
    )(fc, fst, ya, yb, z_arr, uc, bias)


def _hyena_mixer(proj, j, p, B, L):
    C = MIX_WIDTH
    OC = HYENA_ORDER * C
    n = 2 * L
    zf, t = _hyena_features(L)
    hsum, hdiff = _hyena_filters(zf, t, p['filt_w1'][j], p['filt_b1'][j], p['filt_w2'][j], p['filt_b2'][j],
                                 p['filt_w3'][j], p['filt_b3'][j], p['filt_w4'][j], p['filt_b4'][j],
                                 p['filt_freq'][j], L, 256)
    fc, fs, fst = _dft_matrices(L)
    def hi_lo(h):
        hi = h.astype(BF16)
        return jnp.concatenate([hi, (h - hi.astype(F32)).astype(BF16)], axis=1)

    hs_cat = hi_lo(hsum)
    sc = _matmul(fc, hs_cat, F32, 512, 512)
    ss = _matmul(fs, hi_lo(hdiff), F32, 512, 512)
    ny = _matmul(fs[:2 * SUBLANES], hs_cat, F32, 2 * SUBLANES, 512)
    gc = sc[:, :OC] + sc[:, OC:]
    nyq = ny[0:1, :OC] + ny[0:1, OC:]
    gs = ss[:, :OC] + ss[:, OC:]
    k0 = (jnp.arange(L) == 0)[:, None]
    ca = jnp.where(k0, 1.0 / n, 2.0 / n).astype(F32)
    pa = ca * gc
    qa = ca * jnp.where(k0, 0.0, gs)
    qb = qa
    sb = ca * jnp.where(k0, nyq, gc)
    uc = _short_conv(proj, p['conv_w_c'][j], p['conv_b_c'][j], B, L, 256)
    bias = p['long_bias_c'][j].astype(F32)
    tm_f = min(512, L)
    tm_i = min(256, L)
    z = uc
    z_blk = 0
    for o in range(HYENA_ORDER):
        sl = slice(o * C, (o + 1) * C)
        coef = (pa[:, sl], qa[:, sl], qb[:, sl], sb[:, sl])
        z = _hyena_long_conv_gate(uc, z, z_blk, o + 1, fc, fs, fst, coef, bias[o][None, :], B, L, tm_f, tm_i)
        z_blk = 0
    return z


def _pack_halves(x):
    c = x.shape[1] // 2
    bits = lax.bitcast_convert_type(x.astype(BF16).astype(F32), jnp.uint32)
    return (bits[:, :c] & jnp.uint32(0xFFFF0000)) | (bits[:, c:] >> 16)


def _unpack_halves(w):
    hi = lax.bitcast_convert_type(w & jnp.uint32(0xFFFF0000), F32)
    lo = lax.bitcast_convert_type(w << 16, F32)
    return hi, lo


def _post_kernel(x_ref, mix_ref, qm_ref, km_ref, vm_ref, wo_ref, g_ref, b_ref, rw_ref, rb_ref,
                 xo_ref, xb_ref, gate_ref, route_ref, count_ref):
    tm = x_ref.shape[0]
    lane = lax.broadcasted_iota(jnp.int32, (tm, LANES), 1)
    first = lane < HEAD_DIM
    outs = []
    for p in range(MEM_WIDTH // LANES):
        sl = slice(p * LANES, (p + 1) * LANES)
        qs = qm_ref[:, sl]
        zero = jnp.zeros_like(qs)
        q2 = jnp.concatenate([jnp.where(first, qs, zero), jnp.where(first, zero, qs)], axis=0)
        s = lax.dot_general(q2, km_ref[:, sl], (((1,), (1,)), ((), ())), preferred_element_type=F32) * ATT_SCALE
        m = jnp.max(s, -1, keepdims=True)
        e = jnp.exp(s - m)
        l = jnp.sum(e, -1, keepdims=True)
        o2 = jnp.dot(e.astype(BF16), vm_ref[:, sl], preferred_element_type=F32) / l
        outs.append(jnp.where(first, o2[:tm], o2[tm:]).astype(BF16))
    mem_out = jnp.concatenate(outs, axis=1)
    acc = jnp.dot(mix_ref[...], wo_ref[:MIX_WIDTH, :], preferred_element_type=F32)
    acc = acc + jnp.dot(mem_out, wo_ref[MIX_WIDTH:, :], preferred_element_type=F32)
    x1 = _layer_norm(DEEPNORM_ALPHA * x_ref[...] + acc, g_ref[...], b_ref[...])
    xo_ref[...] = x1
    x_hi = x1.astype(BF16)
    xb_ref[...] = x_hi
    x_lo = (x1 - x_hi.astype(F32)).astype(BF16)
    prod = jnp.dot(x_hi, rw_ref[...], preferred_element_type=F32)
    logits = (prod[:, :LANES] + prod[:, LANES:]
              + jnp.dot(x_lo, rw_ref[:, :LANES], preferred_element_type=F32) + rb_ref[...])
    lane_f = lane.astype(F32)
    work = logits
    chosen = jnp.zeros_like(logits)
    gates = jnp.zeros_like(logits)
    denom = jnp.zeros((tm, 1), F32)
    sels, idxs = [], []
    m0 = None
    for k in range(TOP_K):
        m = jnp.max(work, -1, keepdims=True)
        idx = jnp.min(jnp.where(work == m, lane_f, float(LANES)), -1, keepdims=True)
        sel = lane_f == idx
        if k == 0:
            m0 = m
        ek = jnp.exp(m - m0)
        gates = jnp.where(lane == k, ek, gates)
        chosen = chosen + jnp.where(sel, 1.0, 0.0)
        denom = denom + ek
        work = jnp.where(sel, -jnp.inf, work)
        sels.append(sel)
        idxs.append(idx)
    gate_ref[...] = gates / denom

    r_i = lax.broadcasted_iota(jnp.int32, (tm, tm), 0)
    c_i = lax.broadcasted_iota(jnp.int32, (tm, tm), 1)
    before = jnp.where(r_i > c_i, 1.0, 0.0).astype(BF16)
    prefix = jnp.dot(before, chosen.astype(BF16), preferred_element_type=F32)
    route = jnp.zeros((tm, LANES), jnp.int32)
    for k in range(TOP_K):
        rank = jnp.sum(jnp.where(sels[k], prefix, 0.0), -1, keepdims=True)
        route = jnp.where(lane == k, rank.astype(jnp.int32), route)
        route = jnp.where(lane == TOP_K + k, idxs[k].astype(jnp.int32), route)
    route_ref[...] = route
    count_ref[...] = jnp.broadcast_to(jnp.sum(chosen, 0, keepdims=True), count_ref.shape)


def _post_mixer(x, mix, proj, qm_blk, kmem, vmem, wo, g, b, rw, rb, L, tm):
    N, D = x.shape
    nl = L // tm
    CAT = MIX_WIDTH + MEM_WIDTH
    return pl.pallas_call(
        _post_kernel,
        grid=(N // tm,),
        in_specs=[pl.BlockSpec((tm, D), lambda i: (i, 0)),
                  pl.BlockSpec((tm, MIX_WIDTH), lambda i: (i, 0)),
                  pl.BlockSpec((tm, MEM_WIDTH), lambda i: (i, qm_blk)),
                  pl.BlockSpec((None, kmem.shape[1], MEM_WIDTH), lambda i: (i // nl, 0, 0)),
                  pl.BlockSpec((None, vmem.shape[1], MEM_WIDTH), lambda i: (i // nl, 0, 0)),
                  pl.BlockSpec((CAT, D), lambda i: (0, 0)),
                  pl.BlockSpec((1, D), lambda i: (0, 0)),
                  pl.BlockSpec((1, D), lambda i: (0, 0)),
                  pl.BlockSpec((D, 2 * LANES), lambda i: (0, 0)),
                  pl.BlockSpec((1, LANES), lambda i: (0, 0))],
        out_specs=[pl.BlockSpec((tm, D), lambda i: (i, 0)),
                   pl.BlockSpec((tm, D), lambda i: (i, 0)),
                   pl.BlockSpec((tm, LANES), lambda i: (i, 0)),
                   pl.BlockSpec((tm, LANES), lambda i: (i, 0)),
                   pl.BlockSpec((SUBLANES, LANES), lambda i: (i, 0))],
        out_shape=[jax.ShapeDtypeStruct((N, D), F32), jax.ShapeDtypeStruct((N, D), BF16),
                   jax.ShapeDtypeStruct((N, LANES), F32), jax.ShapeDtypeStruct((N, LANES), jnp.int32),
                   jax.ShapeDtypeStruct((N // tm * SUBLANES, LANES), F32)],
        compiler_params=_cparams(("parallel",)),
        name="post_mixer",
    )(x, mix, proj, kmem, vmem, wo, g, b, rw, rb)


EXPERT_TILE = 512
CHUNK = SUBLANES
BIG_CHUNK = 4 * CHUNK
EXPERT_ROW_SPLIT = 1


def _tile_buffer_rows(tm):
    return tm * TOP_K + CHUNK * N_EXPERTS


def _route_tables(route, tile_counts, tm, max_tiles):
    N = route.shape[0]
    nt = N // tm
    cnt = tile_counts.reshape(nt, SUBLANES, LANES)[:, 0, :N_EXPERTS].astype(jnp.int32)
    rc = ((cnt + CHUNK - 1) // CHUNK) * CHUNK
    base = jnp.cumsum(rc, axis=0) - rc
    padded = ((jnp.sum(rc, axis=0) + EXPERT_TILE - 1) // EXPERT_TILE) * EXPERT_TILE
    ends = jnp.cumsum(padded)
    offs = ends - padded
    local = jnp.cumsum(rc, axis=1) - rc
    hbm_start = (offs[None, :] + base).reshape(-1).astype(jnp.int32)
    n_big = rc // BIG_CHUNK
    n_small = (rc % BIG_CHUNK) // CHUNK
    n_chunks = jnp.concatenate([n_big.reshape(-1), n_small.reshape(-1)]).astype(jnp.int32)
    total_chunks = jnp.concatenate([jnp.sum(n_big, axis=1), jnp.sum(n_small, axis=1)]).astype(jnp.int32)
    eid = route[:, TOP_K:2 * TOP_K]
    local_tok = jnp.repeat(local, tm, axis=0)[:, None, :]
    hit = eid[:, :, None] == jnp.arange(N_EXPERTS, dtype=jnp.int32)[None, None, :]
    col = jnp.sum(jnp.where(hit, local_tok, 0), axis=2) + route[:, :TOP_K]
    col_lanes = jnp.pad(col.astype(F32), ((0, 0), (0, LANES - TOP_K)), constant_values=-1.0)
    col_rows = jnp.pad(col.reshape(nt, tm, TOP_K).transpose(0, 2, 1), ((0, 0), (0, SUBLANES - TOP_K), (0, 0)),
                       constant_values=-1).reshape(nt * SUBLANES, tm).astype(jnp.int32)
    tile_ends = ends // EXPERT_TILE
    n_tiles = tile_ends[-1]
    t = jnp.arange(max_tiles, dtype=jnp.int32)
    t_live = jnp.minimum(t, n_tiles - 1)
    tile_expert = jnp.sum((t_live[:, None] >= tile_ends[None, :]).astype(jnp.int32), axis=1)
    tile_expert = jnp.minimum(tile_expert, N_EXPERTS - 1)
    zero_start = (ends - EXPERT_TILE).astype(jnp.int32)
    zero_valid = (padded > 0).astype(jnp.int32)
    chunks = (hbm_start, n_chunks, local.reshape(-1).astype(jnp.int32), total_chunks)
    return chunks, col_lanes, col_rows, tile_expert, n_tiles.reshape(1).astype(jnp.int32), zero_start, zero_valid


def _chunk_copy(hbm_ref, buf_ref, h0, b0, size, sem, to_hbm):
    aligned = lambda v: v if isinstance(v, int) else pl.multiple_of(v, CHUNK)
    h = hbm_ref.at[pl.ds(aligned(h0), size)]
    b = buf_ref.at[pl.ds(aligned(b0), size)]
    return pltpu.make_async_copy(b, h, sem) if to_hbm else pltpu.make_async_copy(h, b, sem)


def _chunk_copies(i, hs_ref, nch_ref, lo_ref, hbm_ref, buf_ref, sem, to_hbm):
    n_pairs = pl.num_programs(0) * N_EXPERTS
    for e in range(N_EXPERTS):
        hs = hs_ref[i * N_EXPERTS + e]
        lo = lo_ref[i * N_EXPERTS + e]
        n_big = nch_ref[i * N_EXPERTS + e]

        def start_big(c, carry, hs=hs, lo=lo):
            _chunk_copy(hbm_ref, buf_ref, hs + c * BIG_CHUNK, lo + c * BIG_CHUNK, BIG_CHUNK, sem, to_hbm).start()
            return carry

        def start_small(c, carry, hs=hs + n_big * BIG_CHUNK, lo=lo + n_big * BIG_CHUNK):
            _chunk_copy(hbm_ref, buf_ref, hs + c * CHUNK, lo + c * CHUNK, CHUNK, sem, to_hbm).start()
            return carry

        lax.fori_loop(0, n_big, start_big, 0)
        lax.fori_loop(0, nch_ref[n_pairs + i * N_EXPERTS + e], start_small, 0)


def _chunk_waits(step, tot_ref, hbm_ref, buf_ref, sem, to_hbm):
    def wait_big(c, carry):
        _chunk_copy(hbm_ref, buf_ref, 0, 0, BIG_CHUNK, sem, to_hbm).wait()
        return carry

    def wait_small(c, carry):
        _chunk_copy(hbm_ref, buf_ref, 0, 0, CHUNK, sem, to_hbm).wait()
        return carry

    lax.fori_loop(0, tot_ref[step], wait_big, 0)
    lax.fori_loop(0, tot_ref[pl.num_programs(0) + step], wait_small, 0)


def _dispatch_kernel(hs_ref, nch_ref, lo_ref, tot_ref, zstart_ref, zvalid_ref, xb_ref, colr_ref, xs_hbm,
                     perm_ref, zero_ref, sem, zero_sem):
    i = pl.program_id(0)
    tm = xb_ref.shape[0]
    rows = perm_ref.shape[1]
    slot = i % 2

    @pl.when(i == 0)
    def _():
        zero_ref[...] = jnp.zeros_like(zero_ref)

        def zero_copy(e):
            start = pl.multiple_of(zstart_ref[e], EXPERT_TILE)
            return pltpu.make_async_copy(zero_ref, xs_hbm.at[pl.ds(start, EXPERT_TILE)], zero_sem)

        for e in range(N_EXPERTS):
            @pl.when(zvalid_ref[e] > 0)
            def _():
                zero_copy(e).start()
        for e in range(N_EXPERTS):
            @pl.when(zvalid_ref[e] > 0)
            def _():
                zero_copy(e).wait()

    r = lax.broadcasted_iota(jnp.int32, (rows, tm), 0)
    onehot = jnp.zeros((rows, tm), F32)
    for k in range(TOP_K):
        onehot = jnp.where(colr_ref[k:k + 1, :] == r, 1.0, onehot)
    perm = jnp.dot(onehot.astype(BF16), xb_ref[...], preferred_element_type=F32)
    perm_ref[slot] = _pack_halves(perm)
    _chunk_copies(i, hs_ref, nch_ref, lo_ref, xs_hbm, perm_ref.at[slot], sem.at[slot], True)

    @pl.when(i > 0)
    def _():
        _chunk_waits(jnp.maximum(i - 1, 0), tot_ref, xs_hbm, perm_ref.at[1 - slot], sem.at[1 - slot], True)

    @pl.when(i == pl.num_programs(0) - 1)
    def _():
        _chunk_waits(i, tot_ref, xs_hbm, perm_ref.at[slot], sem.at[slot], True)


def _dispatch(xb, col_rows, chunks, zero_start, zero_valid, rows, tm):
    N, D = xb.shape
    pb = _tile_buffer_rows(tm)
    return pl.pallas_call(
        _dispatch_kernel,
        grid_spec=pltpu.PrefetchScalarGridSpec(
            num_scalar_prefetch=6,
            grid=(N // tm,),
            in_specs=[pl.BlockSpec((tm, D), lambda i, *_: (i, 0)),
                      pl.BlockSpec((SUBLANES, tm), lambda i, *_: (i, 0))],
            out_specs=pl.BlockSpec(memory_space=pl.ANY),
            scratch_shapes=[pltpu.VMEM((2, pb, D // 2), jnp.uint32),
                            pltpu.VMEM((EXPERT_TILE, D // 2), jnp.uint32),
                            pltpu.SemaphoreType.DMA((2,)),
                            pltpu.SemaphoreType.DMA]),
        out_shape=jax.ShapeDtypeStruct((rows, D // 2), jnp.uint32),
        compiler_params=_cparams(("arbitrary",)),
        name="moe_dispatch",
    )(*chunks, zero_start, zero_valid, xb, col_rows)


def _expert_kernel(te_ref, nt_ref, xs_ref, w1_ref, b1_ref, w2_ref, b2_ref, o_ref, w1b_ref, w2b_ref):
    t = pl.program_id(0)

    @pl.when((t == 0) | (te_ref[t] != te_ref[jnp.maximum(t - 1, 0)]))
    def _():
        w1b_ref[...] = w1_ref[...].astype(BF16)
        w2b_ref[...] = w2_ref[...].astype(BF16)

    @pl.when(t < nt_ref[0])
    def _():
        rows = xs_ref.shape[0] // EXPERT_ROW_SPLIT
        for blk in range(EXPERT_ROW_SPLIT):
            sl = slice(blk * rows, (blk + 1) * rows)
            hi, lo = _unpack_halves(xs_ref[sl, :])
            half = hi.shape[1]
            h = jnp.dot(hi.astype(BF16), w1b_ref[:half, :], preferred_element_type=F32)
            h = h + jnp.dot(lo.astype(BF16), w1b_ref[half:, :], preferred_element_type=F32) + b1_ref[...]
            gl = jnp.minimum(h[:, :D_EXPERT], SWIGLU_LIMIT)
            ul = jnp.clip(h[:, D_EXPERT:], -SWIGLU_LIMIT, SWIGLU_LIMIT)
            a = (ul + 1.0) * (gl * jax.nn.sigmoid(SWIGLU_ALPHA * gl))
            o = jnp.dot(a.astype(BF16), w2b_ref[...], preferred_element_type=F32) + b2_ref[...]
            o_ref[sl, :] = _pack_halves(o)


def _experts(xs, tile_expert, n_tiles, w1, b1, w2, b2, layer):
    rows, W = xs.shape
    D = 2 * W
    last = lambda t, te, nt: jnp.minimum(t, nt[0] - 1)
    return pl.pallas_call(
        _expert_kernel,
        grid_spec=pltpu.PrefetchScalarGridSpec(
            num_scalar_prefetch=2,
            grid=(rows // EXPERT_TILE,),
            in_specs=[pl.BlockSpec((EXPERT_TILE, W), lambda t, te, nt: (last(t, te, nt), 0)),
                      pl.BlockSpec((None, None, D, 2 * D_EXPERT), lambda t, te, nt: (layer, te[t], 0, 0)),
                      pl.BlockSpec((None, None, 1, 2 * D_EXPERT), lambda t, te, nt: (layer, te[t], 0, 0)),
                      pl.BlockSpec((None, None, D_EXPERT, D), lambda t, te, nt: (layer, te[t], 0, 0)),
                      pl.BlockSpec((None, None, 1, D), lambda t, te, nt: (layer, te[t], 0, 0))],
            out_specs=pl.BlockSpec((EXPERT_TILE, W), lambda t, te, nt: (last(t, te, nt), 0)),
            scratch_shapes=[pltpu.VMEM((D, 2 * D_EXPERT), BF16), pltpu.VMEM((D_EXPERT, D), BF16)]),
        out_shape=jax.ShapeDtypeStruct((rows, W), jnp.uint32),
        compiler_params=_cparams(("arbitrary",)),
        name="moe_experts",
    )(tile_expert, n_tiles, xs, w1, b1, w2, b2)


def _combine_kernel(hs_ref, nch_ref, lo_ref, tot_ref, x_ref, gate_ref, col_ref, g_ref, b_ref, eo_hbm, o_ref,
                    buf_ref, sem):
    i = pl.program_id(0)
    tm = x_ref.shape[0]
    rows = buf_ref.shape[1]
    slot = i % 2

    @pl.when(i == 0)
    def _():
        buf_ref[...] = jnp.zeros_like(buf_ref)
        _chunk_copies(0, hs_ref, nch_ref, lo_ref, eo_hbm, buf_ref.at[0], sem.at[0], False)

    @pl.when(i + 1 < pl.num_programs(0))
    def _():
        _chunk_copies(i + 1, hs_ref, nch_ref, lo_ref, eo_hbm, buf_ref.at[1 - slot], sem.at[1 - slot], False)

    _chunk_waits(i, tot_ref, eo_hbm, buf_ref.at[slot], sem.at[slot], False)

    gates = gate_ref[...]
    cols = col_ref[...]
    lane = lax.broadcasted_iota(jnp.int32, gates.shape, 1)
    r = lax.broadcasted_iota(jnp.int32, (tm, rows), 1).astype(F32)
    w = jnp.zeros((tm, rows), F32)
    for k in range(TOP_K):
        gk = jnp.sum(jnp.where(lane == k, gates, 0.0), -1, keepdims=True)
        ck = jnp.sum(jnp.where(lane == k, cols, 0.0), -1, keepdims=True)
        w = jnp.where(ck == r, gk, w)
    hi, lo = _unpack_halves(buf_ref[slot])
    vals = jnp.concatenate([hi.astype(BF16), lo.astype(BF16)], axis=1)
    y = jnp.dot(w.astype(BF16), vals, preferred_element_type=F32)
    o_ref[...] = _layer_norm(DEEPNORM_ALPHA * x_ref[...] + y, g_ref[...], b_ref[...])


def _combine(x, gates, col_lanes, g, b, chunks, eo, tm):
    N, D = x.shape
    pb = _tile_buffer_rows(tm)
    return pl.pallas_call(
        _combine_kernel,
        grid_spec=pltpu.PrefetchScalarGridSpec(
            num_scalar_prefetch=4,
            grid=(N // tm,),
            in_specs=[pl.BlockSpec((tm, D), lambda i, *_: (i, 0)),
                      pl.BlockSpec((tm, LANES), lambda i, *_: (i, 0)),
                      pl.BlockSpec((tm, LANES), lambda i, *_: (i, 0)),
                      pl.BlockSpec((1, D), lambda i, *_: (0, 0)),
                      pl.BlockSpec((1, D), lambda i, *_: (0, 0)),
                      pl.BlockSpec(memory_space=pl.ANY)],
            out_specs=pl.BlockSpec((tm, D), lambda i, *_: (i, 0)),
            scratch_shapes=[pltpu.VMEM((2, pb, D // 2), jnp.uint32),
                            pltpu.SemaphoreType.DMA((2,))]),
        out_shape=jax.ShapeDtypeStruct((N, D), F32),
        compiler_params=_cparams(("arbitrary",)),
        name="moe_combine",
    )(*chunks, x, gates, col_lanes, g, b, eo)


def _moe(x1, x1b, gates, route, tile_counts, w1, b1, w2, b2, g, b, layer, tm):
    N = x1.shape[0]
    max_tiles = (N * TOP_K + CHUNK * N_EXPERTS * (N // tm)) // EXPERT_TILE + N_EXPERTS
    chunks, col_lanes, col_rows, tile_expert, n_tiles, zero_start, zero_valid = _route_tables(
        route, tile_counts, tm, max_tiles)
    xs = _dispatch(x1b, col_rows, chunks, zero_start, zero_valid, max_tiles * EXPERT_TILE, tm)
    eo = _experts(xs, tile_expert, n_tiles, w1, b1, w2, b2, layer)
    return _combine(x1, gates, col_lanes, g, b, chunks, eo, tm)


def kernel(x, mem, w_in_a, rpb_a, w_in_b, q_norm_b, k_norm_b, w_in_c, conv_w_c, conv_b_c, filt_w1, filt_b1, filt_w2, filt_b2, filt_w3, filt_b3, filt_w4, filt_b4, filt_freq, long_bias_c, w_mem_kv, w_out, ln1_g, ln1_b, ln2_g, ln2_b, router_w, router_b, moe_w1, moe_b1, moe_w2, moe_b2):
    B, L, D = x.shape
    N = B * L
    M = mem.shape[1]
    tm = min(512, L)
    hy = dict(filt_w1=filt_w1, filt_b1=filt_b1, filt_w2=filt_w2, filt_b2=filt_b2, filt_w3=filt_w3,
              filt_b3=filt_b3, filt_w4=filt_w4, filt_b4=filt_b4, filt_freq=filt_freq,
              conv_w_c=conv_w_c, conv_b_c=conv_b_c, long_bias_c=long_bias_c)

    n_layers = w_mem_kv.shape[0]
    wkv = jnp.transpose(w_mem_kv, (1, 0, 2)).reshape(D, n_layers * 2 * MEM_WIDTH).astype(BF16)
    kv_all = _matmul(mem.reshape(B * M, D), wkv, BF16, 512, 512).reshape(B, M, n_layers, 2, MEM_WIDTH)

    w1_all = moe_w1.astype(F32)
    w2_all = moe_w2.astype(F32)
    b1_all = moe_b1.astype(F32)[:, :, None, :]
    b2_all = moe_b2.astype(F32)[:, :, None, :]

    q_cols, k_cols, out_rows_b = _gqa_column_perms()

    xf = x.reshape(N, D).astype(F32)
    for i in range(DEPTH):
        kind = i % N_MIXERS
        j = i // N_MIXERS
        wo = w_out[i]
        if kind == 0:
            proj = _matmul(xf, w_in_a[j].astype(BF16), BF16, tm, 3 * MIX_WIDTH + MEM_WIDTH)
            bias_tab = _na_bias_table(rpb_a[j], L // GRID_W)
            mix = _na_attention(proj.reshape(B, L, -1), bias_tab, B, L).reshape(N, MIX_WIDTH)
            qm_blk = (3 * MIX_WIDTH) // MEM_WIDTH
        elif kind == 1:
            w = w_in_b[j]
            w = jnp.concatenate([w[:, :MIX_WIDTH][:, q_cols],
                                 w[:, MIX_WIDTH:MIX_WIDTH + KV_WIDTH][:, k_cols],
                                 w[:, MIX_WIDTH + KV_WIDTH:]], axis=1)
            proj = _matmul(xf, w.astype(BF16), BF16, tm, w.shape[1])
            cos_t, sin_t = _rope_lane_tables(L)
            dim, _ = _pair_lane_dims()
            gq = q_norm_b[j].astype(F32)[dim][None, :]
            gk = k_norm_b[j].astype(F32)[dim][None, :]
            q_rot, k_rot = _qk_prep(proj, cos_t, sin_t, gq, gk, L, tm)
            mix = _gqa_attention(q_rot.reshape(B, L, -1), k_rot.reshape(B, L, -1), proj.reshape(B, L, -1),
                                 B, L, min(128, L)).reshape(N, MIX_WIDTH)
            wo = jnp.concatenate([wo[:MIX_WIDTH][out_rows_b], wo[MIX_WIDTH:]], axis=0)
            qm_blk = (MIX_WIDTH + 2 * KV_WIDTH) // MEM_WIDTH
        else:
            proj = _matmul(xf, w_in_c[j].astype(BF16), BF16, tm, HYENA_WIDTH + MEM_WIDTH)
            mix = _hyena_mixer(proj.reshape(B, L, -1), j, hy, B, L).reshape(N, MIX_WIDTH)
            qm_blk = HYENA_WIDTH // MEM_WIDTH
        kmem = kv_all[:, :, i, 0, :]
        vmem = kv_all[:, :, i, 1, :]
        rw = jnp.pad(router_w[i].astype(F32), ((0, 0), (0, LANES - N_EXPERTS)))
        rw_hi = rw.astype(BF16)
        rw = jnp.concatenate([rw_hi, (rw - rw_hi.astype(F32)).astype(BF16)], axis=1)
        rb = jnp.pad(router_b[i].astype(F32), (0, LANES - N_EXPERTS), constant_values=NEG_BIG)[None, :]
        x1, x1b, gates, route, counts = _post_mixer(
            xf, mix, proj, qm_blk, kmem, vmem, wo.astype(BF16),
            ln1_g[i].astype(F32)[None, :], ln1_b[i].astype(F32)[None, :], rw, rb, L, tm)
        xf = _moe(x1, x1b, gates, route, counts, w1_all, b1_all, w2_all, b2_all,
                  ln2_g[i].astype(F32)[None, :], ln2_b[i].astype(F32)[None, :], i, tm)
    return xf.reshape(B, L, D).astype(x.dtype)
```
